```python
import math
import jax, jax.numpy as jnp
from jax import lax
import numpy as np

D_MODEL = 2048
BATCH = 4
SEQ = 4096
DEPTH = 4

HEAD_DIM = 128
BR_WIDTH = 1024
N_BRANCH = 3
ROPE_THETA = 500000.0
ROPE_DIM = HEAD_DIM // 4
Q_BLOCK = 128
EPS = 1e-6
NEG_INF = -1e30
FORCE_SCORE = 1e6

DIFF_HEADS = BR_WIDTH // (2 * HEAD_DIM)
DIFF_NORM_EPS = 1e-5
NSA_HEADS = BR_WIDTH // HEAD_DIM
NSA_GROUPS = 2
NSA_REP = NSA_HEADS // NSA_GROUPS
CMP_LEN = 32
CMP_STRIDE = 16
SLC_LEN = 64
SLC_TOPK = 16
WINDOW = 512
SLC_Q_CHUNK = 64
SB_HEADS = BR_WIDTH // HEAD_DIM

NSA_KV = NSA_GROUPS * HEAD_DIM
IN_SPLITS = (
    2 * DIFF_HEADS * HEAD_DIM, 2 * DIFF_HEADS * HEAD_DIM, BR_WIDTH, BR_WIDTH,
    BR_WIDTH, NSA_KV, NSA_KV, NSA_KV, NSA_KV, NSA_KV, NSA_KV, 3 * NSA_HEADS, BR_WIDTH,
    BR_WIDTH, BR_WIDTH, BR_WIDTH, BR_WIDTH,
    N_BRANCH * D_MODEL,
)
N_IN = sum(IN_SPLITS)

kernel_name = "hybrid_diff_nsa_stickbreak_block"


def rmsnorm(x, g, eps=EPS):
    xf = x.astype(jnp.float32)
    y = xf * lax.rsqrt(jnp.mean(xf * xf, axis=-1, keepdims=True) + eps)
    return (y * g.astype(jnp.float32)).astype(x.dtype)


def rope_tables(S):
    pos = jnp.arange(S, dtype=jnp.float32)
    inv = ROPE_THETA ** (-jnp.arange(0, ROPE_DIM, 2, dtype=jnp.float32) / ROPE_DIM)
    ang = pos[:, None] * inv[None, :]
    return jnp.cos(ang), jnp.sin(ang)


def partial_rope(x, cos, sin):
    half = ROPE_DIM // 2
    shape = (1, x.shape[1]) + (1,) * (x.ndim - 3) + (half,)
    cs = cos.reshape(shape).astype(x.dtype)
    sn = sin.reshape(shape).astype(x.dtype)
    x1, x2, xp = x[..., :half], x[..., half:ROPE_DIM], x[..., ROPE_DIM:]
    return jnp.concatenate([x1 * cs - x2 * sn, x2 * cs + x1 * sn, xp], axis=-1)


def diff_attention(q, k, v, lq1, lk1, lq2, lk2, norm_g, layer_idx):
    B, S, H = q.shape[0], q.shape[1], q.shape[2]
    nb = S // Q_BLOCK
    lam_init = 0.8 - 0.6 * math.exp(-0.3 * layer_idx)
    f32 = jnp.float32
    lam = (jnp.exp(jnp.sum(lq1.astype(f32) * lk1.astype(f32)))
           - jnp.exp(jnp.sum(lq2.astype(f32) * lk2.astype(f32))) + lam_init)
    scale = HEAD_DIM ** -0.5
    kpos = jnp.arange(S)
    qb = q.reshape(B, nb, Q_BLOCK, H, 2, HEAD_DIM).transpose(1, 0, 2, 3, 4, 5)

    def block(args):
        qblk, b = args
        s = jnp.einsum('bqhcd,bkhcd->bhcqk', qblk, k).astype(f32) * scale
        qpos = b * Q_BLOCK + jnp.arange(Q_BLOCK)
        s = jnp.where(kpos[None, :] <= qpos[:, None], s, NEG_INF)
        p = jax.nn.softmax(s, axis=-1)
        a = p[:, :, 0] - lam * p[:, :, 1]
        return jnp.einsum('bhqk,bkhe->bqhe', a.astype(v.dtype), v)

    o = lax.map(block, (qb, jnp.arange(nb)))
    o = o.transpose(1, 0, 2, 3, 4).reshape(B, S, H, 2 * HEAD_DIM)
    o = rmsnorm(o, norm_g, eps=DIFF_NORM_EPS) * (1.0 - lam_init)
    return o.reshape(B, S, H * 2 * HEAD_DIM)


def stick_breaking(q, k, v):
    B, S, H = q.shape[0], q.shape[1], q.shape[2]
    nb = S // Q_BLOCK
    scale = HEAD_DIM ** -0.5
    kpos = jnp.arange(S)
    qb = q.reshape(B, nb, Q_BLOCK, H, HEAD_DIM).transpose(1, 0, 2, 3, 4)

    def block(args):
        qblk, b = args
        z = jnp.einsum('bqhd,bkhd->bhqk', qblk, k).astype(jnp.float32) * scale
        qpos = b * Q_BLOCK + jnp.arange(Q_BLOCK)
        strict = kpos[None, :] < qpos[:, None]
        log_1m = jnp.where(strict, jax.nn.log_sigmoid(-z), 0.0)
        after = lax.cumsum(log_1m, axis=3, reverse=True) - log_1m
        a = jnp.where(strict, jnp.exp(jax.nn.log_sigmoid(z) + after), 0.0)
        return jnp.einsum('bhqk,bkhd->bqhd', a.astype(v.dtype), v)

    o = lax.map(block, (qb, jnp.arange(nb)))
    return o.transpose(1, 0, 2, 3, 4).reshape(B, S, H * HEAD_DIM)


def compress(x, pe, w1, w2):
    B, S, G, d = x.shape
    n_cmp = (S - CMP_LEN) // CMP_STRIDE + 1
    idx = CMP_STRIDE * jnp.arange(n_cmp)[:, None] + jnp.arange(CMP_LEN)[None, :]
    blk = x[:, idx] + pe[None, None, :, None, :]
    flat = blk.transpose(0, 1, 3, 2, 4).reshape(B, n_cmp, G, CMP_LEN * d)
    return jax.nn.silu(flat @ w1) @ w2


def nsa_attention(q, kc, vc, ks, vs, kw, vw, gate_logits, pe_k, w1_k, w2_k, pe_v, w1_v, w2_v):
    B, S = q.shape[0], q.shape[1]
    G, R, d = NSA_GROUPS, NSA_REP, HEAD_DIM
    f32 = jnp.float32
    scale = d ** -0.5
    qg = q.reshape(B, S, G, R, d)
    t = jnp.arange(S)

    kcmp = compress(kc, pe_k, w1_k, w2_k)
    vcmp = compress(vc, pe_v, w1_v, w2_v)
    n_cmp = kcmp.shape[1]
    cend = CMP_STRIDE * jnp.arange(n_cmp) + CMP_LEN - 1
    cvalid = cend[None, :] <= t[:, None]
    s = jnp.einsum('btgrd,bngd->bgrtn', qg, kcmp).astype(f32) * scale
    p_cmp = jnp.where(cvalid, jax.nn.softmax(jnp.where(cvalid, s, NEG_INF), axis=-1), 0.0)
    o_cmp = jnp.einsum('bgrtn,bngd->btgrd', p_cmp.astype(vcmp.dtype), vcmp)

    n_slc = S // SLC_LEN
    cstart = CMP_STRIDE * jnp.arange(n_cmp)
    sstart = SLC_LEN * jnp.arange(n_slc)
    overlap = ((cstart[:, None] < sstart[None, :] + SLC_LEN)
               & (cstart[:, None] + CMP_LEN > sstart[None, :])).astype(f32)
    imp = jnp.einsum('bgrtn,nj->bgtj', p_cmp, overlap)
    jj = jnp.arange(n_slc)
    tblk = t // SLC_LEN
    forced = (jj[None, :] == tblk[:, None]) | (jj[None, :] == 0)
    imp = jnp.where(forced, FORCE_SCORE, imp)
    imp = jnp.where(jj[None, :] <= tblk[:, None], imp, NEG_INF)
    n_sel = min(SLC_TOPK, n_slc)
    top_val, top_idx = lax.top_k(imp, n_sel)
    sel_ok = top_val > 0.5 * NEG_INF

    ks_blk = ks.reshape(B, n_slc, SLC_LEN, G, d).transpose(0, 3, 1, 2, 4)
    vs_blk = vs.reshape(B, n_slc, SLC_LEN, G, d).transpose(0, 3, 1, 2, 4)
    C = SLC_Q_CHUNK
    nc = S // C
    q_ch = qg.reshape(B, nc, C, G, R, d).transpose(1, 0, 3, 4, 2, 5)
    i_ch = top_idx.reshape(B, G, nc, C, n_sel).transpose(2, 0, 1, 3, 4)
    ok_ch = sel_ok.reshape(B, G, nc, C, n_sel).transpose(2, 0, 1, 3, 4)
    bi = jnp.arange(B)[:, None, None, None]
    gi = jnp.arange(G)[None, :, None, None]

    def sel_block(args):
        qc, ic, okc, cb = args
        kg = ks_blk[bi, gi, ic]
        vg = vs_blk[bi, gi, ic]
        sc = jnp.einsum('bgrcd,bgcnld->bgrcnl', qc, kg).astype(f32) * scale
        tpos = cb * C + jnp.arange(C)
        kpos = ic[..., None] * SLC_LEN + jnp.arange(SLC_LEN)
        ok = okc[..., None] & (kpos <= tpos[None, None, :, None, None])
        sc = jnp.where(ok[:, :, None], sc, NEG_INF)
        p = jax.nn.softmax(sc.reshape(B, G, R, C, n_sel * SLC_LEN), axis=-1)
        p = p.reshape(B, G, R, C, n_sel, SLC_LEN)
        return jnp.einsum('bgrcnl,bgcnld->bgrcd', p.astype(vg.dtype), vg)

    o_slc = lax.map(sel_block, (q_ch, i_ch, ok_ch, jnp.arange(nc)))
    o_slc = o_slc.transpose(1, 0, 4, 2, 3, 5).reshape(B, S, G, R, d)

    nb = S // Q_BLOCK
    span = WINDOW + Q_BLOCK
    kp = jnp.pad(kw, ((0, 0), (WINDOW, 0), (0, 0), (0, 0)))
    vp = jnp.pad(vw, ((0, 0), (WINDOW, 0), (0, 0), (0, 0)))
    widx = Q_BLOCK * jnp.arange(nb)[:, None] + jnp.arange(span)[None, :]
    kb, vb = kp[:, widx], vp[:, widx]
    qb = qg.reshape(B, nb, Q_BLOCK, G, R, d)
    sw = jnp.einsum('bnqgrd,bnkgd->bgrnqk', qb, kb).astype(f32) * scale
    tpos = Q_BLOCK * jnp.arange(nb)[:, None] + jnp.arange(Q_BLOCK)[None, :]
    kpos = (widx - WINDOW)[:, None, :]
    wok = (kpos <= tpos[:, :, None]) & (kpos > tpos[:, :, None] - WINDOW) & (kpos >= 0)
    pw = jax.nn.softmax(jnp.where(wok, sw, NEG_INF), axis=-1)
    o_win = jnp.einsum('bgrnqk,bnkgd->bnqgrd', pw.astype(vb.dtype), vb).reshape(B, S, G, R, d)

    g = jax.nn.sigmoid(gate_logits.reshape(B, S, G, R, 3))
    o = g[..., 0:1] * o_cmp + g[..., 1:2] * o_slc + g[..., 2:3] * o_win
    return o.reshape(B, S, NSA_HEADS * d)


def setup_inputs(seed: int = 0) -> dict:
    key = jax.random.key(seed)
    ks = jax.random.split(key, 20)
    f32 = jnp.float32

    def nrm(k, shape, s):
        return jax.random.normal(k, shape, f32) * s

    D = D_MODEL
    return {
        "x": nrm(ks[0], (BATCH, SEQ, D), 1.0),
        "c": nrm(ks[1], (BATCH, D), 1.0),
        "norm_pre_g": 1.0 + nrm(ks[2], (DEPTH, D), 0.05),
        "norm_post_g": 1.0 + nrm(ks[3], (DEPTH, D), 0.05),
        "w_ada": nrm(ks[4], (DEPTH, D, 3 * D), 0.5 * D ** -0.5),
        "b_ada": nrm(ks[5], (DEPTH, 3 * D), 0.01),
        "w_in": nrm(ks[6], (DEPTH, D, N_IN), D ** -0.5),
        "lambda_q1": nrm(ks[7], (DEPTH, HEAD_DIM), 0.1),
        "lambda_k1": nrm(ks[8], (DEPTH, HEAD_DIM), 0.1),
        "lambda_q2": nrm(ks[9], (DEPTH, HEAD_DIM), 0.1),
        "lambda_k2": nrm(ks[10], (DEPTH, HEAD_DIM), 0.1),
        "diff_norm_g": 1.0 + nrm(ks[11], (DEPTH, 2 * HEAD_DIM), 0.05),
        "cmp_pe_k": nrm(ks[12], (DEPTH, CMP_LEN, HEAD_DIM), 0.02),
        "cmp_w1_k": nrm(ks[13], (DEPTH, CMP_LEN * HEAD_DIM, HEAD_DIM), (CMP_LEN * HEAD_DIM) ** -0.5),
        "cmp_w2_k": nrm(ks[14], (DEPTH, HEAD_DIM, HEAD_DIM), HEAD_DIM ** -0.5),
        "cmp_pe_v": nrm(ks[15], (DEPTH, CMP_LEN, HEAD_DIM), 0.02),
        "cmp_w1_v": nrm(ks[16], (DEPTH, CMP_LEN * HEAD_DIM, HEAD_DIM), (CMP_LEN * HEAD_DIM) ** -0.5),
        "cmp_w2_v": nrm(ks[17], (DEPTH, HEAD_DIM, HEAD_DIM), HEAD_DIM ** -0.5),
        "w_branch": nrm(ks[18], (DEPTH, N_BRANCH, BR_WIDTH, D), BR_WIDTH ** -0.5),
        "w_out": nrm(ks[19], (DEPTH, D, D), D ** -0.5),
    }


def reference(x, c, norm_pre_g, norm_post_g, w_ada, b_ada, w_in, lambda_q1, lambda_k1,
              lambda_q2, lambda_k2, diff_norm_g, cmp_pe_k, cmp_w1_k, cmp_w2_k,
              cmp_pe_v, cmp_w1_v, cmp_w2_v, w_branch, w_out):
    B, S, D = x.shape
    cos, sin = rope_tables(S)
    split_points = [int(v) for v in np.cumsum(IN_SPLITS)[:-1]]
    for l in range(DEPTH):
        mod = jax.nn.silu(c) @ w_ada[l] + b_ada[l]
        shift, scale, gate = jnp.split(mod, 3, axis=-1)
        h = rmsnorm(x, norm_pre_g[l]) * (1.0 + scale[:, None, :]) + shift[:, None, :]
        (a_q, a_k, a_v, a_z,
         n_q, n_kc, n_vc, n_ks, n_vs, n_kw, n_vw, n_g, n_z,
         sb_q, sb_k, sb_v, sb_z, m_g) = jnp.split(h @ w_in[l], split_points, axis=-1)

        a_q = partial_rope(a_q.reshape(B, S, DIFF_HEADS, 2, HEAD_DIM), cos, sin)
        a_k = partial_rope(a_k.reshape(B, S, DIFF_HEADS, 2, HEAD_DIM), cos, sin)
        a_v = a_v.reshape(B, S, DIFF_HEADS, 2 * HEAD_DIM)
        y_a = diff_attention(a_q, a_k, a_v, lambda_q1[l], lambda_k1[l], lambda_q2[l],
                             lambda_k2[l], diff_norm_g[l], l)

        kvs = lambda t_: t_.reshape(B, S, NSA_GROUPS, HEAD_DIM)
        y_b = nsa_attention(
            partial_rope(n_q.reshape(B, S, NSA_HEADS, HEAD_DIM), cos, sin),
            partial_rope(kvs(n_kc), cos, sin), kvs(n_vc),
            partial_rope(kvs(n_ks), cos, sin), kvs(n_vs),
            partial_rope(kvs(n_kw), cos, sin), kvs(n_vw),
            n_g, cmp_pe_k[l], cmp_w1_k[l], cmp_w2_k[l], cmp_pe_v[l], cmp_w1_v[l], cmp_w2_v[l])

        hs = lambda t_: t_.reshape(B, S, SB_HEADS, HEAD_DIM)
        y_c = stick_breaking(hs(sb_q), hs(sb_k), hs(sb_v))

        ys = jnp.stack([y_a * jax.nn.silu(a_z), y_b * jax.nn.silu(n_z),
                        y_c * jax.nn.silu(sb_z)], axis=0)
        yproj = jnp.einsum('nbsw,nwd->nbsd', ys, w_branch[l])
        mg = jax.nn.sigmoid(m_g.reshape(B, S, N_BRANCH, D))
        merged = jnp.einsum('nbsd,bsnd->bsd', yproj, mg)
        out = rmsnorm(merged @ w_out[l], norm_post_g[l])
        x = x + gate[:, None, :] * out
    return x
```

```python
import functools
import math

import jax
import jax.numpy as jnp
from jax import lax
from jax.experimental import pallas as pl
from jax.experimental.pallas import tpu as pltpu

F32 = jnp.float32
BF16 = jnp.bfloat16

HEAD_DIM = 128
BR_WIDTH = 1024
N_BRANCH = 3
ROPE_THETA = 500000.0
ROPE_DIM = HEAD_DIM // 4
ROPE_HALF = ROPE_DIM // 2
EPS = 1e-6
NEG_INF = -1e30
FORCE_SCORE = 1e6
DIFF_HEADS = BR_WIDTH // (2 * HEAD_DIM)
DIFF_NORM_EPS = 1e-5
NSA_HEADS = BR_WIDTH // HEAD_DIM
NSA_GROUPS = 2
NSA_REP = NSA_HEADS // NSA_GROUPS
CMP_LEN = 32
CMP_STRIDE = 16
SLC_LEN = 64
SLC_TOPK = 16
WINDOW = 512
SB_HEADS = BR_WIDTH // HEAD_DIM
NSA_KV = NSA_GROUPS * HEAD_DIM
ATTN_SCALE = HEAD_DIM ** -0.5

LANES = 128
SLC_BLOCKS_PAD = 128
UNSELECTED_BIAS = 2.0 ** 30
VMEM_LIMIT = 56 * 1024 * 1024

_O_AQ, _O_AK, _O_AV, _O_AZ = 0, 1024, 2048, 3072
_O_NQ, _O_NKC, _O_NVC, _O_NKS, _O_NVS, _O_NKW, _O_NVW = 4096, 5120, 5376, 5632, 5888, 6144, 6400
_O_NG, _O_NZ = 6656, 6680
_O_CQ, _O_CK, _O_CV, _O_CZ, _O_MG = 7704, 8728, 9752, 10776, 11800
N_GATES = 3 * NSA_HEADS
P_AQ, P_AK, P_NQ, P_NKC, P_NKS, P_NKW = 0, 1024, 2048, 3072, 3328, 3584
ROPE_COLS = 4096
P_MG = ROPE_COLS


def _rest_offsets(d_model):
    r0 = ROPE_COLS + N_BRANCH * d_model
    offs = dict(AV=r0, AZ=r0 + 1024, NZ=r0 + 2048, CQ=r0 + 3072, CK=r0 + 4096, CV=r0 + 5120,
                CZ=r0 + 6144, NVC=r0 + 7168, NVS=r0 + 7424, NVW=r0 + 7680, NG=r0 + 7936)
    offs["END"] = r0 + 7936 + NSA_GROUPS * LANES
    return offs


def _permute_w_in(w, d_model, n_total):
    d = w.shape[0]
    z = lambda n: jnp.zeros((d, n), w.dtype)
    gates_per_group = N_GATES // NSA_GROUPS
    segs = [w[:, _O_AQ:_O_AQ + 1024], w[:, _O_AK:_O_AK + 1024], w[:, _O_NQ:_O_NQ + 1024],
            w[:, _O_NKC:_O_NKC + 256], w[:, _O_NKS:_O_NKS + 256], w[:, _O_NKW:_O_NKW + 256],
            z(ROPE_COLS - 3840),
            w[:, _O_MG:_O_MG + N_BRANCH * d_model],
            w[:, _O_AV:_O_AV + 1024], w[:, _O_AZ:_O_AZ + 1024], w[:, _O_NZ:_O_NZ + 1024],
            w[:, _O_CQ:_O_CQ + 4096],
            w[:, _O_NVC:_O_NVC + 256], w[:, _O_NVS:_O_NVS + 256], w[:, _O_NVW:_O_NVW + 256]]
    for g in range(NSA_GROUPS):
        segs += [w[:, _O_NG + g * gates_per_group:_O_NG + (g + 1) * gates_per_group],
                 z(LANES - gates_per_group)]
    used = _rest_offsets(d_model)["END"]
    if n_total > used:
        segs.append(z(n_total - used))
    return jnp.concatenate(segs, axis=1).astype(BF16)


def _rope_tables(seq):
    pos = jnp.arange(seq, dtype=F32)
    inv = ROPE_THETA ** (-jnp.arange(0, ROPE_DIM, 2, dtype=F32) / ROPE_DIM)
    ang = pos[:, None] * inv[None, :]
    cos, sin = jnp.cos(ang), jnp.sin(ang)
    ones = jnp.ones((seq, HEAD_DIM - ROPE_DIM), F32)
    zh = jnp.zeros((seq, ROPE_HALF), F32)
    zr = jnp.zeros((seq, HEAD_DIM - ROPE_DIM), F32)
    c = jnp.concatenate([cos, cos, ones], axis=1)
    s1 = jnp.concatenate([-sin, zh, zr], axis=1)
    s2 = jnp.concatenate([zh, sin, zr], axis=1)
    return c, s1, s2


def _sigmoid(x):
    return 1.0 / (1.0 + jnp.exp(-x))


def _silu(x):
    return x * _sigmoid(x)


def _dot_nt(a, b):
    return lax.dot_general(a, b, (((1,), (1,)), ((), ())), preferred_element_type=F32)


def _dot(a, b):
    return jnp.dot(a, b, preferred_element_type=F32)


def _params(*sem):
    return pltpu.CompilerParams(dimension_semantics=sem, vmem_limit_bytes=VMEM_LIMIT)


def _ada_kernel(c_ref, w_ref, b_ref, o_ref):
    c = c_ref[...]
    o_ref[0] = _dot(_silu(c).astype(BF16), w_ref[0].astype(BF16)) + b_ref[0]


def _ada(c, w_ada, b_ada):
    depth, d, n3 = w_ada.shape
    bsz = c.shape[0]
    rows = max(8, bsz)
    cp = jnp.zeros((rows, d), F32).at[:bsz].set(c)
    tn = math.gcd(1024, n3)
    out = pl.pallas_call(
        _ada_kernel,
        grid=(depth, n3 // tn),
        in_specs=[pl.BlockSpec((rows, d), lambda l, j: (0, 0)),
                  pl.BlockSpec((1, d, tn), lambda l, j: (l, 0, j)),
                  pl.BlockSpec((1, 1, tn), lambda l, j: (l, 0, j))],
        out_specs=pl.BlockSpec((1, rows, tn), lambda l, j: (l, 0, j)),
        out_shape=jax.ShapeDtypeStruct((depth, rows, n3), F32),
        compiler_params=_params("arbitrary", "arbitrary"),
        name="ada_mod",
    )(cp, w_ada, b_ada.reshape(depth, 1, n3))
    return out[:, :bsz]


def _inproj_kernel(x_ref, g_ref, sc_ref, sh_ref, c_ref, s1_ref, s2_ref, w_ref, o_ref, h_ref,
                   *, n_rope_tiles, tn):
    j = pl.program_id(1)

    @pl.when(j == 0)
    def _():
        x = x_ref[...]
        y = x * lax.rsqrt(jnp.mean(x * x, axis=-1, keepdims=True) + EPS) * g_ref[...]
        h_ref[...] = (y * (1.0 + sc_ref[0]) + sh_ref[0]).astype(BF16)

    y = _dot(h_ref[...], w_ref[...])

    @pl.when(j < n_rope_tiles)
    def _():
        c, s1, s2 = c_ref[...], s1_ref[...], s2_ref[...]
        for hh in range(tn // HEAD_DIM):
            sl = slice(hh * HEAD_DIM, (hh + 1) * HEAD_DIM)
            yh = y[:, sl]
            r = (yh * c + pltpu.roll(yh, HEAD_DIM - ROPE_HALF, 1) * s1
                 + pltpu.roll(yh, ROPE_HALF, 1) * s2)
            o_ref[:, sl] = r.astype(BF16)

    @pl.when(j >= n_rope_tiles)
    def _():
        o_ref[...] = y.astype(BF16)


def _inproj(xf, g, scale, shift, tables, wp, seq, tm, tn):
    rows, d = xf.shape
    n_total = wp.shape[1]
    tpb = seq // tm
    c, s1, s2 = tables
    tab_spec = pl.BlockSpec((tm, HEAD_DIM), lambda i, j: (i % tpb, 0))
    mod_spec = pl.BlockSpec((1, 1, d), lambda i, j: (i // tpb, 0, 0))
    return pl.pallas_call(
        functools.partial(_inproj_kernel, n_rope_tiles=ROPE_COLS // tn, tn=tn),
        grid=(rows // tm, n_total // tn),
        in_specs=[pl.BlockSpec((tm, d), lambda i, j: (i, 0)),
                  pl.BlockSpec((1, d), lambda i, j: (0, 0)),
                  mod_spec, mod_spec, tab_spec, tab_spec, tab_spec,
                  pl.BlockSpec((d, tn), lambda i, j: (0, j))],
        out_specs=pl.BlockSpec((tm, tn), lambda i, j: (i, j)),
        out_shape=jax.ShapeDtypeStruct((rows, n_total), BF16),
        scratch_shapes=[pltpu.VMEM((tm, d), BF16)],
        compiler_params=_params("arbitrary", "arbitrary"),
        name="in_proj",
    )(xf, g.reshape(1, d), scale[:, None, :], shift[:, None, :], c, s1, s2, wp)


def _softmax_update(carry, s, v):
    m, l, acc = carry
    m_new = jnp.maximum(m, jnp.max(s, axis=-1, keepdims=True))
    alpha = jnp.exp(m - m_new)
    p = jnp.exp(s - m_new)
    l = alpha * l + jnp.sum(p, axis=-1, keepdims=True)
    acc = alpha * acc + _dot(p.astype(BF16), v)
    return m_new, l, acc


def _softmax_init(rows, width):
    return (jnp.full((rows, 1), NEG_INF, F32), jnp.zeros((rows, 1), F32),
            jnp.zeros((rows, width), F32))


def _diff_kernel(q_ref, k_ref, v_ref, z_ref, ng_ref, lq1_ref, lk1_ref, lq2_ref, lk2_ref, o_ref,
                 *, tq, tk, lam_init):
    i = pl.program_id(2)
    t0 = i * tq
    q = q_ref[...]
    qs = (q[:, :HEAD_DIM], q[:, HEAD_DIM:])
    n_full = t0 // tk

    def step(jt, carry, masked):
        k0 = pl.multiple_of(jt * tk, tk)
        k = k_ref[pl.ds(k0, tk), :]
        v = v_ref[pl.ds(k0, tk), :]
        out = []
        for cc in range(2):
            s = _dot_nt(qs[cc], k[:, cc * HEAD_DIM:(cc + 1) * HEAD_DIM]) * ATTN_SCALE
            if masked:
                kpos = k0 + lax.broadcasted_iota(jnp.int32, (tq, tk), 1)
                tpos = t0 + lax.broadcasted_iota(jnp.int32, (tq, tk), 0)
                s = jnp.where(kpos <= tpos, s, NEG_INF)
            out.append(_softmax_update(carry[cc], s, v))
        return tuple(out)

    init = (_softmax_init(tq, 2 * HEAD_DIM), _softmax_init(tq, 2 * HEAD_DIM))
    carry = lax.fori_loop(0, n_full, lambda jt, c: step(jt, c, False), init)
    carry = step(n_full, carry, True)

    lam = (jnp.exp(jnp.sum(lq1_ref[...] * lk1_ref[...], axis=-1, keepdims=True))
           - jnp.exp(jnp.sum(lq2_ref[...] * lk2_ref[...], axis=-1, keepdims=True)) + lam_init)
    o0 = carry[0][2] / carry[0][1]
    o1 = carry[1][2] / carry[1][1]
    o = o0 - lam * o1
    o = o * lax.rsqrt(jnp.mean(o * o, axis=-1, keepdims=True) + DIFF_NORM_EPS) * ng_ref[...]
    o = o * (1.0 - lam_init)
    o_ref[...] = (o * _silu(z_ref[...].astype(F32))).astype(BF16)


def _diff_attn(proj, offs, norm_g, lq1, lk1, lq2, lk2, layer_idx, bsz, seq, tq, tk):
    nq = seq // tq
    w = 2 * HEAD_DIM
    lam_init = 0.8 - 0.6 * math.exp(-0.3 * layer_idx)
    vec = pl.BlockSpec((1, HEAD_DIM), lambda b, h, i: (0, 0))
    return pl.pallas_call(
        functools.partial(_diff_kernel, tq=tq, tk=tk, lam_init=lam_init),
        grid=(bsz, DIFF_HEADS, nq),
        in_specs=[pl.BlockSpec((tq, w), lambda b, h, i: (b * nq + i, P_AQ // w + h)),
                  pl.BlockSpec((seq, w), lambda b, h, i: (b, P_AK // w + h)),
                  pl.BlockSpec((seq, w), lambda b, h, i: (b, offs["AV"] // w + h)),
                  pl.BlockSpec((tq, w), lambda b, h, i: (b * nq + i, offs["AZ"] // w + h)),
                  pl.BlockSpec((1, w), lambda b, h, i: (0, 0)),
                  vec, vec, vec, vec],
        out_specs=pl.BlockSpec((tq, w), lambda b, h, i: (b * nq + i, h)),
        out_shape=jax.ShapeDtypeStruct((bsz * seq, BR_WIDTH), BF16),
        compiler_params=_params("arbitrary", "arbitrary", "arbitrary"),
        name="diff_attn",
    )(proj, proj, proj, proj, norm_g.reshape(1, w), lq1.reshape(1, -1), lk1.reshape(1, -1),
      lq2.reshape(1, -1), lk2.reshape(1, -1))


def _sb_kernel(q_ref, k_ref, v_ref, z_ref, o_ref, *, tq, tk):
    i = pl.program_id(2)
    t0 = i * tq
    q = q_ref[...]
    n_full = t0 // tk
    n_diag = max(1, tq // tk)
    upper = jnp.where(lax.broadcasted_iota(jnp.int32, (tk, tk), 0)
                      > lax.broadcasted_iota(jnp.int32, (tk, tk), 1), 1.0, 0.0).astype(BF16)

    def step(jt, carry, masked):
        later, acc = carry
        k0 = pl.multiple_of(jt * tk, tk)
        k = k_ref[pl.ds(k0, tk), :]
        v = v_ref[pl.ds(k0, tk), :]
        z = _dot_nt(q, k) * ATTN_SCALE
        log_1m = -(jnp.maximum(z, 0.0) + jnp.log(1.0 + jnp.exp(-jnp.abs(z))))
        if masked:
            strict = (k0 + lax.broadcasted_iota(jnp.int32, (tq, tk), 1)
                      < t0 + lax.broadcasted_iota(jnp.int32, (tq, tk), 0))
            log_1m = jnp.where(strict, log_1m, 0.0)
        hi = log_1m.astype(BF16)
        lo = (log_1m - hi.astype(F32)).astype(BF16)
        after = _dot(hi, upper) + _dot(lo, upper) + later
        a = jnp.exp(z + log_1m + after)
        if masked:
            a = jnp.where(strict, a, 0.0)
        acc = acc + _dot(a.astype(BF16), v)
        later = later + jnp.sum(log_1m, axis=-1, keepdims=True)
        return later, acc

    carry = (jnp.zeros((tq, 1), F32), jnp.zeros((tq, HEAD_DIM), F32))
    for dd in range(n_diag - 1, -1, -1):
        carry = step(n_full + dd, carry, True)
    carry = lax.fori_loop(0, n_full, lambda n, c: step(n_full - 1 - n, c, False), carry)
    o_ref[...] = (carry[1] * _silu(z_ref[...].astype(F32))).astype(BF16)


def _sb_attn(proj, offs, bsz, seq, tq, tk):
    nq = seq // tq
    w = HEAD_DIM
    return pl.pallas_call(
        functools.partial(_sb_kernel, tq=tq, tk=tk),
        grid=(bsz, SB_HEADS, nq),
        in_specs=[pl.BlockSpec((tq, w), lambda b, h, i: (b * nq + i, offs["CQ"] // w + h)),
                  pl.BlockSpec((seq, w), lambda b, h, i: (b, offs["CK"] // w + h)),
                  pl.BlockSpec((seq, w), lambda b, h, i: (b, offs["CV"] // w + h)),
                  pl.BlockSpec((tq, w), lambda b, h, i: (b * nq + i, offs["CZ"] // w + h))],
        out_specs=pl.BlockSpec((tq, w), lambda b, h, i: (b * nq + i, h)),
        out_shape=jax.ShapeDtypeStruct((bsz * seq, BR_WIDTH), BF16),
        compiler_params=_params("arbitrary", "arbitrary", "arbitrary"),
        name="stick_breaking",
    )(proj, proj, proj, proj)


def _compress_kernel(x_ref, pe_ref, w1_ref, w2_ref, o_ref, *, nc):
    half = CMP_STRIDE * HEAD_DIM
    top = (x_ref[0, 0, pl.ds(0, nc), :] + pe_ref[0:1, :]).astype(BF16)
    bot = (x_ref[0, 0, pl.ds(1, nc), :] + pe_ref[1:2, :]).astype(BF16)
    hid = _dot(top, w1_ref[0:half, :]) + _dot(bot, w1_ref[half:2 * half, :])
    o_ref[0, 0] = _dot(_silu(hid).astype(BF16), w2_ref[...]).astype(BF16)


def _compress(cols, pe, w1, w2, bsz, seq):
    nc = seq // CMP_STRIDE
    half = CMP_STRIDE * HEAD_DIM
    x = cols.reshape(bsz, nc, CMP_STRIDE, NSA_GROUPS, HEAD_DIM).transpose(0, 3, 1, 2, 4)
    x = x.reshape(bsz, NSA_GROUPS, nc, half).astype(F32)
    x = jnp.pad(x, ((0, 0), (0, 0), (0, 8), (0, 0)))
    return pl.pallas_call(
        functools.partial(_compress_kernel, nc=nc),
        grid=(bsz, NSA_GROUPS),
        in_specs=[pl.BlockSpec((1, 1, nc + 8, half), lambda b, g: (b, g, 0, 0)),
                  pl.BlockSpec((2, half), lambda b, g: (0, 0)),
                  pl.BlockSpec((2 * half, HEAD_DIM), lambda b, g: (0, 0)),
                  pl.BlockSpec((HEAD_DIM, HEAD_DIM), lambda b, g: (0, 0))],
        out_specs=pl.BlockSpec((1, 1, nc, HEAD_DIM), lambda b, g: (b, g, 0, 0)),
        out_shape=jax.ShapeDtypeStruct((bsz, NSA_GROUPS, nc, HEAD_DIM), BF16),
        compiler_params=_params("arbitrary", "arbitrary"),
        name="nsa_compress",
    )(x, pe.reshape(2, half), w1.astype(BF16), w2.astype(BF16))


def _nsa_kernel(q_ref, kc_ref, vc_ref, ks_ref, vs_ref, kw_ref, vw_ref, g_ref, z_ref, o_ref, imp_ref,
                *, tq, tk, seq):
    i = pl.program_id(2)
    t0 = i * tq
    nc = seq // CMP_STRIDE
    nb = seq // SLC_LEN
    nbp = SLC_BLOCKS_PAD
    rep = NSA_REP
    q = q_ref[...]
    qh = [q[:, r * HEAD_DIM:(r + 1) * HEAD_DIM] for r in range(rep)]

    kc, vc = kc_ref[0, 0], vc_ref[0, 0]
    tpos_c = t0 + lax.broadcasted_iota(jnp.int32, (tq, nc), 0)
    ncol = lax.broadcasted_iota(jnp.int32, (tq, nc), 1)
    cvalid = CMP_STRIDE * ncol + (CMP_LEN - 1) <= tpos_c
    psum = jnp.zeros((tq, nc), F32)
    o_cmp = []
    for r in range(rep):
        s = jnp.where(cvalid, _dot_nt(qh[r], kc) * ATTN_SCALE, NEG_INF)
        e = jnp.where(cvalid, jnp.exp(s - jnp.max(s, axis=-1, keepdims=True)), 0.0)
        den = jnp.sum(e, axis=-1, keepdims=True)
        p = e / jnp.where(den > 0.0, den, 1.0)
        psum = psum + p
        o_cmp.append(_dot(p.astype(BF16), vc))

    jrow = lax.broadcasted_iota(jnp.int32, (nbp, nc), 0)
    ncol2 = lax.broadcasted_iota(jnp.int32, (nbp, nc), 1)
    overlap = jnp.where((CMP_STRIDE * ncol2 < SLC_LEN * jrow + SLC_LEN)
                        & (CMP_STRIDE * ncol2 + CMP_LEN > SLC_LEN * jrow)
                        & (ncol2 < nc - 1) & (jrow < nb), 1.0, 0.0).astype(BF16)
    p1 = psum.astype(BF16)
    r1 = psum - p1.astype(F32)
    p2 = r1.astype(BF16)
    p3 = (r1 - p2.astype(F32)).astype(BF16)
    imp = _dot_nt(overlap, p1) + _dot_nt(overlap, p2) + _dot_nt(overlap, p3)
    jt_ = lax.broadcasted_iota(jnp.int32, (nbp, tq), 0)
    tblk = (t0 + lax.broadcasted_iota(jnp.int32, (nbp, tq), 1)) // SLC_LEN
    imp = jnp.where((jt_ == tblk) | (jt_ == 0), FORCE_SCORE, imp)
    imp = jnp.where(jt_ <= tblk, imp, NEG_INF)
    imp_ref[...] = imp

    def rank_body(jp, cnt):
        row = imp_ref[pl.ds(jp, 1), :]
        beats = (row > imp) | ((row == imp) & (jp < jt_))
        return cnt + jnp.where(beats, 1.0, 0.0)

    cnt = lax.fori_loop(0, nb, rank_body, jnp.zeros((nbp, tq), F32))
    keep = (cnt < float(min(SLC_TOPK, nb))) & (imp > 0.5 * NEG_INF)
    unselected = jnp.where(keep, 0.0, 1.0).T.astype(BF16)

    rows = rep * tq
    q_sel = jnp.concatenate([jnp.concatenate([qh[r], unselected], axis=1) for r in range(rep)], axis=0)
    q_win = jnp.concatenate(qh, axis=0)
    row_t = t0 + (lax.broadcasted_iota(jnp.int32, (rows, tk), 0) & (tq - 1))
    col = lax.broadcasted_iota(jnp.int32, (rows, tk), 1)

    def slc_step(jt, carry, masked):
        k0 = pl.multiple_of(jt * tk, tk)
        k = ks_ref[pl.ds(k0, tk), :]
        v = vs_ref[pl.ds(k0, tk), :]
        kblk = (k0 + lax.broadcasted_iota(jnp.int32, (tk, nbp), 0)) // SLC_LEN
        bias = jnp.where(lax.broadcasted_iota(jnp.int32, (tk, nbp), 1) == kblk,
                         -UNSELECTED_BIAS, 0.0).astype(BF16)
        s = _dot_nt(q_sel, jnp.concatenate([k, bias], axis=1)) * ATTN_SCALE
        if masked:
            s = jnp.where(k0 + col <= row_t, s, NEG_INF)
        return _softmax_update(carry, s, v)

    n_full = t0 // tk
    carry = lax.fori_loop(0, n_full, lambda jt, c: slc_step(jt, c, False), _softmax_init(rows, HEAD_DIM))
    m_s, l_s, acc_s = slc_step(n_full, carry, True)
    o_slc = acc_s / l_s

    def win_step(jt, carry):
        k0 = pl.multiple_of(jt * tk, tk)
        k = kw_ref[pl.ds(k0, tk), :]
        v = vw_ref[pl.ds(k0, tk), :]
        kpos = k0 + col
        ok = (kpos <= row_t) & (kpos > row_t - WINDOW)
        s = jnp.where(ok, _dot_nt(q_win, k) * ATTN_SCALE, NEG_INF)
        return _softmax_update(carry, s, v)

    lo = jnp.maximum(t0 - (WINDOW - 1), 0) // tk
    hi = (t0 + tq - 1) // tk
    m_w, l_w, acc_w = lax.fori_loop(lo, hi + 1, win_step, _softmax_init(rows, HEAD_DIM))
    o_win = acc_w / l_w

    gates = _sigmoid(g_ref[...].astype(F32))
    z = z_ref[...].astype(F32)
    for r in range(rep):
        rs = slice(r * tq, (r + 1) * tq)
        cs = slice(r * HEAD_DIM, (r + 1) * HEAD_DIM)
        o = (gates[:, 3 * r:3 * r + 1] * o_cmp[r] + gates[:, 3 * r + 1:3 * r + 2] * o_slc[rs]
             + gates[:, 3 * r + 2:3 * r + 3] * o_win[rs])
        o_ref[:, cs] = (o * _silu(z[:, cs])).astype(BF16)


def _nsa_attn(proj, offs, kcmp, vcmp, bsz, seq, tq, tk):
    nq = seq // tq
    nc = seq // CMP_STRIDE
    w = HEAD_DIM
    gw = NSA_REP * HEAD_DIM
    assert seq // SLC_LEN <= SLC_BLOCKS_PAD and tq & (tq - 1) == 0 and tk % tq == 0
    kv = lambda off: pl.BlockSpec((seq, w), lambda b, g, i: (b, off // w + g))
    cmp_spec = pl.BlockSpec((1, 1, nc, w), lambda b, g, i: (b, g, 0, 0))
    return pl.pallas_call(
        functools.partial(_nsa_kernel, tq=tq, tk=tk, seq=seq),
        grid=(bsz, NSA_GROUPS, nq),
        in_specs=[pl.BlockSpec((tq, gw), lambda b, g, i: (b * nq + i, P_NQ // gw + g)),
                  cmp_spec, cmp_spec,
                  kv(P_NKS), kv(offs["NVS"]), kv(P_NKW), kv(offs["NVW"]),
                  pl.BlockSpec((tq, LANES), lambda b, g, i: (b * nq + i, offs["NG"] // LANES + g)),
                  pl.BlockSpec((tq, gw), lambda b, g, i: (b * nq + i, offs["NZ"] // gw + g))],
        out_specs=pl.BlockSpec((tq, gw), lambda b, g, i: (b * nq + i, g)),
        out_shape=jax.ShapeDtypeStruct((bsz * seq, BR_WIDTH), BF16),
        scratch_shapes=[pltpu.VMEM((SLC_BLOCKS_PAD, tq), F32)],
        compiler_params=_params("arbitrary", "arbitrary", "arbitrary"),
        name="nsa_attn",
    )(proj, kcmp, vcmp, proj, proj, proj, proj, proj, proj)


def _merge_kernel(x_ref, ya_ref, yb_ref, yc_ref, mg0_ref, mg1_ref, mg2_ref, wb_ref, wo_ref, gp_ref,
                  gate_ref, o_ref):
    merged = None
    for n, (y_ref, mg_ref) in enumerate(((ya_ref, mg0_ref), (yb_ref, mg1_ref), (yc_ref, mg2_ref))):
        t = _dot(y_ref[...], wb_ref[n]) * _sigmoid(mg_ref[...].astype(F32))
        merged = t if merged is None else merged + t
    o = _dot(merged.astype(BF16), wo_ref[...])
    o = o * lax.rsqrt(jnp.mean(o * o, axis=-1, keepdims=True) + EPS) * gp_ref[...]
    o_ref[...] = x_ref[...] + gate_ref[0] * o


def _merge(xf, ya, yb, yc, proj, wb, wo, g_post, gate, seq, tm):
    rows, d = xf.shape
    tpb = seq // tm
    row = lambda w_: pl.BlockSpec((tm, w_), lambda i: (i, 0))
    mg = lambda n: pl.BlockSpec((tm, d), lambda i: (i, P_MG // d + n))
    const = pl.Buffered(1)
    return pl.pallas_call(
        _merge_kernel,
        grid=(rows // tm,),
        in_specs=[row(d), row(BR_WIDTH), row(BR_WIDTH), row(BR_WIDTH), mg(0), mg(1), mg(2),
                  pl.BlockSpec((N_BRANCH, BR_WIDTH, d), lambda i: (0, 0, 0), pipeline_mode=const),
                  pl.BlockSpec((d, d), lambda i: (0, 0), pipeline_mode=const),
                  pl.BlockSpec((1, d), lambda i: (0, 0)),
                  pl.BlockSpec((1, 1, d), lambda i: (i // tpb, 0, 0))],
        out_specs=row(d),
        out_shape=jax.ShapeDtypeStruct((rows, d), F32),
        compiler_params=_params("arbitrary"),
        name="merge_out",
    )(xf, ya, yb, yc, proj, proj, proj, wb.astype(BF16), wo.astype(BF16), g_post.reshape(1, d),
      gate[:, None, :])


def _tiles(seq, d_model):
    tm_in = min(1024, seq)
    tn_in = 1024
    tq = min(128, seq)
    return dict(tm_in=tm_in, tn_in=tn_in, tq=tq, tk=tq, tm_merge=min(256, seq))


def kernel(x, c, norm_pre_g, norm_post_g, w_ada, b_ada, w_in, lambda_q1, lambda_k1, lambda_q2,
           lambda_k2, diff_norm_g, cmp_pe_k, cmp_w1_k, cmp_w2_k, cmp_pe_v, cmp_w1_v, cmp_w2_v,
           w_branch, w_out):
    bsz, seq, d = x.shape
    depth = w_in.shape[0]
    t = _tiles(seq, d)
    offs = _rest_offsets(d)
    assert d % 512 == 0 and ROPE_COLS % d == 0 and seq % t["tm_in"] == 0 and seq % 128 == 0
    n_total = -(-offs["END"] // t["tn_in"]) * t["tn_in"]
    tables = _rope_tables(seq)
    mod = _ada(c, w_ada, b_ada)
    xf = x.reshape(bsz * seq, d)
    for l in range(depth):
        shift, scale, gate = jnp.split(mod[l], 3, axis=-1)
        wp = _permute_w_in(w_in[l], d, n_total)
        proj = _inproj(xf, norm_pre_g[l], scale, shift, tables, wp, seq, t["tm_in"], t["tn_in"])
        ya = _diff_attn(proj, offs, diff_norm_g[l], lambda_q1[l], lambda_k1[l], lambda_q2[l],
                        lambda_k2[l], l, bsz, seq, t["tq"], t["tk"])
        kcmp = _compress(proj[:, P_NKC:P_NKC + NSA_KV], cmp_pe_k[l], cmp_w1_k[l], cmp_w2_k[l], bsz, seq)
        vcmp = _compress(proj[:, offs["NVC"]:offs["NVC"] + NSA_KV], cmp_pe_v[l], cmp_w1_v[l],
                         cmp_w2_v[l], bsz, seq)
        yb = _nsa_attn(proj, offs, kcmp, vcmp, bsz, seq, t["tq"], t["tk"])
        yc = _sb_attn(proj, offs, bsz, seq, t["tq"], t["tk"])
        xf = _merge(xf, ya, yb, yc, proj, w_branch[l], w_out[l], norm_post_g[l], gate, seq,
                    t["tm_merge"])
    return xf.reshape(bsz, seq, d)
```

```python
import functools
import math

import jax
import jax.numpy as jnp
from jax import lax
from jax.experimental import pallas as pl
from jax.experimental.pallas import tpu as pltpu

F32 = jnp.float32
BF16 = jnp.bfloat16

HEAD_DIM = 128
BR_WIDTH = 1024
N_BRANCH = 3
ROPE_THETA = 500000.0
ROPE_DIM = HEAD_DIM // 4
ROPE_HALF = ROPE_DIM // 2
EPS = 1e-6
NEG_INF = -1e30
FORCE_SCORE = 1e6
DIFF_HEADS = BR_WIDTH // (2 * HEAD_DIM)
DIFF_NORM_EPS = 1e-5
NSA_HEADS = BR_WIDTH // HEAD_DIM
NSA_GROUPS = 2
NSA_REP = NSA_HEADS // NSA_GROUPS
CMP_LEN = 32
CMP_STRIDE = 16
SLC_LEN = 64
SLC_SHIFT = SLC_LEN.bit_length() - 1
SLC_TOPK = 16
WINDOW = 512
SB_HEADS = BR_WIDTH // HEAD_DIM
NSA_KV = NSA_GROUPS * HEAD_DIM
ATTN_SCALE = HEAD_DIM ** -0.5

LANES = 128
SLC_BLOCKS_PAD = 128
UNSELECTED_BIAS = 2.0 ** 30
VMEM_LIMIT = 56 * 1024 * 1024

_O_AQ, _O_AK, _O_AV, _O_AZ = 0, 1024, 2048, 3072
_O_NQ, _O_NKC, _O_NVC, _O_NKS, _O_NVS, _O_NKW, _O_NVW = 4096, 5120, 5376, 5632, 5888, 6144, 6400
_O_NG, _O_NZ = 6656, 6680
_O_CQ, _O_CK, _O_CV, _O_CZ, _O_MG = 7704, 8728, 9752, 10776, 11800
N_GATES = 3 * NSA_HEADS
P_AQ, P_AK, P_NQ, P_NKC, P_NKS, P_NKW = 0, 1024, 2048, 3072, 3328, 3584
ROPE_COLS = 4096
P_MG = ROPE_COLS


def _rest_offsets(d_model):
    r0 = ROPE_COLS + N_BRANCH * d_model
    offs = dict(AV=r0, AZ=r0 + 1024, NZ=r0 + 2048, CQ=r0 + 3072, CK=r0 + 4096, CV=r0 + 5120,
                CZ=r0 + 6144, NVC=r0 + 7168, NVS=r0 + 7424, NVW=r0 + 7680, NG=r0 + 7936)
    offs["END"] = r0 + 7936 + NSA_GROUPS * LANES
    return offs


def _permute_w_in(w, d_model, n_total):
    d = w.shape[0]
    z = lambda n: jnp.zeros((d, n), w.dtype)
    gates_per_group = N_GATES // NSA_GROUPS
    segs = [w[:, _O_AQ:_O_AQ + 1024], w[:, _O_AK:_O_AK + 1024], w[:, _O_NQ:_O_NQ + 1024],
            w[:, _O_NKC:_O_NKC + 256], w[:, _O_NKS:_O_NKS + 256], w[:, _O_NKW:_O_NKW + 256],
            z(ROPE_COLS - 3840),
            w[:, _O_MG:_O_MG + N_BRANCH * d_model],
            w[:, _O_AV:_O_AV + 1024], w[:, _O_AZ:_O_AZ + 1024], w[:, _O_NZ:_O_NZ + 1024],
            w[:, _O_CQ:_O_CQ + 4096],
            w[:, _O_NVC:_O_NVC + 256], w[:, _O_NVS:_O_NVS + 256], w[:, _O_NVW:_O_NVW + 256]]
    for g in range(NSA_GROUPS):
        segs += [w[:, _O_NG + g * gates_per_group:_O_NG + (g + 1) * gates_per_group],
                 z(LANES - gates_per_group)]
    used = _rest_offsets(d_model)["END"]
    if n_total > used:
        segs.append(z(n_total - used))
    return jnp.concatenate(segs, axis=1).astype(BF16)


def _rope_tables(seq):
    pos = jnp.arange(seq, dtype=F32)
    inv = ROPE_THETA ** (-jnp.arange(0, ROPE_DIM, 2, dtype=F32) / ROPE_DIM)
    ang = pos[:, None] * inv[None, :]
    cos, sin = jnp.cos(ang), jnp.sin(ang)
    ones = jnp.ones((seq, HEAD_DIM - ROPE_DIM), F32)
    zh = jnp.zeros((seq, ROPE_HALF), F32)
    zr = jnp.zeros((seq, HEAD_DIM - ROPE_DIM), F32)
    c = jnp.concatenate([cos, cos, ones], axis=1)
    s1 = jnp.concatenate([-sin, zh, zr], axis=1)
    s2 = jnp.concatenate([zh, sin, zr], axis=1)
    return c, s1, s2


def _sigmoid(x):
    return 1.0 / (1.0 + jnp.exp(-x))


def _silu(x):
    return x * _sigmoid(x)


def _dot_nt(a, b):
    return lax.dot_general(a, b, (((1,), (1,)), ((), ())), preferred_element_type=F32)


def _dot(a, b):
    return jnp.dot(a, b, preferred_element_type=F32)


def _params(*sem):
    return pltpu.CompilerParams(dimension_semantics=sem, vmem_limit_bytes=VMEM_LIMIT)


def _ada_kernel(c_ref, w_ref, b_ref, o_ref):
    c = c_ref[...]
    o_ref[0] = _dot(_silu(c).astype(BF16), w_ref[0].astype(BF16)) + b_ref[0]


def _ada(c, w_ada, b_ada):
    depth, d, n3 = w_ada.shape
    bsz = c.shape[0]
    rows = max(8, bsz)
    cp = jnp.zeros((rows, d), F32).at[:bsz].set(c)
    tn = math.gcd(1024, n3)
    out = pl.pallas_call(
        _ada_kernel,
        grid=(depth, n3 // tn),
        in_specs=[pl.BlockSpec((rows, d), lambda l, j: (0, 0)),
                  pl.BlockSpec((1, d, tn), lambda l, j: (l, 0, j)),
                  pl.BlockSpec((1, 1, tn), lambda l, j: (l, 0, j))],
        out_specs=pl.BlockSpec((1, rows, tn), lambda l, j: (l, 0, j)),
        out_shape=jax.ShapeDtypeStruct((depth, rows, n3), F32),
        compiler_params=_params("arbitrary", "arbitrary"),
        name="ada_mod",
    )(cp, w_ada, b_ada.reshape(depth, 1, n3))
    return out[:, :bsz]


def _inproj_kernel(x_ref, g_ref, sc_ref, sh_ref, c_ref, s1_ref, s2_ref, w_ref, o_ref, h_ref,
                   *, n_rope_tiles, tn):
    j = pl.program_id(1)

    @pl.when(j == 0)
    def _():
        x = x_ref[...]
        y = x * lax.rsqrt(jnp.mean(x * x, axis=-1, keepdims=True) + EPS) * g_ref[...]
        h_ref[...] = (y * (1.0 + sc_ref[0]) + sh_ref[0]).astype(BF16)

    y = _dot(h_ref[...], w_ref[...])

    @pl.when(j < n_rope_tiles)
    def _():
        c, s1, s2 = c_ref[...], s1_ref[...], s2_ref[...]
        for hh in range(tn // HEAD_DIM):
            sl = slice(hh * HEAD_DIM, (hh + 1) * HEAD_DIM)
            yh = y[:, sl]
            r = (yh * c + pltpu.roll(yh, HEAD_DIM - ROPE_HALF, 1) * s1
                 + pltpu.roll(yh, ROPE_HALF, 1) * s2)
            o_ref[:, sl] = r.astype(BF16)

    @pl.when(j >= n_rope_tiles)
    def _():
        o_ref[...] = y.astype(BF16)


def _inproj(xf, g, scale, shift, tables, wp, seq, tm, tn):
    rows, d = xf.shape
    n_total = wp.shape[1]
    tpb = seq // tm
    c, s1, s2 = tables
    tab_spec = pl.BlockSpec((tm, HEAD_DIM), lambda i, j: (i % tpb, 0))
    mod_spec = pl.BlockSpec((1, 1, d), lambda i, j: (i // tpb, 0, 0))
    return pl.pallas_call(
        functools.partial(_inproj_kernel, n_rope_tiles=ROPE_COLS // tn, tn=tn),
        grid=(rows // tm, n_total // tn),
        in_specs=[pl.BlockSpec((tm, d), lambda i, j: (i, 0)),
                  pl.BlockSpec((1, d), lambda i, j: (0, 0)),
                  mod_spec, mod_spec, tab_spec, tab_spec, tab_spec,
                  pl.BlockSpec((d, tn), lambda i, j: (0, j))],
        out_specs=pl.BlockSpec((tm, tn), lambda i, j: (i, j)),
        out_shape=jax.ShapeDtypeStruct((rows, n_total), BF16),
        scratch_shapes=[pltpu.VMEM((tm, d), BF16)],
        compiler_params=_params("arbitrary", "arbitrary"),
        name="in_proj",
    )(xf, g.reshape(1, d), scale[:, None, :], shift[:, None, :], c, s1, s2, wp)


def _softmax_update(carry, s, v):
    m, l, acc = carry
    m_new = jnp.maximum(m, jnp.max(s, axis=-1, keepdims=True))
    alpha = jnp.exp(m - m_new)
    p = jnp.exp(s - m_new)
    l = alpha * l + jnp.sum(p, axis=-1, keepdims=True)
    acc = alpha * acc + _dot(p.astype(BF16), v)
    return m_new, l, acc


def _softmax_init(rows, width):
    return (jnp.full((rows, 1), NEG_INF, F32), jnp.zeros((rows, 1), F32),
            jnp.zeros((rows, width), F32))


def _diff_kernel(q_ref, k_ref, v_ref, z_ref, ng_ref, lq1_ref, lk1_ref, lq2_ref, lk2_ref, o_ref,
                 *, tq, tk, lam_init):
    i = pl.program_id(2)
    t0 = i * tq
    q = q_ref[...]
    qs = (q[:, :HEAD_DIM], q[:, HEAD_DIM:])
    n_full = t0 // tk

    def step(jt, carry, masked):
        k0 = pl.multiple_of(jt * tk, tk)
        k = k_ref[pl.ds(k0, tk), :]
        v = v_ref[pl.ds(k0, tk), :]
        out = []
        for cc in range(2):
            s = _dot_nt(qs[cc], k[:, cc * HEAD_DIM:(cc + 1) * HEAD_DIM]) * ATTN_SCALE
            if masked:
                kpos = k0 + lax.broadcasted_iota(jnp.int32, (tq, tk), 1)
                tpos = t0 + lax.broadcasted_iota(jnp.int32, (tq, tk), 0)
                s = jnp.where(kpos <= tpos, s, NEG_INF)
            out.append(_softmax_update(carry[cc], s, v))
        return tuple(out)

    init = (_softmax_init(tq, 2 * HEAD_DIM), _softmax_init(tq, 2 * HEAD_DIM))
    carry = lax.fori_loop(0, n_full, lambda jt, c: step(jt, c, False), init)
    carry = step(n_full, carry, True)

    lam = (jnp.exp(jnp.sum(lq1_ref[...] * lk1_ref[...], axis=-1, keepdims=True))
           - jnp.exp(jnp.sum(lq2_ref[...] * lk2_ref[...], axis=-1, keepdims=True)) + lam_init)
    o0 = carry[0][2] / carry[0][1]
    o1 = carry[1][2] / carry[1][1]
    o = o0 - lam * o1
    o = o * lax.rsqrt(jnp.mean(o * o, axis=-1, keepdims=True) + DIFF_NORM_EPS) * ng_ref[...]
    o = o * (1.0 - lam_init)
    o_ref[...] = (o * _silu(z_ref[...].astype(F32))).astype(BF16)


def _diff_attn(proj, offs, norm_g, lq1, lk1, lq2, lk2, layer_idx, bsz, seq, tq, tk):
    nq = seq // tq
    w = 2 * HEAD_DIM
    lam_init = 0.8 - 0.6 * math.exp(-0.3 * layer_idx)
    vec = pl.BlockSpec((1, HEAD_DIM), lambda b, h, i: (0, 0))
    return pl.pallas_call(
        functools.partial(_diff_kernel, tq=tq, tk=tk, lam_init=lam_init),
        grid=(bsz, DIFF_HEADS, nq),
        in_specs=[pl.BlockSpec((tq, w), lambda b, h, i: (b * nq + i, P_AQ // w + h)),
                  pl.BlockSpec((seq, w), lambda b, h, i: (b, P_AK // w + h)),
                  pl.BlockSpec((seq, w), lambda b, h, i: (b, offs["AV"] // w + h)),
                  pl.BlockSpec((tq, w), lambda b, h, i: (b * nq + i, offs["AZ"] // w + h)),
                  pl.BlockSpec((1, w), lambda b, h, i: (0, 0)),
                  vec, vec, vec, vec],
        out_specs=pl.BlockSpec((tq, w), lambda b, h, i: (b * nq + i, h)),
        out_shape=jax.ShapeDtypeStruct((bsz * seq, BR_WIDTH), BF16),
        compiler_params=_params("arbitrary", "arbitrary", "arbitrary"),
        name="diff_attn",
    )(proj, proj, proj, proj, norm_g.reshape(1, w), lq1.reshape(1, -1), lk1.reshape(1, -1),
      lq2.reshape(1, -1), lk2.reshape(1, -1))


def _sb_kernel(q_ref, k_ref, v_ref, z_ref, o_ref, *, tq, tk):
    i = pl.program_id(2)
    t0 = i * tq
    q = q_ref[...]
    n_full = t0 // tk
    n_diag = max(1, tq // tk)
    upper2 = jnp.where((lax.broadcasted_iota(jnp.int32, (2 * tk, tk), 0) & (tk - 1))
                       > lax.broadcasted_iota(jnp.int32, (2 * tk, tk), 1), 1.0, 0.0).astype(BF16)

    def step(jt, carry, masked):
        later, acc = carry
        k0 = pl.multiple_of(jt * tk, tk)
        k = k_ref[pl.ds(k0, tk), :]
        v = v_ref[pl.ds(k0, tk), :]
        z = _dot_nt(q, k) * ATTN_SCALE
        log_1m = -(jnp.maximum(z, 0.0) + jnp.log(1.0 + jnp.exp(-jnp.abs(z))))
        if masked:
            strict = (k0 + lax.broadcasted_iota(jnp.int32, (tq, tk), 1)
                      < t0 + lax.broadcasted_iota(jnp.int32, (tq, tk), 0))
            log_1m = jnp.where(strict, log_1m, 0.0)
        hi = log_1m.astype(BF16)
        lo = (log_1m - hi.astype(F32)).astype(BF16)
        after = _dot(jnp.concatenate([hi, lo], axis=1), upper2) + later
        a = jnp.exp(z + log_1m + after)
        if masked:
            a = jnp.where(strict, a, 0.0)
        acc = acc + _dot(a.astype(BF16), v)
        later = later + jnp.sum(log_1m, axis=-1, keepdims=True)
        return later, acc

    carry = (jnp.zeros((tq, 1), F32), jnp.zeros((tq, HEAD_DIM), F32))
    for dd in range(n_diag - 1, -1, -1):
        carry = step(n_full + dd, carry, True)
    carry = lax.fori_loop(0, n_full, lambda n, c: step(n_full - 1 - n, c, False), carry)
    o_ref[...] = (carry[1] * _silu(z_ref[...].astype(F32))).astype(BF16)


def _sb_attn(proj, offs, bsz, seq, tq, tk):
    nq = seq // tq
    w = HEAD_DIM
    return pl.pallas_call(
        functools.partial(_sb_kernel, tq=tq, tk=tk),
        grid=(bsz, SB_HEADS, nq),
        in_specs=[pl.BlockSpec((tq, w), lambda b, h, i: (b * nq + i, offs["CQ"] // w + h)),
                  pl.BlockSpec((seq, w), lambda b, h, i: (b, offs["CK"] // w + h)),
                  pl.BlockSpec((seq, w), lambda b, h, i: (b, offs["CV"] // w + h)),
                  pl.BlockSpec((tq, w), lambda b, h, i: (b * nq + i, offs["CZ"] // w + h))],
        out_specs=pl.BlockSpec((tq, w), lambda b, h, i: (b * nq + i, h)),
        out_shape=jax.ShapeDtypeStruct((bsz * seq, BR_WIDTH), BF16),
        compiler_params=_params("arbitrary", "arbitrary", "arbitrary"),
        name="stick_breaking",
    )(proj, proj, proj, proj)


def _compress_kernel(x_ref, pe_ref, w1_ref, w2_ref, o_ref, *, nc):
    half = CMP_STRIDE * HEAD_DIM
    top = (x_ref[0, 0, pl.ds(0, nc), :] + pe_ref[0:1, :]).astype(BF16)
    bot = (x_ref[0, 0, pl.ds(1, nc), :] + pe_ref[1:2, :]).astype(BF16)
    hid = _dot(top, w1_ref[0:half, :]) + _dot(bot, w1_ref[half:2 * half, :])
    o_ref[0, 0] = _dot(_silu(hid).astype(BF16), w2_ref[...]).astype(BF16)


def _compress(cols, pe, w1, w2, bsz, seq):
    nc = seq // CMP_STRIDE
    half = CMP_STRIDE * HEAD_DIM
    x = cols.reshape(bsz, nc, CMP_STRIDE, NSA_GROUPS, HEAD_DIM).transpose(0, 3, 1, 2, 4)
    x = x.reshape(bsz, NSA_GROUPS, nc, half).astype(F32)
    x = jnp.pad(x, ((0, 0), (0, 0), (0, 8), (0, 0)))
    return pl.pallas_call(
        functools.partial(_compress_kernel, nc=nc),
        grid=(bsz, NSA_GROUPS),
        in_specs=[pl.BlockSpec((1, 1, nc + 8, half), lambda b, g: (b, g, 0, 0)),
                  pl.BlockSpec((2, half), lambda b, g: (0, 0)),
                  pl.BlockSpec((2 * half, HEAD_DIM), lambda b, g: (0, 0)),
                  pl.BlockSpec((HEAD_DIM, HEAD_DIM), lambda b, g: (0, 0))],
        out_specs=pl.BlockSpec((1, 1, nc, HEAD_DIM), lambda b, g: (b, g, 0, 0)),
        out_shape=jax.ShapeDtypeStruct((bsz, NSA_GROUPS, nc, HEAD_DIM), BF16),
        compiler_params=_params("arbitrary", "arbitrary"),
        name="nsa_compress",
    )(x, pe.reshape(2, half), w1.astype(BF16), w2.astype(BF16))


def _nsa_kernel(q_ref, kc_ref, vc_ref, ks_ref, vs_ref, kw_ref, vw_ref, g_ref, z_ref, o_ref, imp_ref,
                *, tq, tk, seq):
    i = pl.program_id(2)
    t0 = i * tq
    nc = seq // CMP_STRIDE
    nb = seq // SLC_LEN
    nbp = SLC_BLOCKS_PAD
    rep = NSA_REP
    q = q_ref[...]
    qh = [q[:, r * HEAD_DIM:(r + 1) * HEAD_DIM] for r in range(rep)]

    kc, vc = kc_ref[0, 0], vc_ref[0, 0]
    tpos_c = t0 + lax.broadcasted_iota(jnp.int32, (tq, nc), 0)
    ncol = lax.broadcasted_iota(jnp.int32, (tq, nc), 1)
    cvalid = CMP_STRIDE * ncol + (CMP_LEN - 1) <= tpos_c
    psum = jnp.zeros((tq, nc), F32)
    o_cmp = []
    for r in range(rep):
        s = jnp.where(cvalid, _dot_nt(qh[r], kc) * ATTN_SCALE, NEG_INF)
        e = jnp.where(cvalid, jnp.exp(s - jnp.max(s, axis=-1, keepdims=True)), 0.0)
        den = jnp.sum(e, axis=-1, keepdims=True)
        p = e / jnp.where(den > 0.0, den, 1.0)
        psum = psum + p
        o_cmp.append(_dot(p.astype(BF16), vc))

    nb8 = imp_ref.shape[0]
    jrow = lax.broadcasted_iota(jnp.int32, (nb8, nc), 0)
    ncol2 = lax.broadcasted_iota(jnp.int32, (nb8, nc), 1)
    overlap = jnp.where((CMP_STRIDE * ncol2 < SLC_LEN * jrow + SLC_LEN)
                        & (CMP_STRIDE * ncol2 + CMP_LEN > SLC_LEN * jrow)
                        & (ncol2 < nc - 1) & (jrow < nb), 1.0, 0.0).astype(BF16)
    p1 = psum.astype(BF16)
    r1 = psum - p1.astype(F32)
    p2 = r1.astype(BF16)
    p3 = (r1 - p2.astype(F32)).astype(BF16)
    imp = _dot_nt(overlap, p1) + _dot_nt(overlap, p2) + _dot_nt(overlap, p3)
    jt_ = lax.broadcasted_iota(jnp.int32, (nb8, tq), 0)
    tblk = jnp.right_shift(t0 + lax.broadcasted_iota(jnp.int32, (nb8, tq), 1), SLC_SHIFT)
    imp = jnp.where((jt_ == tblk) | (jt_ == 0), FORCE_SCORE, imp)
    imp = jnp.where(jt_ <= tblk, imp, NEG_INF)
    imp_ref[...] = imp

    def rank_body(jp, cnt):
        row = imp_ref[pl.ds(jp, 1), :]
        beats = (row > imp) | ((row == imp) & (jp < jt_))
        return cnt + jnp.where(beats, 1.0, 0.0)

    n_live = jnp.minimum((t0 + tq - 1) // SLC_LEN + 1, nb)
    cnt = lax.fori_loop(0, n_live, rank_body, jnp.zeros((nb8, tq), F32))
    keep = (cnt < float(min(SLC_TOPK, nb))) & (imp > 0.5 * NEG_INF)
    unsel_t = jnp.where(keep, 0.0, 1.0)
    if nbp > nb8:
        unsel_t = jnp.concatenate([unsel_t, jnp.zeros((nbp - nb8, tq), F32)], axis=0)
    unselected = unsel_t.T.astype(BF16)

    rows = rep * tq
    q_sel = jnp.concatenate([jnp.concatenate([qh[r], unselected], axis=1) for r in range(rep)], axis=0)
    q_win = jnp.concatenate(qh, axis=0)

    def slc_step(jt, carry, masked):
        k0 = pl.multiple_of(jt * tk, tk)
        k = ks_ref[pl.ds(k0, tk), :]
        v = vs_ref[pl.ds(k0, tk), :]
        kblk = jnp.right_shift(k0 + lax.broadcasted_iota(jnp.int32, (tk, nbp), 0), SLC_SHIFT)
        bias = jnp.where(lax.broadcasted_iota(jnp.int32, (tk, nbp), 1) == kblk,
                         -UNSELECTED_BIAS, 0.0).astype(BF16)
        s = _dot_nt(q_sel, jnp.concatenate([k, bias], axis=1)) * ATTN_SCALE
        if masked:
            row_t = t0 + (lax.broadcasted_iota(jnp.int32, (rows, tk), 0) & (tq - 1))
            s = jnp.where(k0 + lax.broadcasted_iota(jnp.int32, (rows, tk), 1) <= row_t, s, NEG_INF)
        return _softmax_update(carry, s, v)

    n_full = t0 // tk
    carry = lax.fori_loop(0, n_full, lambda jt, c: slc_step(jt, c, False), _softmax_init(rows, HEAD_DIM))
    m_s, l_s, acc_s = slc_step(n_full, carry, True)
    o_slc = acc_s / l_s

    span = min(WINDOW + tq, seq)
    start = pl.multiple_of(jnp.maximum(t0 - WINDOW, 0), tq)
    kpos = start + lax.broadcasted_iota(jnp.int32, (rows, span), 1)
    row_w = t0 + (lax.broadcasted_iota(jnp.int32, (rows, span), 0) & (tq - 1))
    ok = (kpos <= row_w) & (kpos > row_w - WINDOW)
    s = jnp.where(ok, _dot_nt(q_win, kw_ref[pl.ds(start, span), :]) * ATTN_SCALE, NEG_INF)
    p = jnp.exp(s - jnp.max(s, axis=-1, keepdims=True))
    o_win = _dot(p.astype(BF16), vw_ref[pl.ds(start, span), :]) / jnp.sum(p, axis=-1, keepdims=True)

    gates = _sigmoid(g_ref[...].astype(F32))
    z = z_ref[...].astype(F32)
    for r in range(rep):
        rs = slice(r * tq, (r + 1) * tq)
        cs = slice(r * HEAD_DIM, (r + 1) * HEAD_DIM)
        o = (gates[:, 3 * r:3 * r + 1] * o_cmp[r] + gates[:, 3 * r + 1:3 * r + 2] * o_slc[rs]
             + gates[:, 3 * r + 2:3 * r + 3] * o_win[rs])
        o_ref[:, cs] = (o * _silu(z[:, cs])).astype(BF16)


def _nsa_attn(proj, offs, kcmp, vcmp, bsz, seq, tq, tk):
    nq = seq // tq
    nc = seq // CMP_STRIDE
    w = HEAD_DIM
    gw = NSA_REP * HEAD_DIM
    assert seq // SLC_LEN <= SLC_BLOCKS_PAD and tq & (tq - 1) == 0 and tk % tq == 0
    kv = lambda off: pl.BlockSpec((seq, w), lambda b, g, i: (b, off // w + g))
    cmp_spec = pl.BlockSpec((1, 1, nc, w), lambda b, g, i: (b, g, 0, 0))
    return pl.pallas_call(
        functools.partial(_nsa_kernel, tq=tq, tk=tk, seq=seq),
        grid=(bsz, NSA_GROUPS, nq),
        in_specs=[pl.BlockSpec((tq, gw), lambda b, g, i: (b * nq + i, P_NQ // gw + g)),
                  cmp_spec, cmp_spec,
                  kv(P_NKS), kv(offs["NVS"]), kv(P_NKW), kv(offs["NVW"]),
                  pl.BlockSpec((tq, LANES), lambda b, g, i: (b * nq + i, offs["NG"] // LANES + g)),
                  pl.BlockSpec((tq, gw), lambda b, g, i: (b * nq + i, offs["NZ"] // gw + g))],
        out_specs=pl.BlockSpec((tq, gw), lambda b, g, i: (b * nq + i, g)),
        out_shape=jax.ShapeDtypeStruct((bsz * seq, BR_WIDTH), BF16),
        scratch_shapes=[pltpu.VMEM((-(-(seq // SLC_LEN) // 8) * 8, tq), F32)],
        compiler_params=_params("arbitrary", "arbitrary", "arbitrary"),
        name="nsa_attn",
    )(proj, kcmp, vcmp, proj, proj, proj, proj, proj, proj)


def _merge_kernel(x_ref, ya_ref, yb_ref, yc_ref, mg0_ref, mg1_ref, mg2_ref, wb_ref, wo_ref, gp_ref,
                  gate_ref, o_ref):
    merged = None
    for n, (y_ref, mg_ref) in enumerate(((ya_ref, mg0_ref), (yb_ref, mg1_ref), (yc_ref, mg2_ref))):
        t = _dot(y_ref[...], wb_ref[n]) * _sigmoid(mg_ref[...].astype(F32))
        merged = t if merged is None else merged + t
    o = _dot(merged.astype(BF16), wo_ref[...])
    o = o * lax.rsqrt(jnp.mean(o * o, axis=-1, keepdims=True) + EPS) * gp_ref[...]
    o_ref[...] = x_ref[...] + gate_ref[0] * o


def _merge(xf, ya, yb, yc, proj, wb, wo, g_post, gate, seq, tm):
    rows, d = xf.shape
    tpb = seq // tm
    row = lambda w_: pl.BlockSpec((tm, w_), lambda i: (i, 0))
    mg = lambda n: pl.BlockSpec((tm, d), lambda i: (i, P_MG // d + n))
    const = pl.Buffered(1)
    return pl.pallas_call(
        _merge_kernel,
        grid=(rows // tm,),
        in_specs=[row(d), row(BR_WIDTH), row(BR_WIDTH), row(BR_WIDTH), mg(0), mg(1), mg(2),
                  pl.BlockSpec((N_BRANCH, BR_WIDTH, d), lambda i: (0, 0, 0), pipeline_mode=const),
                  pl.BlockSpec((d, d), lambda i: (0, 0), pipeline_mode=const),
                  pl.BlockSpec((1, d), lambda i: (0, 0)),
                  pl.BlockSpec((1, 1, d), lambda i: (i // tpb, 0, 0))],
        out_specs=row(d),
        out_shape=jax.ShapeDtypeStruct((rows, d), F32),
        compiler_params=_params("arbitrary"),
        name="merge_out",
    )(xf, ya, yb, yc, proj, proj, proj, wb.astype(BF16), wo.astype(BF16), g_post.reshape(1, d),
      gate[:, None, :])


def _tiles(seq, d_model):
    return dict(tm_in=min(1024, seq), tn_in=1024, tm_merge=min(256, seq),
                diff=(min(256, seq), min(512, seq)),
                sb=(min(512, seq), min(256, seq)),
                nsa=(min(128, seq), min(512, seq)))


def kernel(x, c, norm_pre_g, norm_post_g, w_ada, b_ada, w_in, lambda_q1, lambda_k1, lambda_q2,
           lambda_k2, diff_norm_g, cmp_pe_k, cmp_w1_k, cmp_w2_k, cmp_pe_v, cmp_w1_v, cmp_w2_v,
           w_branch, w_out):
    bsz, seq, d = x.shape
    depth = w_in.shape[0]
    t = _tiles(seq, d)
    offs = _rest_offsets(d)
    assert d % 512 == 0 and ROPE_COLS % d == 0 and seq % t["tm_in"] == 0 and seq % 128 == 0
    n_total = -(-offs["END"] // t["tn_in"]) * t["tn_in"]
    tables = _rope_tables(seq)
    mod = _ada(c, w_ada, b_ada)
    xf = x.reshape(bsz * seq, d)
    for l in range(depth):
        shift, scale, gate = jnp.split(mod[l], 3, axis=-1)
        wp = _permute_w_in(w_in[l], d, n_total)
        proj = _inproj(xf, norm_pre_g[l], scale, shift, tables, wp, seq, t["tm_in"], t["tn_in"])
        ya = _diff_attn(proj, offs, diff_norm_g[l], lambda_q1[l], lambda_k1[l], lambda_q2[l],
                        lambda_k2[l], l, bsz, seq, *t["diff"])
        kcmp = _compress(proj[:, P_NKC:P_NKC + NSA_KV], cmp_pe_k[l], cmp_w1_k[l], cmp_w2_k[l], bsz, seq)
        vcmp = _compress(proj[:, offs["NVC"]:offs["NVC"] + NSA_KV], cmp_pe_v[l], cmp_w1_v[l],
                         cmp_w2_v[l], bsz, seq)
        yb = _nsa_attn(proj, offs, kcmp, vcmp, bsz, seq, *t["nsa"])
        yc = _sb_attn(proj, offs, bsz, seq, *t["sb"])
        xf = _merge(xf, ya, yb, yc, proj, w_branch[l], w_out[l], norm_post_g[l], gate, seq,
                    t["tm_merge"])
    return xf.reshape(bsz, seq, d)
```

```python
import functools
import math

import jax
import jax.numpy as jnp
from jax import lax
from jax.experimental import pallas as pl
from jax.experimental.pallas import tpu as pltpu

F32 = jnp.float32
BF16 = jnp.bfloat16

HEAD_DIM = 128
BR_WIDTH = 1024
N_BRANCH = 3
ROPE_THETA = 500000.0
ROPE_DIM = HEAD_DIM // 4
ROPE_HALF = ROPE_DIM // 2
EPS = 1e-6
NEG_INF = -1e30
FORCE_SCORE = 1e6
DIFF_HEADS = BR_WIDTH // (2 * HEAD_DIM)
DIFF_NORM_EPS = 1e-5
NSA_HEADS = BR_WIDTH // HEAD_DIM
NSA_GROUPS = 2
NSA_REP = NSA_HEADS // NSA_GROUPS
CMP_LEN = 32
CMP_STRIDE = 16
SLC_LEN = 64
SLC_SHIFT = SLC_LEN.bit_length() - 1
SLC_TOPK = 16
WINDOW = 512
SB_HEADS = BR_WIDTH // HEAD_DIM
NSA_KV = NSA_GROUPS * HEAD_DIM
ATTN_SCALE = HEAD_DIM ** -0.5

LANES = 128
SLC_BLOCKS_PAD = 128
UNSELECTED_BIAS = 2.0 ** 30
SB_UNDERFLOW = -104.0
VMEM_LIMIT = 56 * 1024 * 1024

_O_AQ, _O_AK, _O_AV, _O_AZ = 0, 1024, 2048, 3072
_O_NQ, _O_NKC, _O_NVC, _O_NKS, _O_NVS, _O_NKW, _O_NVW = 4096, 5120, 5376, 5632, 5888, 6144, 6400
_O_NG, _O_NZ = 6656, 6680
_O_CQ, _O_CK, _O_CV, _O_CZ, _O_MG = 7704, 8728, 9752, 10776, 11800
N_GATES = 3 * NSA_HEADS
P_AQ, P_AK, P_NQ, P_NKC, P_NKS, P_NKW = 0, 1024, 2048, 3072, 3328, 3584
ROPE_COLS = 4096
P_MG = ROPE_COLS


def _rest_offsets(d_model):
    r0 = ROPE_COLS + N_BRANCH * d_model
    offs = dict(AV=r0, AZ=r0 + 1024, NZ=r0 + 2048, CQ=r0 + 3072, CK=r0 + 4096, CV=r0 + 5120,
                CZ=r0 + 6144, NVC=r0 + 7168, NVS=r0 + 7424, NVW=r0 + 7680, NG=r0 + 7936)
    offs["END"] = r0 + 7936 + NSA_GROUPS * LANES
    return offs


def _permute_w_in(w, d_model, n_total):
    d = w.shape[0]
    z = lambda n: jnp.zeros((d, n), w.dtype)
    gates_per_group = N_GATES // NSA_GROUPS
    segs = [w[:, _O_AQ:_O_AQ + 1024], w[:, _O_AK:_O_AK + 1024], w[:, _O_NQ:_O_NQ + 1024],
            w[:, _O_NKC:_O_NKC + 256], w[:, _O_NKS:_O_NKS + 256], w[:, _O_NKW:_O_NKW + 256],
            z(ROPE_COLS - 3840),
            w[:, _O_MG:_O_MG + N_BRANCH * d_model],
            w[:, _O_AV:_O_AV + 1024], w[:, _O_AZ:_O_AZ + 1024], w[:, _O_NZ:_O_NZ + 1024],
            w[:, _O_CQ:_O_CQ + 4096],
            w[:, _O_NVC:_O_NVC + 256], w[:, _O_NVS:_O_NVS + 256], w[:, _O_NVW:_O_NVW + 256]]
    for g in range(NSA_GROUPS):
        segs += [w[:, _O_NG + g * gates_per_group:_O_NG + (g + 1) * gates_per_group],
                 z(LANES - gates_per_group)]
    used = _rest_offsets(d_model)["END"]
    if n_total > used:
        segs.append(z(n_total - used))
    return jnp.concatenate(segs, axis=1).astype(BF16)


def _rope_tables(seq):
    pos = jnp.arange(seq, dtype=F32)
    inv = ROPE_THETA ** (-jnp.arange(0, ROPE_DIM, 2, dtype=F32) / ROPE_DIM)
    ang = pos[:, None] * inv[None, :]
    cos, sin = jnp.cos(ang), jnp.sin(ang)
    ones = jnp.ones((seq, HEAD_DIM - ROPE_DIM), F32)
    zh = jnp.zeros((seq, ROPE_HALF), F32)
    zr = jnp.zeros((seq, HEAD_DIM - ROPE_DIM), F32)
    c = jnp.concatenate([cos, cos, ones], axis=1)
    s1 = jnp.concatenate([-sin, zh, zr], axis=1)
    s2 = jnp.concatenate([zh, sin, zr], axis=1)
    return c, s1, s2


def _sigmoid(x):
    return 1.0 / (1.0 + jnp.exp(-x))


def _silu(x):
    return x * _sigmoid(x)


def _dot_nt(a, b):
    return lax.dot_general(a, b, (((1,), (1,)), ((), ())), preferred_element_type=F32)


def _dot(a, b):
    return jnp.dot(a, b, preferred_element_type=F32)


def _params(*sem):
    return pltpu.CompilerParams(dimension_semantics=sem, vmem_limit_bytes=VMEM_LIMIT)


def _ada_kernel(c_ref, w_ref, b_ref, o_ref):
    c = c_ref[...]
    o_ref[0] = _dot(_silu(c).astype(BF16), w_ref[0].astype(BF16)) + b_ref[0]


def _ada(c, w_ada, b_ada):
    depth, d, n3 = w_ada.shape
    bsz = c.shape[0]
    rows = max(8, bsz)
    cp = jnp.zeros((rows, d), F32).at[:bsz].set(c)
    tn = math.gcd(1024, n3)
    out = pl.pallas_call(
        _ada_kernel,
        grid=(depth, n3 // tn),
        in_specs=[pl.BlockSpec((rows, d), lambda l, j: (0, 0)),
                  pl.BlockSpec((1, d, tn), lambda l, j: (l, 0, j)),
                  pl.BlockSpec((1, 1, tn), lambda l, j: (l, 0, j))],
        out_specs=pl.BlockSpec((1, rows, tn), lambda l, j: (l, 0, j)),
        out_shape=jax.ShapeDtypeStruct((depth, rows, n3), F32),
        compiler_params=_params("arbitrary", "arbitrary"),
        name="ada_mod",
    )(cp, w_ada, b_ada.reshape(depth, 1, n3))
    return out[:, :bsz]


def _inproj_kernel(x_ref, g_ref, sc_ref, sh_ref, c_ref, s1_ref, s2_ref, w_ref, o_ref, h_ref,
                   *, n_rope_tiles, tn):
    j = pl.program_id(1)

    @pl.when(j == 0)
    def _():
        x = x_ref[...]
        y = x * lax.rsqrt(jnp.mean(x * x, axis=-1, keepdims=True) + EPS) * g_ref[...]
        h_ref[...] = (y * (1.0 + sc_ref[0]) + sh_ref[0]).astype(BF16)

    y = _dot(h_ref[...], w_ref[...])

    @pl.when(j < n_rope_tiles)
    def _():
        c, s1, s2 = c_ref[...], s1_ref[...], s2_ref[...]
        for hh in range(tn // HEAD_DIM):
            sl = slice(hh * HEAD_DIM, (hh + 1) * HEAD_DIM)
            yh = y[:, sl]
            r = (yh * c + pltpu.roll(yh, HEAD_DIM - ROPE_HALF, 1) * s1
                 + pltpu.roll(yh, ROPE_HALF, 1) * s2)
            o_ref[:, sl] = r.astype(BF16)

    @pl.when(j >= n_rope_tiles)
    def _():
        o_ref[...] = y.astype(BF16)


def _inproj(xf, g, scale, shift, tables, wp, seq, tm, tn):
    rows, d = xf.shape
    n_total = wp.shape[1]
    tpb = seq // tm
    c, s1, s2 = tables
    tab_spec = pl.BlockSpec((tm, HEAD_DIM), lambda i, j: (i % tpb, 0))
    mod_spec = pl.BlockSpec((1, 1, d), lambda i, j: (i // tpb, 0, 0))
    return pl.pallas_call(
        functools.partial(_inproj_kernel, n_rope_tiles=ROPE_COLS // tn, tn=tn),
        grid=(rows // tm, n_total // tn),
        in_specs=[pl.BlockSpec((tm, d), lambda i, j: (i, 0)),
                  pl.BlockSpec((1, d), lambda i, j: (0, 0)),
                  mod_spec, mod_spec, tab_spec, tab_spec, tab_spec,
                  pl.BlockSpec((d, tn), lambda i, j: (0, j))],
        out_specs=pl.BlockSpec((tm, tn), lambda i, j: (i, j)),
        out_shape=jax.ShapeDtypeStruct((rows, n_total), BF16),
        scratch_shapes=[pltpu.VMEM((tm, d), BF16)],
        compiler_params=_params("arbitrary", "arbitrary"),
        name="in_proj",
    )(xf, g.reshape(1, d), scale[:, None, :], shift[:, None, :], c, s1, s2, wp)


def _softmax_update(carry, s, v):
    m, l, acc = carry
    m_new = jnp.maximum(m, jnp.max(s, axis=-1, keepdims=True))
    alpha = jnp.exp(m - m_new)
    p = jnp.exp(s - m_new)
    l = alpha * l + jnp.sum(p, axis=-1, keepdims=True)
    acc = alpha * acc + _dot(p.astype(BF16), v)
    return m_new, l, acc


def _softmax_init(rows, width):
    return (jnp.full((rows, 1), NEG_INF, F32), jnp.zeros((rows, 1), F32),
            jnp.zeros((rows, width), F32))


def _diff_kernel(q_ref, k_ref, v_ref, z_ref, ng_ref, lq1_ref, lk1_ref, lq2_ref, lk2_ref, o_ref,
                 *, tq, tk, lam_init):
    i = pl.program_id(2)
    t0 = i * tq
    q = q_ref[...]
    qs = (q[:, :HEAD_DIM], q[:, HEAD_DIM:])
    n_full = t0 // tk

    def step(jt, carry, masked):
        k0 = pl.multiple_of(jt * tk, tk)
        k = k_ref[pl.ds(k0, tk), :]
        v = v_ref[pl.ds(k0, tk), :]
        out = []
        for cc in range(2):
            s = _dot_nt(qs[cc], k[:, cc * HEAD_DIM:(cc + 1) * HEAD_DIM]) * ATTN_SCALE
            if masked:
                kpos = k0 + lax.broadcasted_iota(jnp.int32, (tq, tk), 1)
                tpos = t0 + lax.broadcasted_iota(jnp.int32, (tq, tk), 0)
                s = jnp.where(kpos <= tpos, s, NEG_INF)
            out.append(_softmax_update(carry[cc], s, v))
        return tuple(out)

    init = (_softmax_init(tq, 2 * HEAD_DIM), _softmax_init(tq, 2 * HEAD_DIM))
    carry = lax.fori_loop(0, n_full, lambda jt, c: step(jt, c, False), init)
    carry = step(n_full, carry, True)

    lam = (jnp.exp(jnp.sum(lq1_ref[...] * lk1_ref[...], axis=-1, keepdims=True))
           - jnp.exp(jnp.sum(lq2_ref[...] * lk2_ref[...], axis=-1, keepdims=True)) + lam_init)
    o0 = carry[0][2] / carry[0][1]
    o1 = carry[1][2] / carry[1][1]
    o = o0 - lam * o1
    o = o * lax.rsqrt(jnp.mean(o * o, axis=-1, keepdims=True) + DIFF_NORM_EPS) * ng_ref[...]
    o = o * (1.0 - lam_init)
    o_ref[...] = (o * _silu(z_ref[...].astype(F32))).astype(BF16)


def _diff_attn(proj, offs, norm_g, lq1, lk1, lq2, lk2, layer_idx, bsz, seq, tq, tk):
    nq = seq // tq
    w = 2 * HEAD_DIM
    lam_init = 0.8 - 0.6 * math.exp(-0.3 * layer_idx)
    vec = pl.BlockSpec((1, HEAD_DIM), lambda b, h, i: (0, 0))
    return pl.pallas_call(
        functools.partial(_diff_kernel, tq=tq, tk=tk, lam_init=lam_init),
        grid=(bsz, DIFF_HEADS, nq),
        in_specs=[pl.BlockSpec((tq, w), lambda b, h, i: (b * nq + i, P_AQ // w + h)),
                  pl.BlockSpec((seq, w), lambda b, h, i: (b, P_AK // w + h)),
                  pl.BlockSpec((seq, w), lambda b, h, i: (b, offs["AV"] // w + h)),
                  pl.BlockSpec((tq, w), lambda b, h, i: (b * nq + i, offs["AZ"] // w + h)),
                  pl.BlockSpec((1, w), lambda b, h, i: (0, 0)),
                  vec, vec, vec, vec],
        out_specs=pl.BlockSpec((tq, w), lambda b, h, i: (b * nq + i, h)),
        out_shape=jax.ShapeDtypeStruct((bsz * seq, BR_WIDTH), BF16),
        compiler_params=_params("arbitrary", "arbitrary", "arbitrary"),
        name="diff_attn",
    )(proj, proj, proj, proj, norm_g.reshape(1, w), lq1.reshape(1, -1), lk1.reshape(1, -1),
      lq2.reshape(1, -1), lk2.reshape(1, -1))


def _sb_kernel(q_ref, k_ref, v_ref, z_ref, o_ref, *, tq, tk):
    i = pl.program_id(2)
    t0 = i * tq
    q = q_ref[...]
    n_full = t0 // tk
    n_diag = max(1, tq // tk)
    upper2 = jnp.where((lax.broadcasted_iota(jnp.int32, (2 * tk, tk), 0) & (tk - 1))
                       > lax.broadcasted_iota(jnp.int32, (2 * tk, tk), 1), 1.0, 0.0).astype(BF16)

    def step(jt, carry, masked):
        later, acc = carry
        k0 = pl.multiple_of(jt * tk, tk)
        k = k_ref[pl.ds(k0, tk), :]
        v = v_ref[pl.ds(k0, tk), :]
        z = _dot_nt(q, k) * ATTN_SCALE
        log_1m = -(jnp.maximum(z, 0.0) + jnp.log(1.0 + jnp.exp(-jnp.abs(z))))
        if masked:
            strict = (k0 + lax.broadcasted_iota(jnp.int32, (tq, tk), 1)
                      < t0 + lax.broadcasted_iota(jnp.int32, (tq, tk), 0))
            log_1m = jnp.where(strict, log_1m, 0.0)
        hi = log_1m.astype(BF16)
        lo = (log_1m - hi.astype(F32)).astype(BF16)
        after = _dot(jnp.concatenate([hi, lo], axis=1), upper2) + later
        a = jnp.exp(z + log_1m + after)
        if masked:
            a = jnp.where(strict, a, 0.0)
        acc = acc + _dot(a.astype(BF16), v)
        later = later + jnp.sum(log_1m, axis=-1, keepdims=True)
        return later, acc

    carry = (jnp.zeros((tq, 1), F32), jnp.zeros((tq, HEAD_DIM), F32))
    for dd in range(n_diag - 1, -1, -1):
        carry = step(n_full + dd, carry, True)

    def cond(c):
        return (c[0] < n_full) & (c[1] > SB_UNDERFLOW)

    def body(c):
        later, acc = step(n_full - 1 - c[0], (c[2], c[3]), False)
        return c[0] + 1, jnp.max(later), later, acc

    acc = lax.while_loop(cond, body, (jnp.int32(0), jnp.max(carry[0]), carry[0], carry[1]))[3]
    o_ref[...] = (acc * _silu(z_ref[...].astype(F32))).astype(BF16)


def _sb_attn(proj, offs, bsz, seq, tq, tk):
    nq = seq // tq
    w = HEAD_DIM
    return pl.pallas_call(
        functools.partial(_sb_kernel, tq=tq, tk=tk),
        grid=(bsz, SB_HEADS, nq),
        in_specs=[pl.BlockSpec((tq, w), lambda b, h, i: (b * nq + i, offs["CQ"] // w + h)),
                  pl.BlockSpec((seq, w), lambda b, h, i: (b, offs["CK"] // w + h)),
                  pl.BlockSpec((seq, w), lambda b, h, i: (b, offs["CV"] // w + h)),
                  pl.BlockSpec((tq, w), lambda b, h, i: (b * nq + i, offs["CZ"] // w + h))],
        out_specs=pl.BlockSpec((tq, w), lambda b, h, i: (b * nq + i, h)),
        out_shape=jax.ShapeDtypeStruct((bsz * seq, BR_WIDTH), BF16),
        compiler_params=_params("arbitrary", "arbitrary", "arbitrary"),
        name="stick_breaking",
    )(proj, proj, proj, proj)


def _compress_kernel(x_ref, pe_ref, w1_ref, w2_ref, o_ref, *, nc):
    half = CMP_STRIDE * HEAD_DIM
    top = (x_ref[0, 0, pl.ds(0, nc), :] + pe_ref[0:1, :]).astype(BF16)
    bot = (x_ref[0, 0, pl.ds(1, nc), :] + pe_ref[1:2, :]).astype(BF16)
    hid = _dot(top, w1_ref[0:half, :]) + _dot(bot, w1_ref[half:2 * half, :])
    o_ref[0, 0] = _dot(_silu(hid).astype(BF16), w2_ref[...]).astype(BF16)


def _compress(cols, pe, w1, w2, bsz, seq):
    nc = seq // CMP_STRIDE
    half = CMP_STRIDE * HEAD_DIM
    x = cols.reshape(bsz, nc, CMP_STRIDE, NSA_GROUPS, HEAD_DIM).transpose(0, 3, 1, 2, 4)
    x = x.reshape(bsz, NSA_GROUPS, nc, half).astype(F32)
    x = jnp.pad(x, ((0, 0), (0, 0), (0, 8), (0, 0)))
    return pl.pallas_call(
        functools.partial(_compress_kernel, nc=nc),
        grid=(bsz, NSA_GROUPS),
        in_specs=[pl.BlockSpec((1, 1, nc + 8, half), lambda b, g: (b, g, 0, 0)),
                  pl.BlockSpec((2, half), lambda b, g: (0, 0)),
                  pl.BlockSpec((2 * half, HEAD_DIM), lambda b, g: (0, 0)),
                  pl.BlockSpec((HEAD_DIM, HEAD_DIM), lambda b, g: (0, 0))],
        out_specs=pl.BlockSpec((1, 1, nc, HEAD_DIM), lambda b, g: (b, g, 0, 0)),
        out_shape=jax.ShapeDtypeStruct((bsz, NSA_GROUPS, nc, HEAD_DIM), BF16),
        compiler_params=_params("arbitrary", "arbitrary"),
        name="nsa_compress",
    )(x, pe.reshape(2, half), w1.astype(BF16), w2.astype(BF16))


def _nsa_kernel(q_ref, kc_ref, vc_ref, ks_ref, vs_ref, kw_ref, vw_ref, g_ref, z_ref, o_ref, imp_ref,
                *, tq, tk, seq):
    i = pl.program_id(2)
    t0 = i * tq
    nc = seq // CMP_STRIDE
    nb = seq // SLC_LEN
    nbp = SLC_BLOCKS_PAD
    rep = NSA_REP
    q = q_ref[...]
    qh = [q[:, r * HEAD_DIM:(r + 1) * HEAD_DIM] for r in range(rep)]

    kc, vc = kc_ref[0, 0], vc_ref[0, 0]
    tpos_c = t0 + lax.broadcasted_iota(jnp.int32, (tq, nc), 0)
    ncol = lax.broadcasted_iota(jnp.int32, (tq, nc), 1)
    cvalid = CMP_STRIDE * ncol + (CMP_LEN - 1) <= tpos_c
    psum = jnp.zeros((tq, nc), F32)
    o_cmp = []
    for r in range(rep):
        s = jnp.where(cvalid, _dot_nt(qh[r], kc) * ATTN_SCALE, NEG_INF)
        e = jnp.where(cvalid, jnp.exp(s - jnp.max(s, axis=-1, keepdims=True)), 0.0)
        den = jnp.sum(e, axis=-1, keepdims=True)
        p = e / jnp.where(den > 0.0, den, 1.0)
        psum = psum + p
        o_cmp.append(_dot(p.astype(BF16), vc))

    nb8 = imp_ref.shape[0]
    jrow = lax.broadcasted_iota(jnp.int32, (nb8, nc), 0)
    ncol2 = lax.broadcasted_iota(jnp.int32, (nb8, nc), 1)
    overlap = jnp.where((CMP_STRIDE * ncol2 < SLC_LEN * jrow + SLC_LEN)
                        & (CMP_STRIDE * ncol2 + CMP_LEN > SLC_LEN * jrow)
                        & (ncol2 < nc - 1) & (jrow < nb), 1.0, 0.0).astype(BF16)
    p1 = psum.astype(BF16)
    r1 = psum - p1.astype(F32)
    p2 = r1.astype(BF16)
    p3 = (r1 - p2.astype(F32)).astype(BF16)
    imp = _dot_nt(overlap, p1) + _dot_nt(overlap, p2) + _dot_nt(overlap, p3)
    jt_ = lax.broadcasted_iota(jnp.int32, (nb8, tq), 0)
    tblk = jnp.right_shift(t0 + lax.broadcasted_iota(jnp.int32, (nb8, tq), 1), SLC_SHIFT)
    imp = jnp.where((jt_ == tblk) | (jt_ == 0), FORCE_SCORE, imp)
    imp = jnp.where(jt_ <= tblk, imp, NEG_INF)
    imp_ref[...] = imp

    def rank_body(jp, cnt):
        row = imp_ref[pl.ds(jp, 1), :]
        beats = (row > imp) | ((row == imp) & (jp < jt_))
        return cnt + jnp.where(beats, 1.0, 0.0)

    n_live = jnp.minimum((t0 + tq - 1) // SLC_LEN + 1, nb)
    cnt = lax.fori_loop(0, n_live, rank_body, jnp.zeros((nb8, tq), F32))
    keep = (cnt < float(min(SLC_TOPK, nb))) & (imp > 0.5 * NEG_INF)
    unsel_t = jnp.where(keep, 0.0, 1.0)
    if nbp > nb8:
        unsel_t = jnp.concatenate([unsel_t, jnp.zeros((nbp - nb8, tq), F32)], axis=0)
    unselected = unsel_t.T.astype(BF16)

    rows = rep * tq
    q_sel = jnp.concatenate([jnp.concatenate([qh[r], unselected], axis=1) for r in range(rep)], axis=0)
    q_win = jnp.concatenate(qh, axis=0)

    def slc_step(jt, carry, masked):
        k0 = pl.multiple_of(jt * tk, tk)
        k = ks_ref[pl.ds(k0, tk), :]
        v = vs_ref[pl.ds(k0, tk), :]
        kblk = jnp.right_shift(k0 + lax.broadcasted_iota(jnp.int32, (tk, nbp), 0), SLC_SHIFT)
        bias = jnp.where(lax.broadcasted_iota(jnp.int32, (tk, nbp), 1) == kblk,
                         -UNSELECTED_BIAS, 0.0).astype(BF16)
        s = _dot_nt(q_sel, jnp.concatenate([k, bias], axis=1)) * ATTN_SCALE
        if masked:
            row_t = t0 + (lax.broadcasted_iota(jnp.int32, (rows, tk), 0) & (tq - 1))
            s = jnp.where(k0 + lax.broadcasted_iota(jnp.int32, (rows, tk), 1) <= row_t, s, NEG_INF)
        return _softmax_update(carry, s, v)

    n_full = t0 // tk
    carry = lax.fori_loop(0, n_full, lambda jt, c: slc_step(jt, c, False), _softmax_init(rows, HEAD_DIM))
    m_s, l_s, acc_s = slc_step(n_full, carry, True)
    o_slc = acc_s / l_s

    span = min(WINDOW + tq, seq)
    start = pl.multiple_of(jnp.maximum(t0 - WINDOW, 0), tq)
    kpos = start + lax.broadcasted_iota(jnp.int32, (rows, span), 1)
    row_w = t0 + (lax.broadcasted_iota(jnp.int32, (rows, span), 0) & (tq - 1))
    ok = (kpos <= row_w) & (kpos > row_w - WINDOW)
    s = jnp.where(ok, _dot_nt(q_win, kw_ref[pl.ds(start, span), :]) * ATTN_SCALE, NEG_INF)
    p = jnp.exp(s - jnp.max(s, axis=-1, keepdims=True))
    o_win = _dot(p.astype(BF16), vw_ref[pl.ds(start, span), :]) / jnp.sum(p, axis=-1, keepdims=True)

    gates = _sigmoid(g_ref[...].astype(F32))
    z = z_ref[...].astype(F32)
    for r in range(rep):
        rs = slice(r * tq, (r + 1) * tq)
        cs = slice(r * HEAD_DIM, (r + 1) * HEAD_DIM)
        o = (gates[:, 3 * r:3 * r + 1] * o_cmp[r] + gates[:, 3 * r + 1:3 * r + 2] * o_slc[rs]
             + gates[:, 3 * r + 2:3 * r + 3] * o_win[rs])
        o_ref[:, cs] = (o * _silu(z[:, cs])).astype(BF16)


def _nsa_attn(proj, offs, kcmp, vcmp, bsz, seq, tq, tk):
    nq = seq // tq
    nc = seq // CMP_STRIDE
    w = HEAD_DIM
    gw = NSA_REP * HEAD_DIM
    assert seq // SLC_LEN <= SLC_BLOCKS_PAD and tq & (tq - 1) == 0 and tk % tq == 0
    kv = lambda off: pl.BlockSpec((seq, w), lambda b, g, i: (b, off // w + g))
    cmp_spec = pl.BlockSpec((1, 1, nc, w), lambda b, g, i: (b, g, 0, 0))
    return pl.pallas_call(
        functools.partial(_nsa_kernel, tq=tq, tk=tk, seq=seq),
        grid=(bsz, NSA_GROUPS, nq),
        in_specs=[pl.BlockSpec((tq, gw), lambda b, g, i: (b * nq + i, P_NQ // gw + g)),
                  cmp_spec, cmp_spec,
                  kv(P_NKS), kv(offs["NVS"]), kv(P_NKW), kv(offs["NVW"]),
                  pl.BlockSpec((tq, LANES), lambda b, g, i: (b * nq + i, offs["NG"] // LANES + g)),
                  pl.BlockSpec((tq, gw), lambda b, g, i: (b * nq + i, offs["NZ"] // gw + g))],
        out_specs=pl.BlockSpec((tq, gw), lambda b, g, i: (b * nq + i, g)),
        out_shape=jax.ShapeDtypeStruct((bsz * seq, BR_WIDTH), BF16),
        scratch_shapes=[pltpu.VMEM((-(-(seq // SLC_LEN) // 8) * 8, tq), F32)],
        compiler_params=_params("arbitrary", "arbitrary", "arbitrary"),
        name="nsa_attn",
    )(proj, kcmp, vcmp, proj, proj, proj, proj, proj, proj)


def _merge_kernel(x_ref, ya_ref, yb_ref, yc_ref, mg0_ref, mg1_ref, mg2_ref, wb_ref, wo_ref, gp_ref,
                  gate_ref, o_ref):
    merged = None
    for n, (y_ref, mg_ref) in enumerate(((ya_ref, mg0_ref), (yb_ref, mg1_ref), (yc_ref, mg2_ref))):
        t = _dot(y_ref[...], wb_ref[n]) * _sigmoid(mg_ref[...].astype(F32))
        merged = t if merged is None else merged + t
    o = _dot(merged.astype(BF16), wo_ref[...])
    o = o * lax.rsqrt(jnp.mean(o * o, axis=-1, keepdims=True) + EPS) * gp_ref[...]
    o_ref[...] = x_ref[...] + gate_ref[0] * o


def _merge(xf, ya, yb, yc, proj, wb, wo, g_post, gate, seq, tm):
    rows, d = xf.shape
    tpb = seq // tm
    row = lambda w_: pl.BlockSpec((tm, w_), lambda i: (i, 0))
    mg = lambda n: pl.BlockSpec((tm, d), lambda i: (i, P_MG // d + n))
    const = pl.Buffered(1)
    return pl.pallas_call(
        _merge_kernel,
        grid=(rows // tm,),
        in_specs=[row(d), row(BR_WIDTH), row(BR_WIDTH), row(BR_WIDTH), mg(0), mg(1), mg(2),
                  pl.BlockSpec((N_BRANCH, BR_WIDTH, d), lambda i: (0, 0, 0), pipeline_mode=const),
                  pl.BlockSpec((d, d), lambda i: (0, 0), pipeline_mode=const),
                  pl.BlockSpec((1, d), lambda i: (0, 0)),
                  pl.BlockSpec((1, 1, d), lambda i: (i // tpb, 0, 0))],
        out_specs=row(d),
        out_shape=jax.ShapeDtypeStruct((rows, d), F32),
        compiler_params=_params("arbitrary"),
        name="merge_out",
    )(xf, ya, yb, yc, proj, proj, proj, wb.astype(BF16), wo.astype(BF16), g_post.reshape(1, d),
      gate[:, None, :])


def _tiles(seq, d_model):
    return dict(tm_in=min(1024, seq), tn_in=1024, tm_merge=min(256, seq),
                diff=(min(256, seq), min(512, seq)),
                sb=(min(512, seq), min(256, seq)),
                nsa=(min(128, seq), min(512, seq)))


def kernel(x, c, norm_pre_g, norm_post_g, w_ada, b_ada, w_in, lambda_q1, lambda_k1, lambda_q2,
           lambda_k2, diff_norm_g, cmp_pe_k, cmp_w1_k, cmp_w2_k, cmp_pe_v, cmp_w1_v, cmp_w2_v,
           w_branch, w_out):
    bsz, seq, d = x.shape
    depth = w_in.shape[0]
    t = _tiles(seq, d)
    offs = _rest_offsets(d)
    assert d % 512 == 0 and ROPE_COLS % d == 0 and seq % t["tm_in"] == 0 and seq % 128 == 0
    n_total = -(-offs["END"] // t["tn_in"]) * t["tn_in"]
    tables = _rope_tables(seq)
    mod = _ada(c, w_ada, b_ada)
    xf = x.reshape(bsz * seq, d)
    for l in range(depth):
        shift, scale, gate = jnp.split(mod[l], 3, axis=-1)
        wp = _permute_w_in(w_in[l], d, n_total)
        proj = _inproj(xf, norm_pre_g[l], scale, shift, tables, wp, seq, t["tm_in"], t["tn_in"])
        ya = _diff_attn(proj, offs, diff_norm_g[l], lambda_q1[l], lambda_k1[l], lambda_q2[l],
                        lambda_k2[l], l, bsz, seq, *t["diff"])
        kcmp = _compress(proj[:, P_NKC:P_NKC + NSA_KV], cmp_pe_k[l], cmp_w1_k[l], cmp_w2_k[l], bsz, seq)
        vcmp = _compress(proj[:, offs["NVC"]:offs["NVC"] + NSA_KV], cmp_pe_v[l], cmp_w1_v[l],
                         cmp_w2_v[l], bsz, seq)
        yb = _nsa_attn(proj, offs, kcmp, vcmp, bsz, seq, *t["nsa"])
        yc = _sb_attn(proj, offs, bsz, seq, *t["sb"])
        xf = _merge(xf, ya, yb, yc, proj, w_branch[l], w_out[l], norm_post_g[l], gate, seq,
                    t["tm_merge"])
    return xf.reshape(bsz, seq, d)
```

```python
import functools
import math

import jax
import jax.numpy as jnp
from jax import lax
from jax.experimental import pallas as pl
from jax.experimental.pallas import tpu as pltpu

F32 = jnp.float32
BF16 = jnp.bfloat16

HEAD_DIM = 128
BR_WIDTH = 1024
N_BRANCH = 3
ROPE_THETA = 500000.0
ROPE_DIM = HEAD_DIM // 4
ROPE_HALF = ROPE_DIM // 2
EPS = 1e-6
NEG_INF = -1e30
FORCE_SCORE = 1e6
DIFF_HEADS = BR_WIDTH // (2 * HEAD_DIM)
DIFF_NORM_EPS = 1e-5
NSA_HEADS = BR_WIDTH // HEAD_DIM
NSA_GROUPS = 2
NSA_REP = NSA_HEADS // NSA_GROUPS
CMP_LEN = 32
CMP_STRIDE = 16
SLC_LEN = 64
SLC_SHIFT = SLC_LEN.bit_length() - 1
SLC_TOPK = 16
WINDOW = 512
SB_HEADS = BR_WIDTH // HEAD_DIM
NSA_KV = NSA_GROUPS * HEAD_DIM
ATTN_SCALE = HEAD_DIM ** -0.5
Q_PRESCALE = ATTN_SCALE * math.log2(math.e)

LANES = 128
SLC_BLOCKS_PAD = 128
UNSELECTED_BIAS = 2.0 ** 30
SB_UNDERFLOW = -104.0
VMEM_LIMIT = 56 * 1024 * 1024

_O_AQ, _O_AK, _O_AV, _O_AZ = 0, 1024, 2048, 3072
_O_NQ, _O_NKC, _O_NVC, _O_NKS, _O_NVS, _O_NKW, _O_NVW = 4096, 5120, 5376, 5632, 5888, 6144, 6400
_O_NG, _O_NZ = 6656, 6680
_O_CQ, _O_CK, _O_CV, _O_CZ, _O_MG = 7704, 8728, 9752, 10776, 11800
N_GATES = 3 * NSA_HEADS
P_AQ, P_AK, P_NQ, P_NKC, P_NKS, P_NKW = 0, 1024, 2048, 3072, 3328, 3584
ROPE_COLS = 4096
P_MG = ROPE_COLS


def _rest_offsets(d_model):
    r0 = ROPE_COLS + N_BRANCH * d_model
    offs = dict(AV=r0, AZ=r0 + 1024, NZ=r0 + 2048, CQ=r0 + 3072, CK=r0 + 4096, CV=r0 + 5120,
                CZ=r0 + 6144, NVC=r0 + 7168, NVS=r0 + 7424, NVW=r0 + 7680, NG=r0 + 7936)
    offs["END"] = r0 + 7936 + NSA_GROUPS * LANES
    return offs


def _permute_w_in(w, d_model, n_total):
    d = w.shape[0]
    z = lambda n: jnp.zeros((d, n), w.dtype)
    gates_per_group = N_GATES // NSA_GROUPS
    segs = [w[:, _O_AQ:_O_AQ + 1024], w[:, _O_AK:_O_AK + 1024], w[:, _O_NQ:_O_NQ + 1024],
            w[:, _O_NKC:_O_NKC + 256], w[:, _O_NKS:_O_NKS + 256], w[:, _O_NKW:_O_NKW + 256],
            z(ROPE_COLS - 3840),
            w[:, _O_MG:_O_MG + N_BRANCH * d_model],
            w[:, _O_AV:_O_AV + 1024], w[:, _O_AZ:_O_AZ + 1024], w[:, _O_NZ:_O_NZ + 1024],
            w[:, _O_CQ:_O_CQ + 4096],
            w[:, _O_NVC:_O_NVC + 256], w[:, _O_NVS:_O_NVS + 256], w[:, _O_NVW:_O_NVW + 256]]
    for g in range(NSA_GROUPS):
        segs += [w[:, _O_NG + g * gates_per_group:_O_NG + (g + 1) * gates_per_group],
                 z(LANES - gates_per_group)]
    used = _rest_offsets(d_model)["END"]
    if n_total > used:
        segs.append(z(n_total - used))
    return jnp.concatenate(segs, axis=1).astype(BF16)


def _rope_tables(seq):
    pos = jnp.arange(seq, dtype=F32)
    inv = ROPE_THETA ** (-jnp.arange(0, ROPE_DIM, 2, dtype=F32) / ROPE_DIM)
    ang = pos[:, None] * inv[None, :]
    cos, sin = jnp.cos(ang), jnp.sin(ang)
    ones = jnp.ones((seq, HEAD_DIM - ROPE_DIM), F32)
    zh = jnp.zeros((seq, ROPE_HALF), F32)
    zr = jnp.zeros((seq, HEAD_DIM - ROPE_DIM), F32)
    c = jnp.concatenate([cos, cos, ones], axis=1)
    s1 = jnp.concatenate([-sin, zh, zr], axis=1)
    s2 = jnp.concatenate([zh, sin, zr], axis=1)
    return c, s1, s2


def _sigmoid(x):
    return 1.0 / (1.0 + jnp.exp(-x))


def _silu(x):
    return x * _sigmoid(x)


def _dot_nt(a, b):
    return lax.dot_general(a, b, (((1,), (1,)), ((), ())), preferred_element_type=F32)


def _dot(a, b):
    return jnp.dot(a, b, preferred_element_type=F32)


def _params(*sem):
    return pltpu.CompilerParams(dimension_semantics=sem, vmem_limit_bytes=VMEM_LIMIT)


def _ada_kernel(c_ref, w_ref, b_ref, o_ref):
    c = c_ref[...]
    o_ref[0] = _dot(_silu(c).astype(BF16), w_ref[0].astype(BF16)) + b_ref[0]


def _ada(c, w_ada, b_ada):
    depth, d, n3 = w_ada.shape
    bsz = c.shape[0]
    rows = max(8, bsz)
    cp = jnp.zeros((rows, d), F32).at[:bsz].set(c)
    tn = math.gcd(1024, n3)
    out = pl.pallas_call(
        _ada_kernel,
        grid=(depth, n3 // tn),
        in_specs=[pl.BlockSpec((rows, d), lambda l, j: (0, 0)),
                  pl.BlockSpec((1, d, tn), lambda l, j: (l, 0, j)),
                  pl.BlockSpec((1, 1, tn), lambda l, j: (l, 0, j))],
        out_specs=pl.BlockSpec((1, rows, tn), lambda l, j: (l, 0, j)),
        out_shape=jax.ShapeDtypeStruct((depth, rows, n3), F32),
        compiler_params=_params("arbitrary", "arbitrary"),
        name="ada_mod",
    )(cp, w_ada, b_ada.reshape(depth, 1, n3))
    return out[:, :bsz]


def _inproj_kernel(x_ref, g_ref, sc_ref, sh_ref, c_ref, s1_ref, s2_ref, w_ref, o_ref, h_ref,
                   *, n_rope_tiles, tn):
    j = pl.program_id(1)

    @pl.when(j == 0)
    def _():
        x = x_ref[...]
        y = x * lax.rsqrt(jnp.mean(x * x, axis=-1, keepdims=True) + EPS) * g_ref[...]
        h_ref[...] = (y * (1.0 + sc_ref[0]) + sh_ref[0]).astype(BF16)

    y = _dot(h_ref[...], w_ref[...])

    @pl.when(j < n_rope_tiles)
    def _():
        c, s1, s2 = c_ref[...], s1_ref[...], s2_ref[...]
        for hh in range(tn // HEAD_DIM):
            sl = slice(hh * HEAD_DIM, (hh + 1) * HEAD_DIM)
            yh = y[:, sl]
            r = (yh * c + pltpu.roll(yh, HEAD_DIM - ROPE_HALF, 1) * s1
                 + pltpu.roll(yh, ROPE_HALF, 1) * s2)
            col = j * tn + hh * HEAD_DIM
            is_q = ((col >= P_AQ) & (col < P_AQ + BR_WIDTH)) | ((col >= P_NQ) & (col < P_NQ + BR_WIDTH))
            o_ref[:, sl] = (r * jnp.where(is_q, Q_PRESCALE, 1.0)).astype(BF16)

    @pl.when(j >= n_rope_tiles)
    def _():
        o_ref[...] = y.astype(BF16)


def _inproj(xf, g, scale, shift, tables, wp, seq, tm, tn):
    rows, d = xf.shape
    n_total = wp.shape[1]
    tpb = seq // tm
    c, s1, s2 = tables
    tab_spec = pl.BlockSpec((tm, HEAD_DIM), lambda i, j: (i % tpb, 0))
    mod_spec = pl.BlockSpec((1, 1, d), lambda i, j: (i // tpb, 0, 0))
    return pl.pallas_call(
        functools.partial(_inproj_kernel, n_rope_tiles=ROPE_COLS // tn, tn=tn),
        grid=(rows // tm, n_total // tn),
        in_specs=[pl.BlockSpec((tm, d), lambda i, j: (i, 0)),
                  pl.BlockSpec((1, d), lambda i, j: (0, 0)),
                  mod_spec, mod_spec, tab_spec, tab_spec, tab_spec,
                  pl.BlockSpec((d, tn), lambda i, j: (0, j))],
        out_specs=pl.BlockSpec((tm, tn), lambda i, j: (i, j)),
        out_shape=jax.ShapeDtypeStruct((rows, n_total), BF16),
        scratch_shapes=[pltpu.VMEM((tm, d), BF16)],
        compiler_params=_params("arbitrary", "arbitrary"),
        name="in_proj",
    )(xf, g.reshape(1, d), scale[:, None, :], shift[:, None, :], c, s1, s2, wp)


def _softmax_update(carry, s, v):
    m, l, acc = carry
    m_new = jnp.maximum(m, jnp.max(s, axis=-1, keepdims=True))
    alpha = jnp.exp2(m - m_new)
    p = jnp.exp2(s - m_new)
    l = alpha * l + jnp.sum(p, axis=-1, keepdims=True)
    acc = alpha * acc + _dot(p.astype(BF16), v)
    return m_new, l, acc


def _softmax_init(rows, width):
    return (jnp.full((rows, 1), NEG_INF, F32), jnp.zeros((rows, 1), F32),
            jnp.zeros((rows, width), F32))


def _diff_kernel(q_ref, k_ref, v_ref, z_ref, ng_ref, lq1_ref, lk1_ref, lq2_ref, lk2_ref, o_ref,
                 *, tq, tk, lam_init):
    i = pl.program_id(2)
    t0 = i * tq
    q = q_ref[...]
    qs = (q[:, :HEAD_DIM], q[:, HEAD_DIM:])
    n_full = t0 // tk

    def step(jt, carry, masked):
        k0 = pl.multiple_of(jt * tk, tk)
        k = k_ref[pl.ds(k0, tk), :]
        v = v_ref[pl.ds(k0, tk), :]
        s = [_dot_nt(qs[cc], k[:, cc * HEAD_DIM:(cc + 1) * HEAD_DIM]) for cc in range(2)]
        if masked:
            kpos = k0 + lax.broadcasted_iota(jnp.int32, (tq, tk), 1)
            tpos = t0 + lax.broadcasted_iota(jnp.int32, (tq, tk), 0)
            s = [jnp.where(kpos <= tpos, sc, NEG_INF) for sc in s]
        return tuple(_softmax_update(carry[cc], s[cc], v) for cc in range(2))

    init = (_softmax_init(tq, 2 * HEAD_DIM), _softmax_init(tq, 2 * HEAD_DIM))
    carry = lax.fori_loop(0, n_full, lambda jt, c: step(jt, c, False), init)
    carry = step(n_full, carry, True)

    lam = (jnp.exp(jnp.sum(lq1_ref[...] * lk1_ref[...], axis=-1, keepdims=True))
           - jnp.exp(jnp.sum(lq2_ref[...] * lk2_ref[...], axis=-1, keepdims=True)) + lam_init)
    o0 = carry[0][2] / carry[0][1]
    o1 = carry[1][2] / carry[1][1]
    o = o0 - lam * o1
    o = o * lax.rsqrt(jnp.mean(o * o, axis=-1, keepdims=True) + DIFF_NORM_EPS) * ng_ref[...]
    o = o * (1.0 - lam_init)
    o_ref[...] = (o * _silu(z_ref[...].astype(F32))).astype(BF16)


def _diff_attn(proj, offs, norm_g, lq1, lk1, lq2, lk2, layer_idx, bsz, seq, tq, tk):
    nq = seq // tq
    w = 2 * HEAD_DIM
    lam_init = 0.8 - 0.6 * math.exp(-0.3 * layer_idx)
    vec = pl.BlockSpec((1, HEAD_DIM), lambda b, h, i: (0, 0))
    return pl.pallas_call(
        functools.partial(_diff_kernel, tq=tq, tk=tk, lam_init=lam_init),
        grid=(bsz, DIFF_HEADS, nq),
        in_specs=[pl.BlockSpec((tq, w), lambda b, h, i: (b * nq + i, P_AQ // w + h)),
                  pl.BlockSpec((seq, w), lambda b, h, i: (b, P_AK // w + h)),
                  pl.BlockSpec((seq, w), lambda b, h, i: (b, offs["AV"] // w + h)),
                  pl.BlockSpec((tq, w), lambda b, h, i: (b * nq + i, offs["AZ"] // w + h)),
                  pl.BlockSpec((1, w), lambda b, h, i: (0, 0)),
                  vec, vec, vec, vec],
        out_specs=pl.BlockSpec((tq, w), lambda b, h, i: (b * nq + i, h)),
        out_shape=jax.ShapeDtypeStruct((bsz * seq, BR_WIDTH), BF16),
        compiler_params=_params("arbitrary", "arbitrary", "arbitrary"),
        name="diff_attn",
    )(proj, proj, proj, proj, norm_g.reshape(1, w), lq1.reshape(1, -1), lk1.reshape(1, -1),
      lq2.reshape(1, -1), lk2.reshape(1, -1))


def _sb_kernel(q_ref, k_ref, v_ref, z_ref, o_ref, *, tq, tk):
    i = pl.program_id(2)
    t0 = i * tq
    q = q_ref[...]
    n_full = t0 // tk
    n_diag = max(1, tq // tk)
    upper2 = jnp.where((lax.broadcasted_iota(jnp.int32, (2 * tk, tk), 0) & (tk - 1))
                       > lax.broadcasted_iota(jnp.int32, (2 * tk, tk), 1), 1.0, 0.0).astype(BF16)

    def step(jt, carry, masked):
        later, acc = carry
        k0 = pl.multiple_of(jt * tk, tk)
        k = k_ref[pl.ds(k0, tk), :]
        v = v_ref[pl.ds(k0, tk), :]
        z = _dot_nt(q, k) * ATTN_SCALE
        log_1m = -(jnp.maximum(z, 0.0) + jnp.log(1.0 + jnp.exp(-jnp.abs(z))))
        if masked:
            strict = (k0 + lax.broadcasted_iota(jnp.int32, (tq, tk), 1)
                      < t0 + lax.broadcasted_iota(jnp.int32, (tq, tk), 0))
            log_1m = jnp.where(strict, log_1m, 0.0)
        hi = log_1m.astype(BF16)
        lo = (log_1m - hi.astype(F32)).astype(BF16)
        after = _dot(jnp.concatenate([hi, lo], axis=1), upper2) + later
        a = jnp.exp(z + log_1m + after)
        if masked:
            a = jnp.where(strict, a, 0.0)
        acc = acc + _dot(a.astype(BF16), v)
        later = later + jnp.sum(log_1m, axis=-1, keepdims=True)
        return later, acc

    carry = (jnp.zeros((tq, 1), F32), jnp.zeros((tq, HEAD_DIM), F32))
    for dd in range(n_diag - 1, -1, -1):
        carry = step(n_full + dd, carry, True)

    def cond(c):
        return (c[0] < n_full) & (c[1] > SB_UNDERFLOW)

    def body(c):
        later, acc = step(n_full - 1 - c[0], (c[2], c[3]), False)
        return c[0] + 1, jnp.max(later), later, acc

    acc = lax.while_loop(cond, body, (jnp.int32(0), jnp.max(carry[0]), carry[0], carry[1]))[3]
    o_ref[...] = (acc * _silu(z_ref[...].astype(F32))).astype(BF16)


def _sb_attn(proj, offs, bsz, seq, tq, tk):
    nq = seq // tq
    w = HEAD_DIM
    return pl.pallas_call(
        functools.partial(_sb_kernel, tq=tq, tk=tk),
        grid=(bsz, SB_HEADS, nq),
        in_specs=[pl.BlockSpec((tq, w), lambda b, h, i: (b * nq + i, offs["CQ"] // w + h)),
                  pl.BlockSpec((seq, w), lambda b, h, i: (b, offs["CK"] // w + h)),
                  pl.BlockSpec((seq, w), lambda b, h, i: (b, offs["CV"] // w + h)),
                  pl.BlockSpec((tq, w), lambda b, h, i: (b * nq + i, offs["CZ"] // w + h))],
        out_specs=pl.BlockSpec((tq, w), lambda b, h, i: (b * nq + i, h)),
        out_shape=jax.ShapeDtypeStruct((bsz * seq, BR_WIDTH), BF16),
        compiler_params=_params("arbitrary", "arbitrary", "arbitrary"),
        name="stick_breaking",
    )(proj, proj, proj, proj)


def _compress_kernel(x_ref, pe_ref, w1_ref, w2_ref, o_ref, *, nc):
    half = CMP_STRIDE * HEAD_DIM
    top = (x_ref[0, 0, pl.ds(0, nc), :] + pe_ref[0:1, :]).astype(BF16)
    bot = (x_ref[0, 0, pl.ds(1, nc), :] + pe_ref[1:2, :]).astype(BF16)
    hid = _dot(top, w1_ref[0:half, :]) + _dot(bot, w1_ref[half:2 * half, :])
    o_ref[0, 0] = _dot(_silu(hid).astype(BF16), w2_ref[...]).astype(BF16)


def _compress(cols, pe, w1, w2, bsz, seq):
    nc = seq // CMP_STRIDE
    half = CMP_STRIDE * HEAD_DIM
    x = cols.reshape(bsz, nc, CMP_STRIDE, NSA_GROUPS, HEAD_DIM).transpose(0, 3, 1, 2, 4)
    x = x.reshape(bsz, NSA_GROUPS, nc, half).astype(F32)
    x = jnp.pad(x, ((0, 0), (0, 0), (0, 8), (0, 0)))
    return pl.pallas_call(
        functools.partial(_compress_kernel, nc=nc),
        grid=(bsz, NSA_GROUPS),
        in_specs=[pl.BlockSpec((1, 1, nc + 8, half), lambda b, g: (b, g, 0, 0)),
                  pl.BlockSpec((2, half), lambda b, g: (0, 0)),
                  pl.BlockSpec((2 * half, HEAD_DIM), lambda b, g: (0, 0)),
                  pl.BlockSpec((HEAD_DIM, HEAD_DIM), lambda b, g: (0, 0))],
        out_specs=pl.BlockSpec((1, 1, nc, HEAD_DIM), lambda b, g: (b, g, 0, 0)),
        out_shape=jax.ShapeDtypeStruct((bsz, NSA_GROUPS, nc, HEAD_DIM), BF16),
        compiler_params=_params("arbitrary", "arbitrary"),
        name="nsa_compress",
    )(x, pe.reshape(2, half), w1.astype(BF16), w2.astype(BF16))


def _nsa_kernel(q_ref, kc_ref, vc_ref, ks_ref, vs_ref, kw_ref, vw_ref, g_ref, z_ref, o_ref, imp_ref,
                *, tq, tk, seq):
    i = pl.program_id(2)
    t0 = i * tq
    nc = seq // CMP_STRIDE
    nb = seq // SLC_LEN
    nbp = SLC_BLOCKS_PAD
    rep = NSA_REP
    q = q_ref[...]
    qh = [q[:, r * HEAD_DIM:(r + 1) * HEAD_DIM] for r in range(rep)]

    kc, vc = kc_ref[0, 0], vc_ref[0, 0]
    tpos_c = t0 + lax.broadcasted_iota(jnp.int32, (tq, nc), 0)
    ncol = lax.broadcasted_iota(jnp.int32, (tq, nc), 1)
    cvalid = CMP_STRIDE * ncol + (CMP_LEN - 1) <= tpos_c
    psum = jnp.zeros((tq, nc), F32)
    o_cmp = []
    for r in range(rep):
        s = jnp.where(cvalid, _dot_nt(qh[r], kc), NEG_INF)
        e = jnp.where(cvalid, jnp.exp2(s - jnp.max(s, axis=-1, keepdims=True)), 0.0)
        den = jnp.sum(e, axis=-1, keepdims=True)
        p = e / jnp.where(den > 0.0, den, 1.0)
        psum = psum + p
        o_cmp.append(_dot(p.astype(BF16), vc))

    nb8 = imp_ref.shape[0]
    jrow = lax.broadcasted_iota(jnp.int32, (nb8, nc), 0)
    ncol2 = lax.broadcasted_iota(jnp.int32, (nb8, nc), 1)
    overlap = jnp.where((CMP_STRIDE * ncol2 < SLC_LEN * jrow + SLC_LEN)
                        & (CMP_STRIDE * ncol2 + CMP_LEN > SLC_LEN * jrow)
                        & (ncol2 < nc - 1) & (jrow < nb), 1.0, 0.0).astype(BF16)
    p1 = psum.astype(BF16)
    r1 = psum - p1.astype(F32)
    p2 = r1.astype(BF16)
    p3 = (r1 - p2.astype(F32)).astype(BF16)
    imp = _dot_nt(overlap, p1) + _dot_nt(overlap, p2) + _dot_nt(overlap, p3)
    jt_ = lax.broadcasted_iota(jnp.int32, (nb8, tq), 0)
    tblk = jnp.right_shift(t0 + lax.broadcasted_iota(jnp.int32, (nb8, tq), 1), SLC_SHIFT)
    imp = jnp.where((jt_ == tblk) | (jt_ == 0), FORCE_SCORE, imp)
    imp = jnp.where(jt_ <= tblk, imp, NEG_INF)
    imp_ref[...] = imp

    def rank_body(jp, cnt):
        row = imp_ref[pl.ds(jp, 1), :]
        beats = (row > imp) | ((row == imp) & (jp < jt_))
        return cnt + jnp.where(beats, 1.0, 0.0)

    n_live = jnp.minimum((t0 + tq - 1) // SLC_LEN + 1, nb)
    cnt = lax.fori_loop(0, n_live, rank_body, jnp.zeros((nb8, tq), F32))
    keep = (cnt < float(min(SLC_TOPK, nb))) & (imp > 0.5 * NEG_INF)
    unsel_t = jnp.where(keep, 0.0, 1.0)
    if nbp > nb8:
        unsel_t = jnp.concatenate([unsel_t, jnp.zeros((nbp - nb8, tq), F32)], axis=0)
    unselected = unsel_t.T.astype(BF16)

    rows = rep * tq
    q_sel = jnp.concatenate([jnp.concatenate([qh[r], unselected], axis=1) for r in range(rep)], axis=0)
    q_win = jnp.concatenate(qh, axis=0)

    def slc_step(jt, carry, masked):
        k0 = pl.multiple_of(jt * tk, tk)
        k = ks_ref[pl.ds(k0, tk), :]
        v = vs_ref[pl.ds(k0, tk), :]
        kblk = jnp.right_shift(k0 + lax.broadcasted_iota(jnp.int32, (tk, nbp), 0), SLC_SHIFT)
        bias = jnp.where(lax.broadcasted_iota(jnp.int32, (tk, nbp), 1) == kblk,
                         -UNSELECTED_BIAS, 0.0).astype(BF16)
        s = _dot_nt(q_sel, jnp.concatenate([k, bias], axis=1))
        if masked:
            row_t = t0 + (lax.broadcasted_iota(jnp.int32, (rows, tk), 0) & (tq - 1))
            s = jnp.where(k0 + lax.broadcasted_iota(jnp.int32, (rows, tk), 1) <= row_t, s, NEG_INF)
        return _softmax_update(carry, s, v)

    n_full = t0 // tk
    carry = lax.fori_loop(0, n_full, lambda jt, c: slc_step(jt, c, False), _softmax_init(rows, HEAD_DIM))
    m_s, l_s, acc_s = slc_step(n_full, carry, True)
    o_slc = acc_s / l_s

    span = min(WINDOW + tq, seq)
    start = pl.multiple_of(jnp.maximum(t0 - WINDOW, 0), tq)
    kpos = start + lax.broadcasted_iota(jnp.int32, (rows, span), 1)
    row_w = t0 + (lax.broadcasted_iota(jnp.int32, (rows, span), 0) & (tq - 1))
    ok = (kpos <= row_w) & (kpos > row_w - WINDOW)
    s = jnp.where(ok, _dot_nt(q_win, kw_ref[pl.ds(start, span), :]), NEG_INF)
    p = jnp.exp2(s - jnp.max(s, axis=-1, keepdims=True))
    o_win = _dot(p.astype(BF16), vw_ref[pl.ds(start, span), :]) / jnp.sum(p, axis=-1, keepdims=True)

    gates = _sigmoid(g_ref[...].astype(F32))
    z = z_ref[...].astype(F32)
    for r in range(rep):
        rs = slice(r * tq, (r + 1) * tq)
        cs = slice(r * HEAD_DIM, (r + 1) * HEAD_DIM)
        o = (gates[:, 3 * r:3 * r + 1] * o_cmp[r] + gates[:, 3 * r + 1:3 * r + 2] * o_slc[rs]
             + gates[:, 3 * r + 2:3 * r + 3] * o_win[rs])
        o_ref[:, cs] = (o * _silu(z[:, cs])).astype(BF16)


def _nsa_attn(proj, offs, kcmp, vcmp, bsz, seq, tq, tk):
    nq = seq // tq
    nc = seq // CMP_STRIDE
    w = HEAD_DIM
    gw = NSA_REP * HEAD_DIM
    assert seq // SLC_LEN <= SLC_BLOCKS_PAD and tq & (tq - 1) == 0 and tk % tq == 0
    kv = lambda off: pl.BlockSpec((seq, w), lambda b, g, i: (b, off // w + g))
    cmp_spec = pl.BlockSpec((1, 1, nc, w), lambda b, g, i: (b, g, 0, 0))
    return pl.pallas_call(
        functools.partial(_nsa_kernel, tq=tq, tk=tk, seq=seq),
        grid=(bsz, NSA_GROUPS, nq),
        in_specs=[pl.BlockSpec((tq, gw), lambda b, g, i: (b * nq + i, P_NQ // gw + g)),
                  cmp_spec, cmp_spec,
                  kv(P_NKS), kv(offs["NVS"]), kv(P_NKW), kv(offs["NVW"]),
                  pl.BlockSpec((tq, LANES), lambda b, g, i: (b * nq + i, offs["NG"] // LANES + g)),
                  pl.BlockSpec((tq, gw), lambda b, g, i: (b * nq + i, offs["NZ"] // gw + g))],
        out_specs=pl.BlockSpec((tq, gw), lambda b, g, i: (b * nq + i, g)),
        out_shape=jax.ShapeDtypeStruct((bsz * seq, BR_WIDTH), BF16),
        scratch_shapes=[pltpu.VMEM((-(-(seq // SLC_LEN) // 8) * 8, tq), F32)],
        compiler_params=_params("arbitrary", "arbitrary", "arbitrary"),
        name="nsa_attn",
    )(proj, kcmp, vcmp, proj, proj, proj, proj, proj, proj)


def _merge_kernel(x_ref, ya_ref, yb_ref, yc_ref, mg0_ref, mg1_ref, mg2_ref, wb_ref, wo_ref, gp_ref,
                  gate_ref, o_ref):
    merged = None
    for n, (y_ref, mg_ref) in enumerate(((ya_ref, mg0_ref), (yb_ref, mg1_ref), (yc_ref, mg2_ref))):
        t = _dot(y_ref[...], wb_ref[n]) * _sigmoid(mg_ref[...].astype(F32))
        merged = t if merged is None else merged + t
    o = _dot(merged.astype(BF16), wo_ref[...])
    o = o * lax.rsqrt(jnp.mean(o * o, axis=-1, keepdims=True) + EPS) * gp_ref[...]
    o_ref[...] = x_ref[...] + gate_ref[0] * o


def _merge(xf, ya, yb, yc, proj, wb, wo, g_post, gate, seq, tm):
    rows, d = xf.shape
    tpb = seq // tm
    row = lambda w_: pl.BlockSpec((tm, w_), lambda i: (i, 0))
    mg = lambda n: pl.BlockSpec((tm, d), lambda i: (i, P_MG // d + n))
    const = pl.Buffered(1)
    return pl.pallas_call(
        _merge_kernel,
        grid=(rows // tm,),
        in_specs=[row(d), row(BR_WIDTH), row(BR_WIDTH), row(BR_WIDTH), mg(0), mg(1), mg(2),
                  pl.BlockSpec((N_BRANCH, BR_WIDTH, d), lambda i: (0, 0, 0), pipeline_mode=const),
                  pl.BlockSpec((d, d), lambda i: (0, 0), pipeline_mode=const),
                  pl.BlockSpec((1, d), lambda i: (0, 0)),
                  pl.BlockSpec((1, 1, d), lambda i: (i // tpb, 0, 0))],
        out_specs=row(d),
        out_shape=jax.ShapeDtypeStruct((rows, d), F32),
        compiler_params=_params("arbitrary"),
        name="merge_out",
    )(xf, ya, yb, yc, proj, proj, proj, wb.astype(BF16), wo.astype(BF16), g_post.reshape(1, d),
      gate[:, None, :])


def _tiles(seq, d_model):
    return dict(tm_in=min(1024, seq), tn_in=1024, tm_merge=min(256, seq),
                diff=(min(512, seq), min(512, seq)),
                sb=(min(512, seq), min(256, seq)),
                nsa=(min(256, seq), min(512, seq)))


def kernel(x, c, norm_pre_g, norm_post_g, w_ada, b_ada, w_in, lambda_q1, lambda_k1, lambda_q2,
           lambda_k2, diff_norm_g, cmp_pe_k, cmp_w1_k, cmp_w2_k, cmp_pe_v, cmp_w1_v, cmp_w2_v,
           w_branch, w_out):
    bsz, seq, d = x.shape
    depth = w_in.shape[0]
    t = _tiles(seq, d)
    offs = _rest_offsets(d)
    assert d % 512 == 0 and ROPE_COLS % d == 0 and seq % t["tm_in"] == 0 and seq % 128 == 0
    n_total = -(-offs["END"] // t["tn_in"]) * t["tn_in"]
    tables = _rope_tables(seq)
    mod = _ada(c, w_ada, b_ada)
    xf = x.reshape(bsz * seq, d)
    for l in range(depth):
        shift, scale, gate = jnp.split(mod[l], 3, axis=-1)
        wp = _permute_w_in(w_in[l], d, n_total)
        proj = _inproj(xf, norm_pre_g[l], scale, shift, tables, wp, seq, t["tm_in"], t["tn_in"])
        ya = _diff_attn(proj, offs, diff_norm_g[l], lambda_q1[l], lambda_k1[l], lambda_q2[l],
                        lambda_k2[l], l, bsz, seq, *t["diff"])
        kcmp = _compress(proj[:, P_NKC:P_NKC + NSA_KV], cmp_pe_k[l], cmp_w1_k[l], cmp_w2_k[l], bsz, seq)
        vcmp = _compress(proj[:, offs["NVC"]:offs["NVC"] + NSA_KV], cmp_pe_v[l], cmp_w1_v[l],
                         cmp_w2_v[l], bsz, seq)
        yb = _nsa_attn(proj, offs, kcmp, vcmp, bsz, seq, *t["nsa"])
        yc = _sb_attn(proj, offs, bsz, seq, *t["sb"])
        xf = _merge(xf, ya, yb, yc, proj, w_branch[l], w_out[l], norm_post_g[l], gate, seq,
                    t["tm_merge"])
    return xf.reshape(bsz, seq, d)
```

```python
import functools
import math

import jax
import jax.numpy as jnp
from jax import lax
from jax.experimental import pallas as pl
from jax.experimental.pallas import tpu as pltpu

F32 = jnp.float32
BF16 = jnp.bfloat16

HEAD_DIM = 128
BR_WIDTH = 1024
N_BRANCH = 3
ROPE_THETA = 500000.0
ROPE_DIM = HEAD_DIM // 4
ROPE_HALF = ROPE_DIM // 2
EPS = 1e-6
NEG_INF = -1e30
FORCE_SCORE = 1e6
DIFF_HEADS = BR_WIDTH // (2 * HEAD_DIM)
DIFF_NORM_EPS = 1e-5
NSA_HEADS = BR_WIDTH // HEAD_DIM
NSA_GROUPS = 2
NSA_REP = NSA_HEADS // NSA_GROUPS
CMP_LEN = 32
CMP_STRIDE = 16
SLC_LEN = 64
SLC_SHIFT = SLC_LEN.bit_length() - 1
SLC_TOPK = 16
WINDOW = 512
SB_HEADS = BR_WIDTH // HEAD_DIM
NSA_KV = NSA_GROUPS * HEAD_DIM
ATTN_SCALE = HEAD_DIM ** -0.5
Q_PRESCALE = ATTN_SCALE * math.log2(math.e)

LANES = 128
SLC_BLOCKS_PAD = 128
UNSELECTED_BIAS = 2.0 ** 30
SB_UNDERFLOW = -104.0
VMEM_LIMIT = 56 * 1024 * 1024

_O_AQ, _O_AK, _O_AV, _O_AZ = 0, 1024, 2048, 3072
_O_NQ, _O_NKC, _O_NVC, _O_NKS, _O_NVS, _O_NKW, _O_NVW = 4096, 5120, 5376, 5632, 5888, 6144, 6400
_O_NG, _O_NZ = 6656, 6680
_O_CQ, _O_CK, _O_CV, _O_CZ, _O_MG = 7704, 8728, 9752, 10776, 11800
N_GATES = 3 * NSA_HEADS
P_AQ, P_AK, P_NQ, P_NKC, P_NKS, P_NKW = 0, 1024, 2048, 3072, 3328, 3584
ROPE_COLS = 4096
P_MG = ROPE_COLS


def _rest_offsets(d_model):
    r0 = ROPE_COLS + N_BRANCH * d_model
    offs = dict(AV=r0, AZ=r0 + 1024, NZ=r0 + 2048, CQ=r0 + 3072, CK=r0 + 4096, CV=r0 + 5120,
                CZ=r0 + 6144, NVC=r0 + 7168, NVS=r0 + 7424, NVW=r0 + 7680, NG=r0 + 7936)
    offs["END"] = r0 + 7936 + NSA_GROUPS * LANES
    return offs


def _permute_w_in(w, d_model, n_total):
    d = w.shape[0]
    z = lambda n: jnp.zeros((d, n), w.dtype)
    gates_per_group = N_GATES // NSA_GROUPS
    segs = [w[:, _O_AQ:_O_AQ + 1024], w[:, _O_AK:_O_AK + 1024], w[:, _O_NQ:_O_NQ + 1024],
            w[:, _O_NKC:_O_NKC + 256], w[:, _O_NKS:_O_NKS + 256], w[:, _O_NKW:_O_NKW + 256],
            z(ROPE_COLS - 3840),
            w[:, _O_MG:_O_MG + N_BRANCH * d_model],
            w[:, _O_AV:_O_AV + 1024], w[:, _O_AZ:_O_AZ + 1024], w[:, _O_NZ:_O_NZ + 1024],
            w[:, _O_CQ:_O_CQ + 4096],
            w[:, _O_NVC:_O_NVC + 256], w[:, _O_NVS:_O_NVS + 256], w[:, _O_NVW:_O_NVW + 256]]
    for g in range(NSA_GROUPS):
        segs += [w[:, _O_NG + g * gates_per_group:_O_NG + (g + 1) * gates_per_group],
                 z(LANES - gates_per_group)]
    used = _rest_offsets(d_model)["END"]
    if n_total > used:
        segs.append(z(n_total - used))
    return jnp.concatenate(segs, axis=1).astype(BF16)


def _rope_tables(seq):
    pos = jnp.arange(seq, dtype=F32)
    inv = ROPE_THETA ** (-jnp.arange(0, ROPE_DIM, 2, dtype=F32) / ROPE_DIM)
    ang = pos[:, None] * inv[None, :]
    cos, sin = jnp.cos(ang), jnp.sin(ang)
    ones = jnp.ones((seq, HEAD_DIM - ROPE_DIM), F32)
    zh = jnp.zeros((seq, ROPE_HALF), F32)
    zr = jnp.zeros((seq, HEAD_DIM - ROPE_DIM), F32)
    c = jnp.concatenate([cos, cos, ones], axis=1)
    s1 = jnp.concatenate([-sin, zh, zr], axis=1)
    s2 = jnp.concatenate([zh, sin, zr], axis=1)
    return c, s1, s2


def _sigmoid(x):
    return 1.0 / (1.0 + jnp.exp(-x))


def _silu(x):
    return x * _sigmoid(x)


def _dot_nt(a, b):
    return lax.dot_general(a, b, (((1,), (1,)), ((), ())), preferred_element_type=F32)


def _dot(a, b):
    return jnp.dot(a, b, preferred_element_type=F32)


def _params(*sem):
    return pltpu.CompilerParams(dimension_semantics=sem, vmem_limit_bytes=VMEM_LIMIT)


def _ada_kernel(c_ref, w_ref, b_ref, o_ref):
    c = c_ref[...]
    o_ref[0] = _dot(_silu(c).astype(BF16), w_ref[0].astype(BF16)) + b_ref[0]


def _ada(c, w_ada, b_ada):
    depth, d, n3 = w_ada.shape
    bsz = c.shape[0]
    rows = max(8, bsz)
    cp = jnp.zeros((rows, d), F32).at[:bsz].set(c)
    tn = math.gcd(1024, n3)
    out = pl.pallas_call(
        _ada_kernel,
        grid=(depth, n3 // tn),
        in_specs=[pl.BlockSpec((rows, d), lambda l, j: (0, 0)),
                  pl.BlockSpec((1, d, tn), lambda l, j: (l, 0, j)),
                  pl.BlockSpec((1, 1, tn), lambda l, j: (l, 0, j))],
        out_specs=pl.BlockSpec((1, rows, tn), lambda l, j: (l, 0, j)),
        out_shape=jax.ShapeDtypeStruct((depth, rows, n3), F32),
        compiler_params=_params("arbitrary", "arbitrary"),
        name="ada_mod",
    )(cp, w_ada, b_ada.reshape(depth, 1, n3))
    return out[:, :bsz]


def _modulated_norm(x, g, scale, shift):
    y = x * lax.rsqrt(jnp.mean(x * x, axis=-1, keepdims=True) + EPS) * g
    return (y * (1.0 + scale) + shift).astype(BF16)


def _prenorm_kernel(x_ref, g_ref, sc_ref, sh_ref, h_ref):
    h_ref[...] = _modulated_norm(x_ref[...], g_ref[...], sc_ref[0], sh_ref[0])


def _prenorm(xf, g, scale, shift, seq, tm):
    rows, d = xf.shape
    tpb = seq // tm
    mod_spec = pl.BlockSpec((1, 1, d), lambda i: (i // tpb, 0, 0))
    return pl.pallas_call(
        _prenorm_kernel,
        grid=(rows // tm,),
        in_specs=[pl.BlockSpec((tm, d), lambda i: (i, 0)), pl.BlockSpec((1, d), lambda i: (0, 0)),
                  mod_spec, mod_spec],
        out_specs=pl.BlockSpec((tm, d), lambda i: (i, 0)),
        out_shape=jax.ShapeDtypeStruct((rows, d), BF16),
        compiler_params=_params("arbitrary"),
        name="pre_norm",
    )(xf, g.reshape(1, d), scale[:, None, :], shift[:, None, :])


def _inproj_kernel(h_ref, c_ref, s1_ref, s2_ref, w_ref, o_ref, *, n_rope_tiles, tn):
    j = pl.program_id(1)
    y = _dot(h_ref[...], w_ref[...])

    @pl.when(j < n_rope_tiles)
    def _():
        c, s1, s2 = c_ref[...], s1_ref[...], s2_ref[...]
        for hh in range(tn // HEAD_DIM):
            sl = slice(hh * HEAD_DIM, (hh + 1) * HEAD_DIM)
            yh = y[:, sl]
            r = (yh * c + pltpu.roll(yh, HEAD_DIM - ROPE_HALF, 1) * s1
                 + pltpu.roll(yh, ROPE_HALF, 1) * s2)
            col = j * tn + hh * HEAD_DIM
            is_q = ((col >= P_AQ) & (col < P_AQ + BR_WIDTH)) | ((col >= P_NQ) & (col < P_NQ + BR_WIDTH))
            o_ref[:, sl] = (r * jnp.where(is_q, Q_PRESCALE, 1.0)).astype(BF16)

    @pl.when(j >= n_rope_tiles)
    def _():
        o_ref[...] = y.astype(BF16)


def _inproj(h, tables, wp, seq, tm, tn):
    rows, d = h.shape
    n_total = wp.shape[1]
    tpb = seq // tm
    c, s1, s2 = tables
    tab_spec = pl.BlockSpec((tm, HEAD_DIM), lambda i, j: (i % tpb, 0))
    return pl.pallas_call(
        functools.partial(_inproj_kernel, n_rope_tiles=ROPE_COLS // tn, tn=tn),
        grid=(rows // tm, n_total // tn),
        in_specs=[pl.BlockSpec((tm, d), lambda i, j: (i, 0)),
                  tab_spec, tab_spec, tab_spec,
                  pl.BlockSpec((d, tn), lambda i, j: (0, j))],
        out_specs=pl.BlockSpec((tm, tn), lambda i, j: (i, j)),
        out_shape=jax.ShapeDtypeStruct((rows, n_total), BF16),
        compiler_params=_params("arbitrary", "arbitrary"),
        name="in_proj",
    )(h, c, s1, s2, wp)


def _softmax_update(carry, s, v):
    m, l, acc = carry
    m_new = jnp.maximum(m, jnp.max(s, axis=-1, keepdims=True))
    alpha = jnp.exp2(m - m_new)
    p = jnp.exp2(s - m_new)
    l = alpha * l + jnp.sum(p, axis=-1, keepdims=True)
    acc = alpha * acc + _dot(p.astype(BF16), v)
    return m_new, l, acc


def _softmax_init(rows, width):
    return (jnp.full((rows, 1), NEG_INF, F32), jnp.zeros((rows, 1), F32),
            jnp.zeros((rows, width), F32))


def _diff_kernel(q_ref, k_ref, v_ref, z_ref, ng_ref, lq1_ref, lk1_ref, lq2_ref, lk2_ref, o_ref,
                 *, tq, tk, lam_init):
    i = pl.program_id(2)
    t0 = i * tq
    q = q_ref[...]
    qs = (q[:, :HEAD_DIM], q[:, HEAD_DIM:])
    n_full = t0 // tk

    def step(jt, carry, masked):
        k0 = pl.multiple_of(jt * tk, tk)
        k = k_ref[pl.ds(k0, tk), :]
        v = v_ref[pl.ds(k0, tk), :]
        s = [_dot_nt(qs[cc], k[:, cc * HEAD_DIM:(cc + 1) * HEAD_DIM]) for cc in range(2)]
        if masked:
            kpos = k0 + lax.broadcasted_iota(jnp.int32, (tq, tk), 1)
            tpos = t0 + lax.broadcasted_iota(jnp.int32, (tq, tk), 0)
            s = [jnp.where(kpos <= tpos, sc, NEG_INF) for sc in s]
        return tuple(_softmax_update(carry[cc], s[cc], v) for cc in range(2))

    init = (_softmax_init(tq, 2 * HEAD_DIM), _softmax_init(tq, 2 * HEAD_DIM))
    carry = lax.fori_loop(0, n_full, lambda jt, c: step(jt, c, False), init)
    carry = step(n_full, carry, True)

    lam = (jnp.exp(jnp.sum(lq1_ref[...] * lk1_ref[...], axis=-1, keepdims=True))
           - jnp.exp(jnp.sum(lq2_ref[...] * lk2_ref[...], axis=-1, keepdims=True)) + lam_init)
    o0 = carry[0][2] / carry[0][1]
    o1 = carry[1][2] / carry[1][1]
    o = o0 - lam * o1
    o = o * lax.rsqrt(jnp.mean(o * o, axis=-1, keepdims=True) + DIFF_NORM_EPS) * ng_ref[...]
    o = o * (1.0 - lam_init)
    o_ref[...] = (o * _silu(z_ref[...].astype(F32))).astype(BF16)


def _diff_attn(proj, offs, norm_g, lq1, lk1, lq2, lk2, layer_idx, bsz, seq, tq, tk):
    nq = seq // tq
    w = 2 * HEAD_DIM
    lam_init = 0.8 - 0.6 * math.exp(-0.3 * layer_idx)
    vec = pl.BlockSpec((1, HEAD_DIM), lambda b, h, i: (0, 0))
    return pl.pallas_call(
        functools.partial(_diff_kernel, tq=tq, tk=tk, lam_init=lam_init),
        grid=(bsz, DIFF_HEADS, nq),
        in_specs=[pl.BlockSpec((tq, w), lambda b, h, i: (b * nq + i, P_AQ // w + h)),
                  pl.BlockSpec((seq, w), lambda b, h, i: (b, P_AK // w + h)),
                  pl.BlockSpec((seq, w), lambda b, h, i: (b, offs["AV"] // w + h)),
                  pl.BlockSpec((tq, w), lambda b, h, i: (b * nq + i, offs["AZ"] // w + h)),
                  pl.BlockSpec((1, w), lambda b, h, i: (0, 0)),
                  vec, vec, vec, vec],
        out_specs=pl.BlockSpec((tq, w), lambda b, h, i: (b * nq + i, h)),
        out_shape=jax.ShapeDtypeStruct((bsz * seq, BR_WIDTH), BF16),
        compiler_params=_params("arbitrary", "arbitrary", "arbitrary"),
        name="diff_attn",
    )(proj, proj, proj, proj, norm_g.reshape(1, w), lq1.reshape(1, -1), lk1.reshape(1, -1),
      lq2.reshape(1, -1), lk2.reshape(1, -1))


def _sb_kernel(q_ref, k_ref, v_ref, z_ref, o_ref, *, tq, tk):
    i = pl.program_id(2)
    t0 = i * tq
    q = q_ref[...]
    n_full = t0 // tk
    n_diag = max(1, tq // tk)
    upper2 = jnp.where((lax.broadcasted_iota(jnp.int32, (2 * tk, tk), 0) & (tk - 1))
                       > lax.broadcasted_iota(jnp.int32, (2 * tk, tk), 1), 1.0, 0.0).astype(BF16)

    def step(jt, carry, masked):
        later, acc = carry
        k0 = pl.multiple_of(jt * tk, tk)
        k = k_ref[pl.ds(k0, tk), :]
        v = v_ref[pl.ds(k0, tk), :]
        z = _dot_nt(q, k) * ATTN_SCALE
        log_1m = -(jnp.maximum(z, 0.0) + jnp.log(1.0 + jnp.exp(-jnp.abs(z))))
        if masked:
            strict = (k0 + lax.broadcasted_iota(jnp.int32, (tq, tk), 1)
                      < t0 + lax.broadcasted_iota(jnp.int32, (tq, tk), 0))
            log_1m = jnp.where(strict, log_1m, 0.0)
        hi = log_1m.astype(BF16)
        lo = (log_1m - hi.astype(F32)).astype(BF16)
        after = _dot(jnp.concatenate([hi, lo], axis=1), upper2) + later
        a = jnp.exp(z + log_1m + after)
        if masked:
            a = jnp.where(strict, a, 0.0)
        acc = acc + _dot(a.astype(BF16), v)
        later = later + jnp.sum(log_1m, axis=-1, keepdims=True)
        return later, acc

    carry = (jnp.zeros((tq, 1), F32), jnp.zeros((tq, HEAD_DIM), F32))
    for dd in range(n_diag - 1, -1, -1):
        carry = step(n_full + dd, carry, True)

    def cond(c):
        return (c[0] < n_full) & (c[1] > SB_UNDERFLOW)

    def body(c):
        later, acc = step(n_full - 1 - c[0], (c[2], c[3]), False)
        return c[0] + 1, jnp.max(later), later, acc

    acc = lax.while_loop(cond, body, (jnp.int32(0), jnp.max(carry[0]), carry[0], carry[1]))[3]
    o_ref[...] = (acc * _silu(z_ref[...].astype(F32))).astype(BF16)


def _sb_attn(proj, offs, bsz, seq, tq, tk):
    nq = seq // tq
    w = HEAD_DIM
    return pl.pallas_call(
        functools.partial(_sb_kernel, tq=tq, tk=tk),
        grid=(bsz, SB_HEADS, nq),
        in_specs=[pl.BlockSpec((tq, w), lambda b, h, i: (b * nq + i, offs["CQ"] // w + h)),
                  pl.BlockSpec((seq, w), lambda b, h, i: (b, offs["CK"] // w + h)),
                  pl.BlockSpec((seq, w), lambda b, h, i: (b, offs["CV"] // w + h)),
                  pl.BlockSpec((tq, w), lambda b, h, i: (b * nq + i, offs["CZ"] // w + h))],
        out_specs=pl.BlockSpec((tq, w), lambda b, h, i: (b * nq + i, h)),
        out_shape=jax.ShapeDtypeStruct((bsz * seq, BR_WIDTH), BF16),
        compiler_params=_params("arbitrary", "arbitrary", "arbitrary"),
        name="stick_breaking",
    )(proj, proj, proj, proj)


def _compress_kernel(x_ref, pe_ref, w1_ref, w2_ref, o_ref, xs_ref, *, seq):
    nc = seq // CMP_STRIDE
    xs_ref[pl.ds(0, seq), :] = x_ref[...].astype(F32)
    xs_ref[pl.ds(seq, CMP_STRIDE), :] = jnp.zeros((CMP_STRIDE, HEAD_DIM), F32)
    hid = None
    for l in range(CMP_LEN):
        rows = xs_ref[pl.ds(l, nc, stride=CMP_STRIDE), :] + pe_ref[l:l + 1, :]
        t = _dot(rows.astype(BF16), w1_ref[l * HEAD_DIM:(l + 1) * HEAD_DIM, :])
        hid = t if hid is None else hid + t
    o_ref[0, 0] = _dot(_silu(hid).astype(BF16), w2_ref[...]).astype(BF16)


def _compress(proj, col_off, pe, w1, w2, bsz, seq):
    nc = seq // CMP_STRIDE
    w = HEAD_DIM
    return pl.pallas_call(
        functools.partial(_compress_kernel, seq=seq),
        grid=(bsz, NSA_GROUPS),
        in_specs=[pl.BlockSpec((seq, w), lambda b, g: (b, col_off // w + g)),
                  pl.BlockSpec((CMP_LEN, w), lambda b, g: (0, 0)),
                  pl.BlockSpec((CMP_LEN * w, w), lambda b, g: (0, 0)),
                  pl.BlockSpec((w, w), lambda b, g: (0, 0))],
        out_specs=pl.BlockSpec((1, 1, nc, w), lambda b, g: (b, g, 0, 0)),
        out_shape=jax.ShapeDtypeStruct((bsz, NSA_GROUPS, nc, w), BF16),
        scratch_shapes=[pltpu.VMEM((seq + CMP_STRIDE, w), F32)],
        compiler_params=_params("arbitrary", "arbitrary"),
        name="nsa_compress",
    )(proj, pe, w1.astype(BF16), w2.astype(BF16))


def _nsa_kernel(q_ref, kc_ref, vc_ref, ks_ref, vs_ref, kw_ref, vw_ref, g_ref, z_ref, o_ref, imp_ref,
                *, tq, tk, seq):
    i = pl.program_id(2)
    t0 = i * tq
    nc = seq // CMP_STRIDE
    nb = seq // SLC_LEN
    nbp = SLC_BLOCKS_PAD
    rep = NSA_REP
    q = q_ref[...]
    qh = [q[:, r * HEAD_DIM:(r + 1) * HEAD_DIM] for r in range(rep)]

    kc, vc = kc_ref[0, 0], vc_ref[0, 0]
    tpos_c = t0 + lax.broadcasted_iota(jnp.int32, (tq, nc), 0)
    ncol = lax.broadcasted_iota(jnp.int32, (tq, nc), 1)
    cvalid = CMP_STRIDE * ncol + (CMP_LEN - 1) <= tpos_c
    psum = jnp.zeros((tq, nc), F32)
    o_cmp = []
    for r in range(rep):
        s = jnp.where(cvalid, _dot_nt(qh[r], kc), NEG_INF)
        e = jnp.where(cvalid, jnp.exp2(s - jnp.max(s, axis=-1, keepdims=True)), 0.0)
        den = jnp.sum(e, axis=-1, keepdims=True)
        p = e / jnp.where(den > 0.0, den, 1.0)
        psum = psum + p
        o_cmp.append(_dot(p.astype(BF16), vc))

    nb8 = imp_ref.shape[0]
    jrow = lax.broadcasted_iota(jnp.int32, (nb8, nc), 0)
    ncol2 = lax.broadcasted_iota(jnp.int32, (nb8, nc), 1)
    overlap = jnp.where((CMP_STRIDE * ncol2 < SLC_LEN * jrow + SLC_LEN)
                        & (CMP_STRIDE * ncol2 + CMP_LEN > SLC_LEN * jrow)
                        & (ncol2 < nc - 1) & (jrow < nb), 1.0, 0.0).astype(BF16)
    p1 = psum.astype(BF16)
    r1 = psum - p1.astype(F32)
    p2 = r1.astype(BF16)
    p3 = (r1 - p2.astype(F32)).astype(BF16)
    imp = _dot_nt(overlap, p1) + _dot_nt(overlap, p2) + _dot_nt(overlap, p3)
    jt_ = lax.broadcasted_iota(jnp.int32, (nb8, tq), 0)
    tblk = jnp.right_shift(t0 + lax.broadcasted_iota(jnp.int32, (nb8, tq), 1), SLC_SHIFT)
    imp = jnp.where((jt_ == tblk) | (jt_ == 0), FORCE_SCORE, imp)
    imp = jnp.where(jt_ <= tblk, imp, NEG_INF)
    imp_ref[...] = imp

    def rank_body(jp, cnt):
        row = imp_ref[pl.ds(jp, 1), :]
        beats = (row > imp) | ((row == imp) & (jp < jt_))
        return cnt + jnp.where(beats, 1.0, 0.0)

    n_live = jnp.minimum((t0 + tq - 1) // SLC_LEN + 1, nb)
    cnt = lax.fori_loop(0, n_live, rank_body, jnp.zeros((nb8, tq), F32))
    keep = (cnt < float(min(SLC_TOPK, nb))) & (imp > 0.5 * NEG_INF)
    unsel_t = jnp.where(keep, 0.0, 1.0)
    if nbp > nb8:
        unsel_t = jnp.concatenate([unsel_t, jnp.zeros((nbp - nb8, tq), F32)], axis=0)
    unselected = unsel_t.T.astype(BF16)

    rows = rep * tq
    q_sel = jnp.concatenate([jnp.concatenate([qh[r], unselected], axis=1) for r in range(rep)], axis=0)
    q_win = jnp.concatenate(qh, axis=0)

    def slc_step(jt, carry, masked):
        k0 = pl.multiple_of(jt * tk, tk)
        k = ks_ref[pl.ds(k0, tk), :]
        v = vs_ref[pl.ds(k0, tk), :]
        kblk = jnp.right_shift(k0 + lax.broadcasted_iota(jnp.int32, (tk, nbp), 0), SLC_SHIFT)
        bias = jnp.where(lax.broadcasted_iota(jnp.int32, (tk, nbp), 1) == kblk,
                         -UNSELECTED_BIAS, 0.0).astype(BF16)
        s = _dot_nt(q_sel, jnp.concatenate([k, bias], axis=1))
        if masked:
            causal = (k0 + lax.broadcasted_iota(jnp.int32, (tq, tk), 1)
                      <= t0 + lax.broadcasted_iota(jnp.int32, (tq, tk), 0))
            s = s + jnp.concatenate([jnp.where(causal, 0.0, NEG_INF)] * rep, axis=0)
        return _softmax_update(carry, s, v)

    n_full = t0 // tk
    carry = lax.fori_loop(0, n_full, lambda jt, c: slc_step(jt, c, False), _softmax_init(rows, HEAD_DIM))
    m_s, l_s, acc_s = slc_step(n_full, carry, True)
    o_slc = acc_s / l_s

    span = min(WINDOW + tq, seq)
    start = pl.multiple_of(jnp.maximum(t0 - WINDOW, 0), tq)
    kpos = start + lax.broadcasted_iota(jnp.int32, (tq, span), 1)
    row_w = t0 + lax.broadcasted_iota(jnp.int32, (tq, span), 0)
    in_window = jnp.where(kpos <= row_w, jnp.where(kpos > row_w - WINDOW, 0.0, NEG_INF), NEG_INF)
    s = _dot_nt(q_win, kw_ref[pl.ds(start, span), :]) + jnp.concatenate([in_window] * rep, axis=0)
    p = jnp.exp2(s - jnp.max(s, axis=-1, keepdims=True))
    o_win = _dot(p.astype(BF16), vw_ref[pl.ds(start, span), :]) / jnp.sum(p, axis=-1, keepdims=True)

    gates = _sigmoid(g_ref[...].astype(F32))
    z = z_ref[...].astype(F32)
    for r in range(rep):
        rs = slice(r * tq, (r + 1) * tq)
        cs = slice(r * HEAD_DIM, (r + 1) * HEAD_DIM)
        o = (gates[:, 3 * r:3 * r + 1] * o_cmp[r] + gates[:, 3 * r + 1:3 * r + 2] * o_slc[rs]
             + gates[:, 3 * r + 2:3 * r + 3] * o_win[rs])
        o_ref[:, cs] = (o * _silu(z[:, cs])).astype(BF16)


def _nsa_attn(proj, offs, kcmp, vcmp, bsz, seq, tq, tk):
    nq = seq // tq
    nc = seq // CMP_STRIDE
    w = HEAD_DIM
    gw = NSA_REP * HEAD_DIM
    assert seq // SLC_LEN <= SLC_BLOCKS_PAD and tq & (tq - 1) == 0 and tk % tq == 0
    kv = lambda off: pl.BlockSpec((seq, w), lambda b, g, i: (b, off // w + g))
    cmp_spec = pl.BlockSpec((1, 1, nc, w), lambda b, g, i: (b, g, 0, 0))
    return pl.pallas_call(
        functools.partial(_nsa_kernel, tq=tq, tk=tk, seq=seq),
        grid=(bsz, NSA_GROUPS, nq),
        in_specs=[pl.BlockSpec((tq, gw), lambda b, g, i: (b * nq + i, P_NQ // gw + g)),
                  cmp_spec, cmp_spec,
                  kv(P_NKS), kv(offs["NVS"]), kv(P_NKW), kv(offs["NVW"]),
                  pl.BlockSpec((tq, LANES), lambda b, g, i: (b * nq + i, offs["NG"] // LANES + g)),
                  pl.BlockSpec((tq, gw), lambda b, g, i: (b * nq + i, offs["NZ"] // gw + g))],
        out_specs=pl.BlockSpec((tq, gw), lambda b, g, i: (b * nq + i, g)),
        out_shape=jax.ShapeDtypeStruct((bsz * seq, BR_WIDTH), BF16),
        scratch_shapes=[pltpu.VMEM((-(-(seq // SLC_LEN) // 8) * 8, tq), F32)],
        compiler_params=_params("arbitrary", "arbitrary", "arbitrary"),
        name="nsa_attn",
    )(proj, kcmp, vcmp, proj, proj, proj, proj, proj, proj)


def _merge_kernel(x_ref, ya_ref, yb_ref, yc_ref, mg0_ref, mg1_ref, mg2_ref, wb_ref, wo_ref, gp_ref,
                  gate_ref, *rest, emit_h):
    merged = None
    for n, (y_ref, mg_ref) in enumerate(((ya_ref, mg0_ref), (yb_ref, mg1_ref), (yc_ref, mg2_ref))):
        t = _dot(y_ref[...], wb_ref[n]) * _sigmoid(mg_ref[...].astype(F32))
        merged = t if merged is None else merged + t
    o = _dot(merged.astype(BF16), wo_ref[...])
    o = o * lax.rsqrt(jnp.mean(o * o, axis=-1, keepdims=True) + EPS) * gp_ref[...]
    x_new = x_ref[...] + gate_ref[0] * o
    if emit_h:
        g_next_ref, sc_next_ref, sh_next_ref, o_ref, h_ref = rest
        h_ref[...] = _modulated_norm(x_new, g_next_ref[...], sc_next_ref[0], sh_next_ref[0])
    else:
        (o_ref,) = rest
    o_ref[...] = x_new


def _merge(xf, ya, yb, yc, proj, wb, wo, g_post, gate, next_norm, seq, tm):
    rows, d = xf.shape
    tpb = seq // tm
    row = lambda w_: pl.BlockSpec((tm, w_), lambda i: (i, 0))
    mg = lambda n: pl.BlockSpec((tm, d), lambda i: (i, P_MG // d + n))
    vec = pl.BlockSpec((1, d), lambda i: (0, 0))
    mod_spec = pl.BlockSpec((1, 1, d), lambda i: (i // tpb, 0, 0))
    const = pl.Buffered(1)
    emit_h = next_norm is not None
    in_specs = [row(d), row(BR_WIDTH), row(BR_WIDTH), row(BR_WIDTH), mg(0), mg(1), mg(2),
                pl.BlockSpec((N_BRANCH, BR_WIDTH, d), lambda i: (0, 0, 0), pipeline_mode=const),
                pl.BlockSpec((d, d), lambda i: (0, 0), pipeline_mode=const), vec, mod_spec]
    args = [xf, ya, yb, yc, proj, proj, proj, wb.astype(BF16), wo.astype(BF16), g_post.reshape(1, d),
            gate[:, None, :]]
    out_specs, out_shape = row(d), jax.ShapeDtypeStruct((rows, d), F32)
    if emit_h:
        g_next, sc_next, sh_next = next_norm
        in_specs += [vec, mod_spec, mod_spec]
        args += [g_next.reshape(1, d), sc_next[:, None, :], sh_next[:, None, :]]
        out_specs, out_shape = [out_specs, row(d)], [out_shape, jax.ShapeDtypeStruct((rows, d), BF16)]
    out = pl.pallas_call(
        functools.partial(_merge_kernel, emit_h=emit_h),
        grid=(rows // tm,),
        in_specs=in_specs,
        out_specs=out_specs,
        out_shape=out_shape,
        compiler_params=_params("arbitrary"),
        name="merge_out",
    )(*args)
    return (out[0], out[1]) if emit_h else (out, None)


def _tiles(seq, d_model):
    return dict(tm_in=min(1024, seq), tn_in=2048, tm_merge=min(256, seq),
                diff=(min(512, seq), min(512, seq)),
                sb=(min(512, seq), min(256, seq)),
                nsa=(min(256, seq), min(512, seq)))


def kernel(x, c, norm_pre_g, norm_post_g, w_ada, b_ada, w_in, lambda_q1, lambda_k1, lambda_q2,
           lambda_k2, diff_norm_g, cmp_pe_k, cmp_w1_k, cmp_w2_k, cmp_pe_v, cmp_w1_v, cmp_w2_v,
           w_branch, w_out):
    bsz, seq, d = x.shape
    depth = w_in.shape[0]
    t = _tiles(seq, d)
    offs = _rest_offsets(d)
    assert d % 512 == 0 and ROPE_COLS % d == 0 and seq % t["tm_in"] == 0 and seq % 128 == 0
    n_total = -(-offs["END"] // t["tn_in"]) * t["tn_in"]
    tables = _rope_tables(seq)
    mod = _ada(c, w_ada, b_ada)
    xf = x.reshape(bsz * seq, d)
    mods = [jnp.split(mod[l], 3, axis=-1) for l in range(depth)]
    h = _prenorm(xf, norm_pre_g[0], mods[0][1], mods[0][0], seq, t["tm_merge"])
    for l in range(depth):
        wp = _permute_w_in(w_in[l], d, n_total)
        proj = _inproj(h, tables, wp, seq, t["tm_in"], t["tn_in"])
        ya = _diff_attn(proj, offs, diff_norm_g[l], lambda_q1[l], lambda_k1[l], lambda_q2[l],
                        lambda_k2[l], l, bsz, seq, *t["diff"])
        kcmp = _compress(proj, P_NKC, cmp_pe_k[l], cmp_w1_k[l], cmp_w2_k[l], bsz, seq)
        vcmp = _compress(proj, offs["NVC"], cmp_pe_v[l], cmp_w1_v[l], cmp_w2_v[l], bsz, seq)
        yb = _nsa_attn(proj, offs, kcmp, vcmp, bsz, seq, *t["nsa"])
        yc = _sb_attn(proj, offs, bsz, seq, *t["sb"])
        next_norm = (norm_pre_g[l + 1], mods[l + 1][1], mods[l + 1][0]) if l + 1 < depth else None
        xf, h = _merge(xf, ya, yb, yc, proj, w_branch[l], w_out[l], norm_post_g[l], mods[l][2],
                       next_norm, seq, t["tm_merge"])
    return xf.reshape(bsz, seq, d)
```

```python
import functools
import math

import jax
import jax.numpy as jnp
import numpy as np
from jax import lax
from jax.experimental import pallas as pl
from jax.experimental.pallas import tpu as pltpu

F32 = jnp.float32
BF16 = jnp.bfloat16

HEAD_DIM = 128
BR_WIDTH = 1024
N_BRANCH = 3
ROPE_THETA = 500000.0
ROPE_DIM = HEAD_DIM // 4
ROPE_HALF = ROPE_DIM // 2
EPS = 1e-6
NEG_INF = -1e30
FORCE_SCORE = 1e6
DIFF_HEADS = BR_WIDTH // (2 * HEAD_DIM)
DIFF_NORM_EPS = 1e-5
NSA_HEADS = BR_WIDTH // HEAD_DIM
NSA_GROUPS = 2
NSA_REP = NSA_HEADS // NSA_GROUPS
CMP_LEN = 32
CMP_STRIDE = 16
SLC_LEN = 64
SLC_SHIFT = SLC_LEN.bit_length() - 1
SLC_TOPK = 16
WINDOW = 512
SB_HEADS = BR_WIDTH // HEAD_DIM
NSA_KV = NSA_GROUPS * HEAD_DIM
ATTN_SCALE = HEAD_DIM ** -0.5
Q_PRESCALE = ATTN_SCALE * math.log2(math.e)

LANES = 128
SLC_BLOCKS_PAD = 128
UNSELECTED_BIAS = 2.0 ** 30
SB_UNDERFLOW = -104.0
VMEM_LIMIT = 56 * 1024 * 1024

_O_AQ, _O_AK, _O_AV, _O_AZ = 0, 1024, 2048, 3072
_O_NQ, _O_NKC, _O_NVC, _O_NKS, _O_NVS, _O_NKW, _O_NVW = 4096, 5120, 5376, 5632, 5888, 6144, 6400
_O_NG, _O_NZ = 6656, 6680
_O_CQ, _O_CK, _O_CV, _O_CZ, _O_MG = 7704, 8728, 9752, 10776, 11800
N_GATES = 3 * NSA_HEADS
P_AQ, P_AK, P_NQ, P_NKC, P_NKS, P_NKW = 0, 1024, 2048, 3072, 3328, 3584
ROPE_COLS = 4096
P_MG = ROPE_COLS


def _rest_offsets(d_model):
    r0 = ROPE_COLS + N_BRANCH * d_model
    offs = dict(AV=r0, AZ=r0 + 1024, NZ=r0 + 2048, CQ=r0 + 3072, CK=r0 + 4096, CV=r0 + 5120,
                CZ=r0 + 6144, NVC=r0 + 7168, NVS=r0 + 7424, NVW=r0 + 7680, NG=r0 + 7936)
    offs["END"] = r0 + 7936 + NSA_GROUPS * LANES
    return offs


PREP_TILE = 2 * LANES
PREP_SHIFT = _O_NZ % LANES
GATES_PER_GROUP = N_GATES // NSA_GROUPS
_ZERO, _ALIGNED, _SHIFTED, _GATES = 0, 1, 2, 3


def _prep_plan(d_model, n_total):
    offs = _rest_offsets(d_model)
    pieces = [(P_AQ, _O_AQ, 1024), (P_AK, _O_AK, 1024), (P_NQ, _O_NQ, 1024), (P_NKC, _O_NKC, 256),
              (P_NKS, _O_NKS, 256), (P_NKW, _O_NKW, 256), (P_MG, _O_MG, N_BRANCH * d_model),
              (offs["AV"], _O_AV, 1024), (offs["AZ"], _O_AZ, 1024), (offs["NZ"], _O_NZ, 1024),
              (offs["CQ"], _O_CQ, 4096), (offs["NVC"], _O_NVC, 256), (offs["NVS"], _O_NVS, 256),
              (offs["NVW"], _O_NVW, 256)]
    n_tiles = n_total // PREP_TILE
    kind = np.full((n_tiles,), _ZERO, np.int32)
    main = np.zeros((n_tiles,), np.int32)
    for dest, src, width in pieces:
        for off in range(0, width, PREP_TILE):
            shift = (src + off) % LANES
            base = src + off - shift
            assert shift in (0, PREP_SHIFT) and base % PREP_TILE == 0 and (dest + off) % PREP_TILE == 0
            kind[(dest + off) // PREP_TILE] = _SHIFTED if shift else _ALIGNED
            main[(dest + off) // PREP_TILE] = base // PREP_TILE
    assert offs["NG"] % PREP_TILE == 0 and _O_NG % PREP_TILE == 0 and NSA_GROUPS * LANES == PREP_TILE
    kind[offs["NG"] // PREP_TILE] = _GATES
    main[offs["NG"] // PREP_TILE] = _O_NG // PREP_TILE
    return jnp.asarray(kind), jnp.asarray(main)


def _prep_kernel(kind_ref, main_ref, a_ref, b_ref, o_ref):
    kind = kind_ref[pl.program_id(1)]
    rows = a_ref.shape[0]
    lane = lax.broadcasted_iota(jnp.int32, (rows, LANES), 1)

    @pl.when(kind == _ZERO)
    def _():
        o_ref[...] = jnp.zeros(o_ref.shape, BF16)

    @pl.when(kind == _ALIGNED)
    def _():
        o_ref[...] = a_ref[...].astype(BF16)

    @pl.when(kind == _SHIFTED)
    def _():
        tiles = [a_ref[:, t * LANES:(t + 1) * LANES] for t in range(PREP_TILE // LANES)] + [b_ref[...]]
        rolled = [pltpu.roll(t, LANES - PREP_SHIFT, 1) for t in tiles]
        for t in range(PREP_TILE // LANES):
            o_ref[:, t * LANES:(t + 1) * LANES] = jnp.where(lane < LANES - PREP_SHIFT, rolled[t],
                                                            rolled[t + 1]).astype(BF16)

    @pl.when(kind == _GATES)
    def _():
        a = a_ref[:, :LANES]
        for g in range(NSA_GROUPS):
            src = a if g == 0 else pltpu.roll(a, LANES - g * GATES_PER_GROUP, 1)
            o_ref[:, g * LANES:(g + 1) * LANES] = jnp.where(lane < GATES_PER_GROUP, src, 0.0).astype(BF16)


def _prep_w_in(w_in, d_model, n_total):
    depth, d, n_in = w_in.shape
    kind, main = _prep_plan(d_model, n_total)
    per_lane = PREP_TILE // LANES
    last_lane_block = n_in // LANES
    grid_spec = pltpu.PrefetchScalarGridSpec(
        num_scalar_prefetch=2,
        grid=(depth, n_total // PREP_TILE),
        in_specs=[pl.BlockSpec((None, d, PREP_TILE), lambda l, j, kind, main: (l, 0, main[j])),
                  pl.BlockSpec((None, d, LANES),
                               lambda l, j, kind, main: (l, 0, jnp.minimum((main[j] + 1) * per_lane, last_lane_block)))],
        out_specs=pl.BlockSpec((None, d, PREP_TILE), lambda l, j, kind, main: (l, 0, j)),
    )
    return pl.pallas_call(
        _prep_kernel,
        grid_spec=grid_spec,
        out_shape=jax.ShapeDtypeStruct((depth, d, n_total), BF16),
        compiler_params=_params("arbitrary", "arbitrary"),
        name="w_in_prep",
    )(kind, main, w_in, w_in)


def _rope_tables(seq):
    pos = jnp.arange(seq, dtype=F32)
    inv = ROPE_THETA ** (-jnp.arange(0, ROPE_DIM, 2, dtype=F32) / ROPE_DIM)
    ang = pos[:, None] * inv[None, :]
    cos, sin = jnp.cos(ang), jnp.sin(ang)
    ones = jnp.ones((seq, HEAD_DIM - ROPE_DIM), F32)
    zh = jnp.zeros((seq, ROPE_HALF), F32)
    zr = jnp.zeros((seq, HEAD_DIM - ROPE_DIM), F32)
    c = jnp.concatenate([cos, cos, ones], axis=1)
    s1 = jnp.concatenate([-sin, zh, zr], axis=1)
    s2 = jnp.concatenate([zh, sin, zr], axis=1)
    return c, s1, s2


def _sigmoid(x):
    return 1.0 / (1.0 + jnp.exp(-x))


def _silu(x):
    return x * _sigmoid(x)


def _dot_nt(a, b):
    return lax.dot_general(a, b, (((1,), (1,)), ((), ())), preferred_element_type=F32)


def _dot(a, b):
    return jnp.dot(a, b, preferred_element_type=F32)


def _params(*sem):
    return pltpu.CompilerParams(dimension_semantics=sem, vmem_limit_bytes=VMEM_LIMIT)


def _ada_kernel(c_ref, w_ref, b_ref, o_ref):
    c = c_ref[...]
    o_ref[0] = _dot(_silu(c).astype(BF16), w_ref[0].astype(BF16)) + b_ref[0]


def _ada(c, w_ada, b_ada):
    depth, d, n3 = w_ada.shape
    bsz = c.shape[0]
    rows = max(8, bsz)
    cp = jnp.zeros((rows, d), F32).at[:bsz].set(c)
    tn = math.gcd(1024, n3)
    out = pl.pallas_call(
        _ada_kernel,
        grid=(depth, n3 // tn),
        in_specs=[pl.BlockSpec((rows, d), lambda l, j: (0, 0)),
                  pl.BlockSpec((1, d, tn), lambda l, j: (l, 0, j)),
                  pl.BlockSpec((1, 1, tn), lambda l, j: (l, 0, j))],
        out_specs=pl.BlockSpec((1, rows, tn), lambda l, j: (l, 0, j)),
        out_shape=jax.ShapeDtypeStruct((depth, rows, n3), F32),
        compiler_params=_params("arbitrary", "arbitrary"),
        name="ada_mod",
    )(cp, w_ada, b_ada.reshape(depth, 1, n3))
    return out[:, :bsz]


def _modulated_norm(x, g, scale, shift):
    y = x * lax.rsqrt(jnp.mean(x * x, axis=-1, keepdims=True) + EPS) * g
    return (y * (1.0 + scale) + shift).astype(BF16)


def _prenorm_kernel(x_ref, g_ref, sc_ref, sh_ref, h_ref):
    h_ref[...] = _modulated_norm(x_ref[...], g_ref[...], sc_ref[0], sh_ref[0])


def _prenorm(xf, g, scale, shift, seq, tm):
    rows, d = xf.shape
    tpb = seq // tm
    mod_spec = pl.BlockSpec((1, 1, d), lambda i: (i // tpb, 0, 0))
    return pl.pallas_call(
        _prenorm_kernel,
        grid=(rows // tm,),
        in_specs=[pl.BlockSpec((tm, d), lambda i: (i, 0)), pl.BlockSpec((1, d), lambda i: (0, 0)),
                  mod_spec, mod_spec],
        out_specs=pl.BlockSpec((tm, d), lambda i: (i, 0)),
        out_shape=jax.ShapeDtypeStruct((rows, d), BF16),
        compiler_params=_params("arbitrary"),
        name="pre_norm",
    )(xf, g.reshape(1, d), scale[:, None, :], shift[:, None, :])


def _inproj_kernel(h_ref, c_ref, s1_ref, s2_ref, w_ref, o_ref, *, n_rope_tiles, tn):
    j = pl.program_id(1)
    y = _dot(h_ref[...], w_ref[...])

    @pl.when(j < n_rope_tiles)
    def _():
        c, s1, s2 = c_ref[...], s1_ref[...], s2_ref[...]
        for hh in range(tn // HEAD_DIM):
            sl = slice(hh * HEAD_DIM, (hh + 1) * HEAD_DIM)
            yh = y[:, sl]
            r = (yh * c + pltpu.roll(yh, HEAD_DIM - ROPE_HALF, 1) * s1
                 + pltpu.roll(yh, ROPE_HALF, 1) * s2)
            col = j * tn + hh * HEAD_DIM
            is_q = ((col >= P_AQ) & (col < P_AQ + BR_WIDTH)) | ((col >= P_NQ) & (col < P_NQ + BR_WIDTH))
            o_ref[:, sl] = (r * jnp.where(is_q, Q_PRESCALE, 1.0)).astype(BF16)

    @pl.when(j >= n_rope_tiles)
    def _():
        o_ref[...] = y.astype(BF16)


def _inproj(h, tables, wp_all, layer, seq, tm, tn):
    rows, d = h.shape
    n_total = wp_all.shape[2]
    tpb = seq // tm
    c, s1, s2 = tables
    tab_spec = pl.BlockSpec((tm, HEAD_DIM), lambda i, j: (i % tpb, 0))
    return pl.pallas_call(
        functools.partial(_inproj_kernel, n_rope_tiles=ROPE_COLS // tn, tn=tn),
        grid=(rows // tm, n_total // tn),
        in_specs=[pl.BlockSpec((tm, d), lambda i, j: (i, 0)),
                  tab_spec, tab_spec, tab_spec,
                  pl.BlockSpec((None, d, tn), lambda i, j: (layer, 0, j))],
        out_specs=pl.BlockSpec((tm, tn), lambda i, j: (i, j)),
        out_shape=jax.ShapeDtypeStruct((rows, n_total), BF16),
        compiler_params=_params("arbitrary", "arbitrary"),
        name="in_proj",
    )(h, c, s1, s2, wp_all)


def _softmax_update(carry, s, v):
    m, l, acc = carry
    m_new = jnp.maximum(m, jnp.max(s, axis=-1, keepdims=True))
    alpha = jnp.exp2(m - m_new)
    p = jnp.exp2(s - m_new)
    l = alpha * l + jnp.sum(p, axis=-1, keepdims=True)
    acc = alpha * acc + _dot(p.astype(BF16), v)
    return m_new, l, acc


def _softmax_init(rows, width):
    return (jnp.full((rows, 1), NEG_INF, F32), jnp.zeros((rows, 1), F32),
            jnp.zeros((rows, width), F32))


def _diff_kernel(q_ref, k_ref, v_ref, z_ref, ng_ref, lq1_ref, lk1_ref, lq2_ref, lk2_ref, o_ref,
                 *, tq, tk, lam_init):
    i = pl.program_id(2)
    t0 = i * tq
    q = q_ref[...]
    qs = (q[:, :HEAD_DIM], q[:, HEAD_DIM:])
    n_full = t0 // tk

    def step(jt, carry, masked):
        k0 = pl.multiple_of(jt * tk, tk)
        k = k_ref[pl.ds(k0, tk), :]
        v = v_ref[pl.ds(k0, tk), :]
        s = [_dot_nt(qs[cc], k[:, cc * HEAD_DIM:(cc + 1) * HEAD_DIM]) for cc in range(2)]
        if masked:
            kpos = k0 + lax.broadcasted_iota(jnp.int32, (tq, tk), 1)
            tpos = t0 + lax.broadcasted_iota(jnp.int32, (tq, tk), 0)
            s = [jnp.where(kpos <= tpos, sc, NEG_INF) for sc in s]
        return tuple(_softmax_update(carry[cc], s[cc], v) for cc in range(2))

    init = (_softmax_init(tq, 2 * HEAD_DIM), _softmax_init(tq, 2 * HEAD_DIM))
    carry = lax.fori_loop(0, n_full, lambda jt, c: step(jt, c, False), init)
    carry = step(n_full, carry, True)

    lam = (jnp.exp(jnp.sum(lq1_ref[...] * lk1_ref[...], axis=-1, keepdims=True))
           - jnp.exp(jnp.sum(lq2_ref[...] * lk2_ref[...], axis=-1, keepdims=True)) + lam_init)
    o0 = carry[0][2] / carry[0][1]
    o1 = carry[1][2] / carry[1][1]
    o = o0 - lam * o1
    o = o * lax.rsqrt(jnp.mean(o * o, axis=-1, keepdims=True) + DIFF_NORM_EPS) * ng_ref[...]
    o = o * (1.0 - lam_init)
    o_ref[...] = (o * _silu(z_ref[...].astype(F32))).astype(BF16)


def _diff_attn(proj, offs, norm_g, lq1, lk1, lq2, lk2, layer_idx, bsz, seq, tq, tk):
    nq = seq // tq
    w = 2 * HEAD_DIM
    lam_init = 0.8 - 0.6 * math.exp(-0.3 * layer_idx)
    vec = pl.BlockSpec((1, HEAD_DIM), lambda b, h, i: (0, 0))
    return pl.pallas_call(
        functools.partial(_diff_kernel, tq=tq, tk=tk, lam_init=lam_init),
        grid=(bsz, DIFF_HEADS, nq),
        in_specs=[pl.BlockSpec((tq, w), lambda b, h, i: (b * nq + i, P_AQ // w + h)),
                  pl.BlockSpec((seq, w), lambda b, h, i: (b, P_AK // w + h)),
                  pl.BlockSpec((seq, w), lambda b, h, i: (b, offs["AV"] // w + h)),
                  pl.BlockSpec((tq, w), lambda b, h, i: (b * nq + i, offs["AZ"] // w + h)),
                  pl.BlockSpec((1, w), lambda b, h, i: (0, 0)),
                  vec, vec, vec, vec],
        out_specs=pl.BlockSpec((tq, w), lambda b, h, i: (b * nq + i, h)),
        out_shape=jax.ShapeDtypeStruct((bsz * seq, BR_WIDTH), BF16),
        compiler_params=_params("arbitrary", "arbitrary", "arbitrary"),
        name="diff_attn",
    )(proj, proj, proj, proj, norm_g.reshape(1, w), lq1.reshape(1, -1), lk1.reshape(1, -1),
      lq2.reshape(1, -1), lk2.reshape(1, -1))


def _sb_kernel(q_ref, k_ref, v_ref, z_ref, o_ref, *, tq, tk):
    i = pl.program_id(2)
    t0 = i * tq
    n_full = t0 // tk
    qh = (q_ref[pl.ds(0, tk), :], q_ref[pl.ds(tk, tk), :])
    upper2 = jnp.where((lax.broadcasted_iota(jnp.int32, (2 * tk, tk), 0) & (tk - 1))
                       > lax.broadcasted_iota(jnp.int32, (2 * tk, tk), 1), 1.0, 0.0).astype(BF16)
    strict = (lax.broadcasted_iota(jnp.int32, (tk, tk), 1) < lax.broadcasted_iota(jnp.int32, (tk, tk), 0))

    def steps(items):
        kv = []
        for _, jt, _, _ in items:
            k0 = pl.multiple_of(jt * tk, tk)
            kv.append((k_ref[pl.ds(k0, tk), :], v_ref[pl.ds(k0, tk), :]))
        z = [_dot_nt(qh[it[0]], kv[n][0]) * ATTN_SCALE for n, it in enumerate(items)]
        log_1m = [-(jnp.maximum(zz, 0.0) + jnp.log(1.0 + jnp.exp(-jnp.abs(zz)))) for zz in z]
        log_1m = [jnp.where(strict, lm, 0.0) if it[3] else lm for lm, it in zip(log_1m, items)]
        stacked = []
        for lm in log_1m:
            hi = lm.astype(BF16)
            stacked.append(jnp.concatenate([hi, (lm - hi.astype(F32)).astype(BF16)], axis=1))
        after = [_dot(st, upper2) + it[2][0] for st, it in zip(stacked, items)]
        a = [jnp.exp(zz + lm + af) for zz, lm, af in zip(z, log_1m, after)]
        a = [jnp.where(strict, aa, 0.0) if it[3] else aa for aa, it in zip(a, items)]
        pv = [_dot(aa.astype(BF16), kv[n][1]) for n, aa in enumerate(a)]
        return [(it[2][0] + jnp.sum(lm, axis=-1, keepdims=True), it[2][1] + p)
                for it, lm, p in zip(items, log_1m, pv)]

    zero = (jnp.zeros((tk, 1), F32), jnp.zeros((tk, HEAD_DIM), F32))
    c_b, c_a = steps([(1, n_full + 1, zero, True), (0, n_full, zero, True)])
    (c_b,) = steps([(1, n_full, c_b, False)])

    def alive(ca, cb):
        return jnp.maximum(jnp.max(ca[0]), jnp.max(cb[0]))

    def cond(c):
        return (c[0] < n_full) & (c[1] > SB_UNDERFLOW)

    def body(c):
        jt = n_full - 1 - c[0]
        ca, cb = steps([(0, jt, c[2], False), (1, jt, c[3], False)])
        return c[0] + 1, alive(ca, cb), ca, cb

    _, _, c_a, c_b = lax.while_loop(cond, body, (jnp.int32(0), alive(c_a, c_b), c_a, c_b))
    zg = z_ref[...].astype(F32)
    o_ref[pl.ds(0, tk), :] = (c_a[1] * _silu(zg[:tk])).astype(BF16)
    o_ref[pl.ds(tk, tk), :] = (c_b[1] * _silu(zg[tk:])).astype(BF16)


def _sb_attn(proj, offs, bsz, seq, tq, tk):
    nq = seq // tq
    w = HEAD_DIM
    assert tq == 2 * tk
    return pl.pallas_call(
        functools.partial(_sb_kernel, tq=tq, tk=tk),
        grid=(bsz, SB_HEADS, nq),
        in_specs=[pl.BlockSpec((tq, w), lambda b, h, i: (b * nq + i, offs["CQ"] // w + h)),
                  pl.BlockSpec((seq, w), lambda b, h, i: (b, offs["CK"] // w + h)),
                  pl.BlockSpec((seq, w), lambda b, h, i: (b, offs["CV"] // w + h)),
                  pl.BlockSpec((tq, w), lambda b, h, i: (b * nq + i, offs["CZ"] // w + h))],
        out_specs=pl.BlockSpec((tq, w), lambda b, h, i: (b * nq + i, h)),
        out_shape=jax.ShapeDtypeStruct((bsz * seq, BR_WIDTH), BF16),
        compiler_params=_params("arbitrary", "arbitrary", "arbitrary"),
        name="stick_breaking",
    )(proj, proj, proj, proj)


def _compress_kernel(x_ref, pe_ref, w1_ref, w2_ref, o_ref, xs_ref, *, seq):
    nc = seq // CMP_STRIDE
    xs_ref[pl.ds(0, seq), :] = x_ref[...].astype(F32)
    xs_ref[pl.ds(seq, CMP_STRIDE), :] = jnp.zeros((CMP_STRIDE, HEAD_DIM), F32)
    hid = None
    for l in range(CMP_LEN):
        rows = xs_ref[pl.ds(l, nc, stride=CMP_STRIDE), :] + pe_ref[l:l + 1, :]
        t = _dot(rows.astype(BF16), w1_ref[l * HEAD_DIM:(l + 1) * HEAD_DIM, :])
        hid = t if hid is None else hid + t
    o_ref[0, 0] = _dot(_silu(hid).astype(BF16), w2_ref[...]).astype(BF16)


def _compress(proj, col_off, pe, w1, w2, bsz, seq):
    nc = seq // CMP_STRIDE
    w = HEAD_DIM
    return pl.pallas_call(
        functools.partial(_compress_kernel, seq=seq),
        grid=(bsz, NSA_GROUPS),
        in_specs=[pl.BlockSpec((seq, w), lambda b, g: (b, col_off // w + g)),
                  pl.BlockSpec((CMP_LEN, w), lambda b, g: (0, 0)),
                  pl.BlockSpec((CMP_LEN * w, w), lambda b, g: (0, 0)),
                  pl.BlockSpec((w, w), lambda b, g: (0, 0))],
        out_specs=pl.BlockSpec((1, 1, nc, w), lambda b, g: (b, g, 0, 0)),
        out_shape=jax.ShapeDtypeStruct((bsz, NSA_GROUPS, nc, w), BF16),
        scratch_shapes=[pltpu.VMEM((seq + CMP_STRIDE, w), F32)],
        compiler_params=_params("arbitrary", "arbitrary"),
        name="nsa_compress",
    )(proj, pe, w1.astype(BF16), w2.astype(BF16))


def _nsa_kernel(q_ref, kc_ref, vc_ref, ks_ref, vs_ref, kw_ref, vw_ref, g_ref, z_ref, o_ref, imp_ref,
                *, tq, tk, seq):
    i = pl.program_id(2)
    t0 = i * tq
    nc = seq // CMP_STRIDE
    nb = seq // SLC_LEN
    nbp = SLC_BLOCKS_PAD
    rep = NSA_REP
    q = q_ref[...]
    qh = [q[:, r * HEAD_DIM:(r + 1) * HEAD_DIM] for r in range(rep)]

    kc, vc = kc_ref[0, 0], vc_ref[0, 0]
    tpos_c = t0 + lax.broadcasted_iota(jnp.int32, (tq, nc), 0)
    ncol = lax.broadcasted_iota(jnp.int32, (tq, nc), 1)
    cvalid = CMP_STRIDE * ncol + (CMP_LEN - 1) <= tpos_c
    psum = jnp.zeros((tq, nc), F32)
    o_cmp = []
    for r in range(rep):
        s = jnp.where(cvalid, _dot_nt(qh[r], kc), NEG_INF)
        e = jnp.where(cvalid, jnp.exp2(s - jnp.max(s, axis=-1, keepdims=True)), 0.0)
        den = jnp.sum(e, axis=-1, keepdims=True)
        p = e / jnp.where(den > 0.0, den, 1.0)
        psum = psum + p
        o_cmp.append(_dot(p.astype(BF16), vc))

    nb8 = imp_ref.shape[0]
    jrow = lax.broadcasted_iota(jnp.int32, (nb8, nc), 0)
    ncol2 = lax.broadcasted_iota(jnp.int32, (nb8, nc), 1)
    overlap = jnp.where((CMP_STRIDE * ncol2 < SLC_LEN * jrow + SLC_LEN)
                        & (CMP_STRIDE * ncol2 + CMP_LEN > SLC_LEN * jrow)
                        & (ncol2 < nc - 1) & (jrow < nb), 1.0, 0.0).astype(BF16)
    p1 = psum.astype(BF16)
    r1 = psum - p1.astype(F32)
    p2 = r1.astype(BF16)
    p3 = (r1 - p2.astype(F32)).astype(BF16)
    imp = _dot_nt(overlap, p1) + _dot_nt(overlap, p2) + _dot_nt(overlap, p3)
    jt_ = lax.broadcasted_iota(jnp.int32, (nb8, tq), 0)
    tblk = jnp.right_shift(t0 + lax.broadcasted_iota(jnp.int32, (nb8, tq), 1), SLC_SHIFT)
    imp = jnp.where((jt_ == tblk) | (jt_ == 0), FORCE_SCORE, imp)
    imp = jnp.where(jt_ <= tblk, imp, NEG_INF)
    imp_ref[...] = imp

    def rank_body(jp, cnt):
        row = imp_ref[pl.ds(jp, 1), :]
        beats = (row > imp) | ((row == imp) & (jp < jt_))
        return cnt + jnp.where(beats, 1.0, 0.0)

    n_live = jnp.minimum((t0 + tq - 1) // SLC_LEN + 1, nb)
    cnt = lax.fori_loop(0, n_live, rank_body, jnp.zeros((nb8, tq), F32))
    keep = (cnt < float(min(SLC_TOPK, nb))) & (imp > 0.5 * NEG_INF)
    unsel_t = jnp.where(keep, 0.0, 1.0)
    if nbp > nb8:
        unsel_t = jnp.concatenate([unsel_t, jnp.zeros((nbp - nb8, tq), F32)], axis=0)
    unselected = unsel_t.T.astype(BF16)

    rows = rep * tq
    q_sel = jnp.concatenate([jnp.concatenate([qh[r], unselected], axis=1) for r in range(rep)], axis=0)
    q_win = jnp.concatenate(qh, axis=0)

    def slc_step(jt, carry, masked):
        k0 = pl.multiple_of(jt * tk, tk)
        k = ks_ref[pl.ds(k0, tk), :]
        v = vs_ref[pl.ds(k0, tk), :]
        kblk = jnp.right_shift(k0 + lax.broadcasted_iota(jnp.int32, (tk, nbp), 0), SLC_SHIFT)
        bias = jnp.where(lax.broadcasted_iota(jnp.int32, (tk, nbp), 1) == kblk,
                         -UNSELECTED_BIAS, 0.0).astype(BF16)
        s = _dot_nt(q_sel, jnp.concatenate([k, bias], axis=1))
        if masked:
            causal = (k0 + lax.broadcasted_iota(jnp.int32, (tq, tk), 1)
                      <= t0 + lax.broadcasted_iota(jnp.int32, (tq, tk), 0))
            s = s + jnp.concatenate([jnp.where(causal, 0.0, NEG_INF)] * rep, axis=0)
        return _softmax_update(carry, s, v)

    n_full = t0 // tk
    carry = lax.fori_loop(0, n_full, lambda jt, c: slc_step(jt, c, False), _softmax_init(rows, HEAD_DIM))
    m_s, l_s, acc_s = slc_step(n_full, carry, True)
    o_slc = acc_s / l_s

    span = min(WINDOW + tq, seq)
    start = pl.multiple_of(jnp.maximum(t0 - WINDOW, 0), tq)
    kpos = start + lax.broadcasted_iota(jnp.int32, (tq, span), 1)
    row_w = t0 + lax.broadcasted_iota(jnp.int32, (tq, span), 0)
    in_window = jnp.where(kpos <= row_w, jnp.where(kpos > row_w - WINDOW, 0.0, NEG_INF), NEG_INF)
    s = _dot_nt(q_win, kw_ref[pl.ds(start, span), :]) + jnp.concatenate([in_window] * rep, axis=0)
    p = jnp.exp2(s - jnp.max(s, axis=-1, keepdims=True))
    o_win = _dot(p.astype(BF16), vw_ref[pl.ds(start, span), :]) / jnp.sum(p, axis=-1, keepdims=True)

    gates = _sigmoid(g_ref[...].astype(F32))
    z = z_ref[...].astype(F32)
    for r in range(rep):
        rs = slice(r * tq, (r + 1) * tq)
        cs = slice(r * HEAD_DIM, (r + 1) * HEAD_DIM)
        o = (gates[:, 3 * r:3 * r + 1] * o_cmp[r] + gates[:, 3 * r + 1:3 * r + 2] * o_slc[rs]
             + gates[:, 3 * r + 2:3 * r + 3] * o_win[rs])
        o_ref[:, cs] = (o * _silu(z[:, cs])).astype(BF16)


def _nsa_attn(proj, offs, kcmp, vcmp, bsz, seq, tq, tk):
    nq = seq // tq
    nc = seq // CMP_STRIDE
    w = HEAD_DIM
    gw = NSA_REP * HEAD_DIM
    assert seq // SLC_LEN <= SLC_BLOCKS_PAD and tq & (tq - 1) == 0 and tk % tq == 0
    kv = lambda off: pl.BlockSpec((seq, w), lambda b, g, i: (b, off // w + g))
    cmp_spec = pl.BlockSpec((1, 1, nc, w), lambda b, g, i: (b, g, 0, 0))
    return pl.pallas_call(
        functools.partial(_nsa_kernel, tq=tq, tk=tk, seq=seq),
        grid=(bsz, NSA_GROUPS, nq),
        in_specs=[pl.BlockSpec((tq, gw), lambda b, g, i: (b * nq + i, P_NQ // gw + g)),
                  cmp_spec, cmp_spec,
                  kv(P_NKS), kv(offs["NVS"]), kv(P_NKW), kv(offs["NVW"]),
                  pl.BlockSpec((tq, LANES), lambda b, g, i: (b * nq + i, offs["NG"] // LANES + g)),
                  pl.BlockSpec((tq, gw), lambda b, g, i: (b * nq + i, offs["NZ"] // gw + g))],
        out_specs=pl.BlockSpec((tq, gw), lambda b, g, i: (b * nq + i, g)),
        out_shape=jax.ShapeDtypeStruct((bsz * seq, BR_WIDTH), BF16),
        scratch_shapes=[pltpu.VMEM((-(-(seq // SLC_LEN) // 8) * 8, tq), F32)],
        compiler_params=_params("arbitrary", "arbitrary", "arbitrary"),
        name="nsa_attn",
    )(proj, kcmp, vcmp, proj, proj, proj, proj, proj, proj)


def _merge_kernel(x_ref, ya_ref, yb_ref, yc_ref, mg0_ref, mg1_ref, mg2_ref, wb_ref, wo_ref, gp_ref,
                  gate_ref, *rest, emit_h):
    merged = None
    for n, (y_ref, mg_ref) in enumerate(((ya_ref, mg0_ref), (yb_ref, mg1_ref), (yc_ref, mg2_ref))):
        t = _dot(y_ref[...], wb_ref[n]) * _sigmoid(mg_ref[...].astype(F32))
        merged = t if merged is None else merged + t
    o = _dot(merged.astype(BF16), wo_ref[...])
    o = o * lax.rsqrt(jnp.mean(o * o, axis=-1, keepdims=True) + EPS) * gp_ref[...]
    x_new = x_ref[...] + gate_ref[0] * o
    if emit_h:
        g_next_ref, sc_next_ref, sh_next_ref, o_ref, h_ref = rest
        h_ref[...] = _modulated_norm(x_new, g_next_ref[...], sc_next_ref[0], sh_next_ref[0])
    else:
        (o_ref,) = rest
    o_ref[...] = x_new


def _merge(xf, ya, yb, yc, proj, wb_all, wo_all, layer, g_post, gate, next_norm, seq, tm):
    rows, d = xf.shape
    tpb = seq // tm
    row = lambda w_: pl.BlockSpec((tm, w_), lambda i: (i, 0))
    mg = lambda n: pl.BlockSpec((tm, d), lambda i: (i, P_MG // d + n))
    vec = pl.BlockSpec((1, d), lambda i: (0, 0))
    mod_spec = pl.BlockSpec((1, 1, d), lambda i: (i // tpb, 0, 0))
    const = pl.Buffered(1)
    emit_h = next_norm is not None
    in_specs = [row(d), row(BR_WIDTH), row(BR_WIDTH), row(BR_WIDTH), mg(0), mg(1), mg(2),
                pl.BlockSpec((None, N_BRANCH, BR_WIDTH, d), lambda i: (layer, 0, 0, 0), pipeline_mode=const),
                pl.BlockSpec((None, d, d), lambda i: (layer, 0, 0), pipeline_mode=const), vec, mod_spec]
    args = [xf, ya, yb, yc, proj, proj, proj, wb_all, wo_all, g_post.reshape(1, d),
            gate[:, None, :]]
    out_specs, out_shape = row(d), jax.ShapeDtypeStruct((rows, d), F32)
    if emit_h:
        g_next, sc_next, sh_next = next_norm
        in_specs += [vec, mod_spec, mod_spec]
        args += [g_next.reshape(1, d), sc_next[:, None, :], sh_next[:, None, :]]
        out_specs, out_shape = [out_specs, row(d)], [out_shape, jax.ShapeDtypeStruct((rows, d), BF16)]
    out = pl.pallas_call(
        functools.partial(_merge_kernel, emit_h=emit_h),
        grid=(rows // tm,),
        in_specs=in_specs,
        out_specs=out_specs,
        out_shape=out_shape,
        compiler_params=_params("arbitrary"),
        name="merge_out",
    )(*args)
    return (out[0], out[1]) if emit_h else (out, None)


def _tiles(seq, d_model):
    return dict(tm_in=min(1024, seq), tn_in=2048, tm_merge=min(256, seq),
                diff=(min(512, seq), min(512, seq)),
                sb=(min(512, seq), min(512, seq) // 2),
                nsa=(min(256, seq), min(512, seq)))


def kernel(x, c, norm_pre_g, norm_post_g, w_ada, b_ada, w_in, lambda_q1, lambda_k1, lambda_q2,
           lambda_k2, diff_norm_g, cmp_pe_k, cmp_w1_k, cmp_w2_k, cmp_pe_v, cmp_w1_v, cmp_w2_v,
           w_branch, w_out):
    bsz, seq, d = x.shape
    depth = w_in.shape[0]
    t = _tiles(seq, d)
    offs = _rest_offsets(d)
    assert d % 512 == 0 and ROPE_COLS % d == 0 and seq % t["tm_in"] == 0 and seq % 128 == 0
    n_total = -(-offs["END"] // t["tn_in"]) * t["tn_in"]
    tables = _rope_tables(seq)
    mod = _ada(c, w_ada, b_ada)
    xf = x.reshape(bsz * seq, d)
    mods = [jnp.split(mod[l], 3, axis=-1) for l in range(depth)]
    h = _prenorm(xf, norm_pre_g[0], mods[0][1], mods[0][0], seq, t["tm_merge"])
    wb_all, wo_all = w_branch.astype(BF16), w_out.astype(BF16)
    wp_all = _prep_w_in(w_in, d, n_total)
    for l in range(depth):
        proj = _inproj(h, tables, wp_all, l, seq, t["tm_in"], t["tn_in"])
        ya = _diff_attn(proj, offs, diff_norm_g[l], lambda_q1[l], lambda_k1[l], lambda_q2[l],
                        lambda_k2[l], l, bsz, seq, *t["diff"])
        kcmp = _compress(proj, P_NKC, cmp_pe_k[l], cmp_w1_k[l], cmp_w2_k[l], bsz, seq)
        vcmp = _compress(proj, offs["NVC"], cmp_pe_v[l], cmp_w1_v[l], cmp_w2_v[l], bsz, seq)
        yb = _nsa_attn(proj, offs, kcmp, vcmp, bsz, seq, *t["nsa"])
        yc = _sb_attn(proj, offs, bsz, seq, *t["sb"])
        next_norm = (norm_pre_g[l + 1], mods[l + 1][1], mods[l + 1][0]) if l + 1 < depth else None
        xf, h = _merge(xf, ya, yb, yc, proj, wb_all, wo_all, l, norm_post_g[l], mods[l][2],
                       next_norm, seq, t["tm_merge"])
    return xf.reshape(bsz, seq, d)
```

```python
import functools
import math

import jax
import jax.numpy as jnp
from jax import lax
from jax.experimental import pallas as pl
from jax.experimental.pallas import tpu as pltpu

F32 = jnp.float32
BF16 = jnp.bfloat16

HEAD_DIM = 128
BR_WIDTH = 1024
N_BRANCH = 3
ROPE_THETA = 500000.0
ROPE_DIM = HEAD_DIM // 4
ROPE_HALF = ROPE_DIM // 2
EPS = 1e-6
NEG_INF = -1e30
FORCE_SCORE = 1e6
DIFF_HEADS = BR_WIDTH // (2 * HEAD_DIM)
DIFF_NORM_EPS = 1e-5
NSA_HEADS = BR_WIDTH // HEAD_DIM
NSA_GROUPS = 2
NSA_REP = NSA_HEADS // NSA_GROUPS
CMP_LEN = 32
CMP_STRIDE = 16
SLC_LEN = 64
SLC_SHIFT = SLC_LEN.bit_length() - 1
SLC_TOPK = 16
WINDOW = 512
SB_HEADS = BR_WIDTH // HEAD_DIM
NSA_KV = NSA_GROUPS * HEAD_DIM
ATTN_SCALE = HEAD_DIM ** -0.5
Q_PRESCALE = ATTN_SCALE * math.log2(math.e)

LANES = 128
SLC_BLOCKS_PAD = 128
UNSELECTED_BIAS = 2.0 ** 30
SB_UNDERFLOW = -104.0
VMEM_LIMIT = 56 * 1024 * 1024

_O_AQ, _O_AK, _O_AV, _O_AZ = 0, 1024, 2048, 3072
_O_NQ, _O_NKC, _O_NVC, _O_NKS, _O_NVS, _O_NKW, _O_NVW = 4096, 5120, 5376, 5632, 5888, 6144, 6400
_O_NG, _O_NZ = 6656, 6680
_O_CQ, _O_CK, _O_CV, _O_CZ, _O_MG = 7704, 8728, 9752, 10776, 11800
N_GATES = 3 * NSA_HEADS
P_AQ, P_AK, P_NQ, P_NKC, P_NKS, P_NKW = 0, 1024, 2048, 3072, 3328, 3584
ROPE_COLS = 4096
P_MG = ROPE_COLS


def _rest_offsets(d_model):
    r0 = ROPE_COLS + N_BRANCH * d_model
    offs = dict(AV=r0, AZ=r0 + 1024, NZ=r0 + 2048, CQ=r0 + 3072, CK=r0 + 4096, CV=r0 + 5120,
                CZ=r0 + 6144, NVC=r0 + 7168, NVS=r0 + 7424, NVW=r0 + 7680, NG=r0 + 7936)
    offs["END"] = r0 + 7936 + NSA_GROUPS * LANES
    return offs


def _permute_w_in_t(w_in, d_model, n_total):
    wt = jnp.swapaxes(w_in, 1, 2)
    depth = wt.shape[0]
    z = lambda n: jnp.zeros((depth, n, d_model), wt.dtype)
    gates_per_group = N_GATES // NSA_GROUPS
    segs = [wt[:, _O_AQ:_O_AQ + 1024], wt[:, _O_AK:_O_AK + 1024], wt[:, _O_NQ:_O_NQ + 1024],
            wt[:, _O_NKC:_O_NKC + 256], wt[:, _O_NKS:_O_NKS + 256], wt[:, _O_NKW:_O_NKW + 256],
            z(ROPE_COLS - 3840),
            wt[:, _O_MG:_O_MG + N_BRANCH * d_model],
            wt[:, _O_AV:_O_AV + 1024], wt[:, _O_AZ:_O_AZ + 1024], wt[:, _O_NZ:_O_NZ + 1024],
            wt[:, _O_CQ:_O_CQ + 4096],
            wt[:, _O_NVC:_O_NVC + 256], wt[:, _O_NVS:_O_NVS + 256], wt[:, _O_NVW:_O_NVW + 256]]
    for g in range(NSA_GROUPS):
        segs += [wt[:, _O_NG + g * gates_per_group:_O_NG + (g + 1) * gates_per_group],
                 z(LANES - gates_per_group)]
    used = _rest_offsets(d_model)["END"]
    if n_total > used:
        segs.append(z(n_total - used))
    return jnp.concatenate(segs, axis=1).astype(BF16)


def _rope_tables(seq):
    pos = jnp.arange(seq, dtype=F32)
    inv = ROPE_THETA ** (-jnp.arange(0, ROPE_DIM, 2, dtype=F32) / ROPE_DIM)
    ang = pos[:, None] * inv[None, :]
    cos, sin = jnp.cos(ang), jnp.sin(ang)
    ones = jnp.ones((seq, HEAD_DIM - ROPE_DIM), F32)
    zh = jnp.zeros((seq, ROPE_HALF), F32)
    zr = jnp.zeros((seq, HEAD_DIM - ROPE_DIM), F32)
    c = jnp.concatenate([cos, cos, ones], axis=1)
    s1 = jnp.concatenate([-sin, zh, zr], axis=1)
    s2 = jnp.concatenate([zh, sin, zr], axis=1)
    return c, s1, s2


def _sigmoid(x):
    return 1.0 / (1.0 + jnp.exp(-x))


def _silu(x):
    return x * _sigmoid(x)


def _dot_nt(a, b):
    return lax.dot_general(a, b, (((1,), (1,)), ((), ())), preferred_element_type=F32)


def _dot(a, b):
    return jnp.dot(a, b, preferred_element_type=F32)


def _params(*sem):
    return pltpu.CompilerParams(dimension_semantics=sem, vmem_limit_bytes=VMEM_LIMIT)


def _ada_kernel(c_ref, w_ref, b_ref, o_ref):
    c = c_ref[...]
    o_ref[0] = _dot(_silu(c).astype(BF16), w_ref[0].astype(BF16)) + b_ref[0]


def _ada(c, w_ada, b_ada):
    depth, d, n3 = w_ada.shape
    bsz = c.shape[0]
    rows = max(8, bsz)
    cp = jnp.zeros((rows, d), F32).at[:bsz].set(c)
    tn = math.gcd(1024, n3)
    out = pl.pallas_call(
        _ada_kernel,
        grid=(depth, n3 // tn),
        in_specs=[pl.BlockSpec((rows, d), lambda l, j: (0, 0)),
                  pl.BlockSpec((1, d, tn), lambda l, j: (l, 0, j)),
                  pl.BlockSpec((1, 1, tn), lambda l, j: (l, 0, j))],
        out_specs=pl.BlockSpec((1, rows, tn), lambda l, j: (l, 0, j)),
        out_shape=jax.ShapeDtypeStruct((depth, rows, n3), F32),
        compiler_params=_params("arbitrary", "arbitrary"),
        name="ada_mod",
    )(cp, w_ada, b_ada.reshape(depth, 1, n3))
    return out[:, :bsz]


def _modulated_norm(x, g, scale, shift):
    y = x * lax.rsqrt(jnp.mean(x * x, axis=-1, keepdims=True) + EPS) * g
    return (y * (1.0 + scale) + shift).astype(BF16)


def _prenorm_kernel(x_ref, g_ref, sc_ref, sh_ref, h_ref):
    h_ref[...] = _modulated_norm(x_ref[...], g_ref[...], sc_ref[0], sh_ref[0])


def _prenorm(xf, g, scale, shift, seq, tm):
    rows, d = xf.shape
    tpb = seq // tm
    mod_spec = pl.BlockSpec((1, 1, d), lambda i: (i // tpb, 0, 0))
    return pl.pallas_call(
        _prenorm_kernel,
        grid=(rows // tm,),
        in_specs=[pl.BlockSpec((tm, d), lambda i: (i, 0)), pl.BlockSpec((1, d), lambda i: (0, 0)),
                  mod_spec, mod_spec],
        out_specs=pl.BlockSpec((tm, d), lambda i: (i, 0)),
        out_shape=jax.ShapeDtypeStruct((rows, d), BF16),
        compiler_params=_params("arbitrary"),
        name="pre_norm",
    )(xf, g.reshape(1, d), scale[:, None, :], shift[:, None, :])


def _inproj_kernel(h_ref, c_ref, s1_ref, s2_ref, w_ref, o_ref, *, n_rope_tiles, tn):
    j = pl.program_id(1)
    y = _dot_nt(h_ref[...], w_ref[...])

    @pl.when(j < n_rope_tiles)
    def _():
        c, s1, s2 = c_ref[...], s1_ref[...], s2_ref[...]
        for hh in range(tn // HEAD_DIM):
            sl = slice(hh * HEAD_DIM, (hh + 1) * HEAD_DIM)
            yh = y[:, sl]
            r = (yh * c + pltpu.roll(yh, HEAD_DIM - ROPE_HALF, 1) * s1
                 + pltpu.roll(yh, ROPE_HALF, 1) * s2)
            col = j * tn + hh * HEAD_DIM
            is_q = ((col >= P_AQ) & (col < P_AQ + BR_WIDTH)) | ((col >= P_NQ) & (col < P_NQ + BR_WIDTH))
            o_ref[:, sl] = (r * jnp.where(is_q, Q_PRESCALE, 1.0)).astype(BF16)

    @pl.when(j >= n_rope_tiles)
    def _():
        o_ref[...] = y.astype(BF16)


def _inproj(h, tables, wp_all, layer, seq, tm, tn):
    rows, d = h.shape
    n_total = wp_all.shape[1]
    tpb = seq // tm
    c, s1, s2 = tables
    tab_spec = pl.BlockSpec((tm, HEAD_DIM), lambda i, j: (i % tpb, 0))
    return pl.pallas_call(
        functools.partial(_inproj_kernel, n_rope_tiles=ROPE_COLS // tn, tn=tn),
        grid=(rows // tm, n_total // tn),
        in_specs=[pl.BlockSpec((tm, d), lambda i, j: (i, 0)),
                  tab_spec, tab_spec, tab_spec,
                  pl.BlockSpec((None, tn, d), lambda i, j: (layer, j, 0))],
        out_specs=pl.BlockSpec((tm, tn), lambda i, j: (i, j)),
        out_shape=jax.ShapeDtypeStruct((rows, n_total), BF16),
        compiler_params=_params("arbitrary", "arbitrary"),
        name="in_proj",
    )(h, c, s1, s2, wp_all)


def _softmax_update(carry, s, v):
    m, l, acc = carry
    m_new = jnp.maximum(m, jnp.max(s, axis=-1, keepdims=True))
    alpha = jnp.exp2(m - m_new)
    p = jnp.exp2(s - m_new)
    l = alpha * l + jnp.sum(p, axis=-1, keepdims=True)
    acc = alpha * acc + _dot(p.astype(BF16), v)
    return m_new, l, acc


def _softmax_init(rows, width):
    return (jnp.full((rows, 1), NEG_INF, F32), jnp.zeros((rows, 1), F32),
            jnp.zeros((rows, width), F32))


def _diff_kernel(q_ref, k_ref, v_ref, z_ref, ng_ref, lq1_ref, lk1_ref, lq2_ref, lk2_ref, o_ref,
                 *, tq, tk, lam_init):
    i = pl.program_id(2)
    t0 = i * tq
    q = q_ref[...]
    qs = (q[:, :HEAD_DIM], q[:, HEAD_DIM:])
    n_full = t0 // tk

    def step(jt, carry, masked):
        k0 = pl.multiple_of(jt * tk, tk)
        k = k_ref[pl.ds(k0, tk), :]
        v = v_ref[pl.ds(k0, tk), :]
        s = [_dot_nt(qs[cc], k[:, cc * HEAD_DIM:(cc + 1) * HEAD_DIM]) for cc in range(2)]
        if masked:
            kpos = k0 + lax.broadcasted_iota(jnp.int32, (tq, tk), 1)
            tpos = t0 + lax.broadcasted_iota(jnp.int32, (tq, tk), 0)
            s = [jnp.where(kpos <= tpos, sc, NEG_INF) for sc in s]
        return tuple(_softmax_update(carry[cc], s[cc], v) for cc in range(2))

    init = (_softmax_init(tq, 2 * HEAD_DIM), _softmax_init(tq, 2 * HEAD_DIM))
    carry = lax.fori_loop(0, n_full, lambda jt, c: step(jt, c, False), init)
    carry = step(n_full, carry, True)

    lam = (jnp.exp(jnp.sum(lq1_ref[...] * lk1_ref[...], axis=-1, keepdims=True))
           - jnp.exp(jnp.sum(lq2_ref[...] * lk2_ref[...], axis=-1, keepdims=True)) + lam_init)
    o0 = carry[0][2] / carry[0][1]
    o1 = carry[1][2] / carry[1][1]
    o = o0 - lam * o1
    o = o * lax.rsqrt(jnp.mean(o * o, axis=-1, keepdims=True) + DIFF_NORM_EPS) * ng_ref[...]
    o = o * (1.0 - lam_init)
    o_ref[...] = (o * _silu(z_ref[...].astype(F32))).astype(BF16)


def _diff_attn(proj, offs, norm_g, lq1, lk1, lq2, lk2, layer_idx, bsz, seq, tq, tk):
    nq = seq // tq
    w = 2 * HEAD_DIM
    lam_init = 0.8 - 0.6 * math.exp(-0.3 * layer_idx)
    vec = pl.BlockSpec((1, HEAD_DIM), lambda b, h, i: (0, 0))
    return pl.pallas_call(
        functools.partial(_diff_kernel, tq=tq, tk=tk, lam_init=lam_init),
        grid=(bsz, DIFF_HEADS, nq),
        in_specs=[pl.BlockSpec((tq, w), lambda b, h, i: (b * nq + i, P_AQ // w + h)),
                  pl.BlockSpec((seq, w), lambda b, h, i: (b, P_AK // w + h)),
                  pl.BlockSpec((seq, w), lambda b, h, i: (b, offs["AV"] // w + h)),
                  pl.BlockSpec((tq, w), lambda b, h, i: (b * nq + i, offs["AZ"] // w + h)),
                  pl.BlockSpec((1, w), lambda b, h, i: (0, 0)),
                  vec, vec, vec, vec],
        out_specs=pl.BlockSpec((tq, w), lambda b, h, i: (b * nq + i, h)),
        out_shape=jax.ShapeDtypeStruct((bsz * seq, BR_WIDTH), BF16),
        compiler_params=_params("arbitrary", "arbitrary", "arbitrary"),
        name="diff_attn",
    )(proj, proj, proj, proj, norm_g.reshape(1, w), lq1.reshape(1, -1), lk1.reshape(1, -1),
      lq2.reshape(1, -1), lk2.reshape(1, -1))


def _sb_kernel(q_ref, k_ref, v_ref, z_ref, o_ref, *, tq, tk):
    i = pl.program_id(2)
    t0 = i * tq
    n_full = t0 // tk
    qh = (q_ref[pl.ds(0, tk), :], q_ref[pl.ds(tk, tk), :])
    upper2 = jnp.where((lax.broadcasted_iota(jnp.int32, (2 * tk, tk), 0) & (tk - 1))
                       > lax.broadcasted_iota(jnp.int32, (2 * tk, tk), 1), 1.0, 0.0).astype(BF16)
    strict = (lax.broadcasted_iota(jnp.int32, (tk, tk), 1) < lax.broadcasted_iota(jnp.int32, (tk, tk), 0))

    def steps(items):
        kv = []
        for _, jt, _, _ in items:
            k0 = pl.multiple_of(jt * tk, tk)
            kv.append((k_ref[pl.ds(k0, tk), :], v_ref[pl.ds(k0, tk), :]))
        z = [_dot_nt(qh[it[0]], kv[n][0]) * ATTN_SCALE for n, it in enumerate(items)]
        log_1m = [-(jnp.maximum(zz, 0.0) + jnp.log(1.0 + jnp.exp(-jnp.abs(zz)))) for zz in z]
        log_1m = [jnp.where(strict, lm, 0.0) if it[3] else lm for lm, it in zip(log_1m, items)]
        stacked = []
        for lm in log_1m:
            hi = lm.astype(BF16)
            stacked.append(jnp.concatenate([hi, (lm - hi.astype(F32)).astype(BF16)], axis=1))
        after = [_dot(st, upper2) + it[2][0] for st, it in zip(stacked, items)]
        a = [jnp.exp(zz + lm + af) for zz, lm, af in zip(z, log_1m, after)]
        a = [jnp.where(strict, aa, 0.0) if it[3] else aa for aa, it in zip(a, items)]
        pv = [_dot(aa.astype(BF16), kv[n][1]) for n, aa in enumerate(a)]
        return [(it[2][0] + jnp.sum(lm, axis=-1, keepdims=True), it[2][1] + p)
                for it, lm, p in zip(items, log_1m, pv)]

    zero = (jnp.zeros((tk, 1), F32), jnp.zeros((tk, HEAD_DIM), F32))
    c_b, c_a = steps([(1, n_full + 1, zero, True), (0, n_full, zero, True)])
    (c_b,) = steps([(1, n_full, c_b, False)])

    def alive(ca, cb):
        return jnp.maximum(jnp.max(ca[0]), jnp.max(cb[0]))

    def cond(c):
        return (c[0] < n_full) & (c[1] > SB_UNDERFLOW)

    def body(c):
        jt = n_full - 1 - c[0]
        ca, cb = steps([(0, jt, c[2], False), (1, jt, c[3], False)])
        return c[0] + 1, alive(ca, cb), ca, cb

    _, _, c_a, c_b = lax.while_loop(cond, body, (jnp.int32(0), alive(c_a, c_b), c_a, c_b))
    zg = z_ref[...].astype(F32)
    o_ref[pl.ds(0, tk), :] = (c_a[1] * _silu(zg[:tk])).astype(BF16)
    o_ref[pl.ds(tk, tk), :] = (c_b[1] * _silu(zg[tk:])).astype(BF16)


def _sb_attn(proj, offs, bsz, seq, tq, tk):
    nq = seq // tq
    w = HEAD_DIM
    assert tq == 2 * tk
    return pl.pallas_call(
        functools.partial(_sb_kernel, tq=tq, tk=tk),
        grid=(bsz, SB_HEADS, nq),
        in_specs=[pl.BlockSpec((tq, w), lambda b, h, i: (b * nq + i, offs["CQ"] // w + h)),
                  pl.BlockSpec((seq, w), lambda b, h, i: (b, offs["CK"] // w + h)),
                  pl.BlockSpec((seq, w), lambda b, h, i: (b, offs["CV"] // w + h)),
                  pl.BlockSpec((tq, w), lambda b, h, i: (b * nq + i, offs["CZ"] // w + h))],
        out_specs=pl.BlockSpec((tq, w), lambda b, h, i: (b * nq + i, h)),
        out_shape=jax.ShapeDtypeStruct((bsz * seq, BR_WIDTH), BF16),
        compiler_params=_params("arbitrary", "arbitrary", "arbitrary"),
        name="stick_breaking",
    )(proj, proj, proj, proj)


def _compress_kernel(x_ref, pe_ref, w1_ref, w2_ref, o_ref, xs_ref, *, seq):
    nc = seq // CMP_STRIDE
    xs_ref[pl.ds(0, seq), :] = x_ref[...].astype(F32)
    xs_ref[pl.ds(seq, CMP_STRIDE), :] = jnp.zeros((CMP_STRIDE, HEAD_DIM), F32)
    hid = None
    for l in range(CMP_LEN):
        rows = xs_ref[pl.ds(l, nc, stride=CMP_STRIDE), :] + pe_ref[l:l + 1, :]
        t = _dot(rows.astype(BF16), w1_ref[l * HEAD_DIM:(l + 1) * HEAD_DIM, :])
        hid = t if hid is None else hid + t
    o_ref[0, 0] = _dot(_silu(hid).astype(BF16), w2_ref[...]).astype(BF16)


def _compress(proj, col_off, pe, w1, w2, bsz, seq):
    nc = seq // CMP_STRIDE
    w = HEAD_DIM
    return pl.pallas_call(
        functools.partial(_compress_kernel, seq=seq),
        grid=(bsz, NSA_GROUPS),
        in_specs=[pl.BlockSpec((seq, w), lambda b, g: (b, col_off // w + g)),
                  pl.BlockSpec((CMP_LEN, w), lambda b, g: (0, 0)),
                  pl.BlockSpec((CMP_LEN * w, w), lambda b, g: (0, 0)),
                  pl.BlockSpec((w, w), lambda b, g: (0, 0))],
        out_specs=pl.BlockSpec((1, 1, nc, w), lambda b, g: (b, g, 0, 0)),
        out_shape=jax.ShapeDtypeStruct((bsz, NSA_GROUPS, nc, w), BF16),
        scratch_shapes=[pltpu.VMEM((seq + CMP_STRIDE, w), F32)],
        compiler_params=_params("arbitrary", "arbitrary"),
        name="nsa_compress",
    )(proj, pe, w1.astype(BF16), w2.astype(BF16))


def _nsa_kernel(q_ref, kc_ref, vc_ref, ks_ref, vs_ref, kw_ref, vw_ref, g_ref, z_ref, o_ref, imp_ref,
                *, tq, tk, seq):
    i = pl.program_id(2)
    t0 = i * tq
    nc = seq // CMP_STRIDE
    nb = seq // SLC_LEN
    nbp = SLC_BLOCKS_PAD
    rep = NSA_REP
    q = q_ref[...]
    qh = [q[:, r * HEAD_DIM:(r + 1) * HEAD_DIM] for r in range(rep)]

    kc, vc = kc_ref[0, 0], vc_ref[0, 0]
    tpos_c = t0 + lax.broadcasted_iota(jnp.int32, (tq, nc), 0)
    ncol = lax.broadcasted_iota(jnp.int32, (tq, nc), 1)
    cvalid = CMP_STRIDE * ncol + (CMP_LEN - 1) <= tpos_c
    psum = jnp.zeros((tq, nc), F32)
    o_cmp = []
    for r in range(rep):
        s = jnp.where(cvalid, _dot_nt(qh[r], kc), NEG_INF)
        e = jnp.where(cvalid, jnp.exp2(s - jnp.max(s, axis=-1, keepdims=True)), 0.0)
        den = jnp.sum(e, axis=-1, keepdims=True)
        p = e / jnp.where(den > 0.0, den, 1.0)
        psum = psum + p
        o_cmp.append(_dot(p.astype(BF16), vc))

    nb8 = imp_ref.shape[0]
    jrow = lax.broadcasted_iota(jnp.int32, (nb8, nc), 0)
    ncol2 = lax.broadcasted_iota(jnp.int32, (nb8, nc), 1)
    overlap = jnp.where((CMP_STRIDE * ncol2 < SLC_LEN * jrow + SLC_LEN)
                        & (CMP_STRIDE * ncol2 + CMP_LEN > SLC_LEN * jrow)
                        & (ncol2 < nc - 1) & (jrow < nb), 1.0, 0.0).astype(BF16)
    p1 = psum.astype(BF16)
    r1 = psum - p1.astype(F32)
    p2 = r1.astype(BF16)
    p3 = (r1 - p2.astype(F32)).astype(BF16)
    imp = _dot_nt(overlap, p1) + _dot_nt(overlap, p2) + _dot_nt(overlap, p3)
    jt_ = lax.broadcasted_iota(jnp.int32, (nb8, tq), 0)
    tblk = jnp.right_shift(t0 + lax.broadcasted_iota(jnp.int32, (nb8, tq), 1), SLC_SHIFT)
    imp = jnp.where((jt_ == tblk) | (jt_ == 0), FORCE_SCORE, imp)
    imp = jnp.where(jt_ <= tblk, imp, NEG_INF)
    imp_ref[...] = imp

    def rank_body(jp, cnt):
        row = imp_ref[pl.ds(jp, 1), :]
        beats = (row > imp) | ((row == imp) & (jp < jt_))
        return cnt + jnp.where(beats, 1.0, 0.0)

    n_live = jnp.minimum((t0 + tq - 1) // SLC_LEN + 1, nb)
    cnt = lax.fori_loop(0, n_live, rank_body, jnp.zeros((nb8, tq), F32))
    keep = (cnt < float(min(SLC_TOPK, nb))) & (imp > 0.5 * NEG_INF)
    unsel_t = jnp.where(keep, 0.0, 1.0)
    if nbp > nb8:
        unsel_t = jnp.concatenate([unsel_t, jnp.zeros((nbp - nb8, tq), F32)], axis=0)
    unselected = unsel_t.T.astype(BF16)

    rows = rep * tq
    q_sel = jnp.concatenate([jnp.concatenate([qh[r], unselected], axis=1) for r in range(rep)], axis=0)
    q_win = jnp.concatenate(qh, axis=0)

    def slc_step(jt, carry, masked):
        k0 = pl.multiple_of(jt * tk, tk)
        k = ks_ref[pl.ds(k0, tk), :]
        v = vs_ref[pl.ds(k0, tk), :]
        kblk = jnp.right_shift(k0 + lax.broadcasted_iota(jnp.int32, (tk, nbp), 0), SLC_SHIFT)
        bias = jnp.where(lax.broadcasted_iota(jnp.int32, (tk, nbp), 1) == kblk,
                         -UNSELECTED_BIAS, 0.0).astype(BF16)
        s = _dot_nt(q_sel, jnp.concatenate([k, bias], axis=1))
        if masked:
            causal = (k0 + lax.broadcasted_iota(jnp.int32, (tq, tk), 1)
                      <= t0 + lax.broadcasted_iota(jnp.int32, (tq, tk), 0))
            s = s + jnp.concatenate([jnp.where(causal, 0.0, NEG_INF)] * rep, axis=0)
        return _softmax_update(carry, s, v)

    n_full = t0 // tk
    carry = lax.fori_loop(0, n_full, lambda jt, c: slc_step(jt, c, False), _softmax_init(rows, HEAD_DIM))
    m_s, l_s, acc_s = slc_step(n_full, carry, True)
    o_slc = acc_s / l_s

    span = min(WINDOW + tq, seq)
    start = pl.multiple_of(jnp.maximum(t0 - WINDOW, 0), tq)
    kpos = start + lax.broadcasted_iota(jnp.int32, (tq, span), 1)
    row_w = t0 + lax.broadcasted_iota(jnp.int32, (tq, span), 0)
    in_window = jnp.where(kpos <= row_w, jnp.where(kpos > row_w - WINDOW, 0.0, NEG_INF), NEG_INF)
    s = _dot_nt(q_win, kw_ref[pl.ds(start, span), :]) + jnp.concatenate([in_window] * rep, axis=0)
    p = jnp.exp2(s - jnp.max(s, axis=-1, keepdims=True))
    o_win = _dot(p.astype(BF16), vw_ref[pl.ds(start, span), :]) / jnp.sum(p, axis=-1, keepdims=True)

    gates = _sigmoid(g_ref[...].astype(F32))
    z = z_ref[...].astype(F32)
    for r in range(rep):
        rs = slice(r * tq, (r + 1) * tq)
        cs = slice(r * HEAD_DIM, (r + 1) * HEAD_DIM)
        o = (gates[:, 3 * r:3 * r + 1] * o_cmp[r] + gates[:, 3 * r + 1:3 * r + 2] * o_slc[rs]
             + gates[:, 3 * r + 2:3 * r + 3] * o_win[rs])
        o_ref[:, cs] = (o * _silu(z[:, cs])).astype(BF16)


def _nsa_attn(proj, offs, kcmp, vcmp, bsz, seq, tq, tk):
    nq = seq // tq
    nc = seq // CMP_STRIDE
    w = HEAD_DIM
    gw = NSA_REP * HEAD_DIM
    assert seq // SLC_LEN <= SLC_BLOCKS_PAD and tq & (tq - 1) == 0 and tk % tq == 0
    kv = lambda off: pl.BlockSpec((seq, w), lambda b, g, i: (b, off // w + g))
    cmp_spec = pl.BlockSpec((1, 1, nc, w), lambda b, g, i: (b, g, 0, 0))
    return pl.pallas_call(
        functools.partial(_nsa_kernel, tq=tq, tk=tk, seq=seq),
        grid=(bsz, NSA_GROUPS, nq),
        in_specs=[pl.BlockSpec((tq, gw), lambda b, g, i: (b * nq + i, P_NQ // gw + g)),
                  cmp_spec, cmp_spec,
                  kv(P_NKS), kv(offs["NVS"]), kv(P_NKW), kv(offs["NVW"]),
                  pl.BlockSpec((tq, LANES), lambda b, g, i: (b * nq + i, offs["NG"] // LANES + g)),
                  pl.BlockSpec((tq, gw), lambda b, g, i: (b * nq + i, offs["NZ"] // gw + g))],
        out_specs=pl.BlockSpec((tq, gw), lambda b, g, i: (b * nq + i, g)),
        out_shape=jax.ShapeDtypeStruct((bsz * seq, BR_WIDTH), BF16),
        scratch_shapes=[pltpu.VMEM((-(-(seq // SLC_LEN) // 8) * 8, tq), F32)],
        compiler_params=_params("arbitrary", "arbitrary", "arbitrary"),
        name="nsa_attn",
    )(proj, kcmp, vcmp, proj, proj, proj, proj, proj, proj)


def _merge_kernel(x_ref, ya_ref, yb_ref, yc_ref, mg0_ref, mg1_ref, mg2_ref, wb_ref, wo_ref, gp_ref,
                  gate_ref, *rest, emit_h):
    merged = None
    for n, (y_ref, mg_ref) in enumerate(((ya_ref, mg0_ref), (yb_ref, mg1_ref), (yc_ref, mg2_ref))):
        t = _dot(y_ref[...], wb_ref[n]) * _sigmoid(mg_ref[...].astype(F32))
        merged = t if merged is None else merged + t
    o = _dot(merged.astype(BF16), wo_ref[...])
    o = o * lax.rsqrt(jnp.mean(o * o, axis=-1, keepdims=True) + EPS) * gp_ref[...]
    x_new = x_ref[...] + gate_ref[0] * o
    if emit_h:
        g_next_ref, sc_next_ref, sh_next_ref, o_ref, h_ref = rest
        h_ref[...] = _modulated_norm(x_new, g_next_ref[...], sc_next_ref[0], sh_next_ref[0])
    else:
        (o_ref,) = rest
    o_ref[...] = x_new


def _merge(xf, ya, yb, yc, proj, wb_all, wo_all, layer, g_post, gate, next_norm, seq, tm):
    rows, d = xf.shape
    tpb = seq // tm
    row = lambda w_: pl.BlockSpec((tm, w_), lambda i: (i, 0))
    mg = lambda n: pl.BlockSpec((tm, d), lambda i: (i, P_MG // d + n))
    vec = pl.BlockSpec((1, d), lambda i: (0, 0))
    mod_spec = pl.BlockSpec((1, 1, d), lambda i: (i // tpb, 0, 0))
    const = pl.Buffered(1)
    emit_h = next_norm is not None
    in_specs = [row(d), row(BR_WIDTH), row(BR_WIDTH), row(BR_WIDTH), mg(0), mg(1), mg(2),
                pl.BlockSpec((None, N_BRANCH, BR_WIDTH, d), lambda i: (layer, 0, 0, 0), pipeline_mode=const),
                pl.BlockSpec((None, d, d), lambda i: (layer, 0, 0), pipeline_mode=const), vec, mod_spec]
    args = [xf, ya, yb, yc, proj, proj, proj, wb_all, wo_all, g_post.reshape(1, d),
            gate[:, None, :]]
    out_specs, out_shape = row(d), jax.ShapeDtypeStruct((rows, d), F32)
    if emit_h:
        g_next, sc_next, sh_next = next_norm
        in_specs += [vec, mod_spec, mod_spec]
        args += [g_next.reshape(1, d), sc_next[:, None, :], sh_next[:, None, :]]
        out_specs, out_shape = [out_specs, row(d)], [out_shape, jax.ShapeDtypeStruct((rows, d), BF16)]
    out = pl.pallas_call(
        functools.partial(_merge_kernel, emit_h=emit_h),
        grid=(rows // tm,),
        in_specs=in_specs,
        out_specs=out_specs,
        out_shape=out_shape,
        compiler_params=_params("arbitrary"),
        name="merge_out",
    )(*args)
    return (out[0], out[1]) if emit_h else (out, None)


def _tiles(seq, d_model):
    return dict(tm_in=min(1024, seq), tn_in=2048, tm_merge=min(256, seq),
                diff=(min(512, seq), min(512, seq)),
                sb=(min(512, seq), min(512, seq) // 2),
                nsa=(min(256, seq), min(512, seq)))


def kernel(x, c, norm_pre_g, norm_post_g, w_ada, b_ada, w_in, lambda_q1, lambda_k1, lambda_q2,
           lambda_k2, diff_norm_g, cmp_pe_k, cmp_w1_k, cmp_w2_k, cmp_pe_v, cmp_w1_v, cmp_w2_v,
           w_branch, w_out):
    bsz, seq, d = x.shape
    depth = w_in.shape[0]
    t = _tiles(seq, d)
    offs = _rest_offsets(d)
    assert d % 512 == 0 and ROPE_COLS % d == 0 and seq % t["tm_in"] == 0 and seq % 128 == 0
    n_total = -(-offs["END"] // t["tn_in"]) * t["tn_in"]
    tables = _rope_tables(seq)
    mod = _ada(c, w_ada, b_ada)
    xf = x.reshape(bsz * seq, d)
    mods = [jnp.split(mod[l], 3, axis=-1) for l in range(depth)]
    h = _prenorm(xf, norm_pre_g[0], mods[0][1], mods[0][0], seq, t["tm_merge"])
    wb_all, wo_all = w_branch.astype(BF16), w_out.astype(BF16)
    wp_all = _permute_w_in_t(w_in, d, n_total)
    for l in range(depth):
        proj = _inproj(h, tables, wp_all, l, seq, t["tm_in"], t["tn_in"])
        ya = _diff_attn(proj, offs, diff_norm_g[l], lambda_q1[l], lambda_k1[l], lambda_q2[l],
                        lambda_k2[l], l, bsz, seq, *t["diff"])
        kcmp = _compress(proj, P_NKC, cmp_pe_k[l], cmp_w1_k[l], cmp_w2_k[l], bsz, seq)
        vcmp = _compress(proj, offs["NVC"], cmp_pe_v[l], cmp_w1_v[l], cmp_w2_v[l], bsz, seq)
        yb = _nsa_attn(proj, offs, kcmp, vcmp, bsz, seq, *t["nsa"])
        yc = _sb_attn(proj, offs, bsz, seq, *t["sb"])
        next_norm = (norm_pre_g[l + 1], mods[l + 1][1], mods[l + 1][0]) if l + 1 < depth else None
        xf, h = _merge(xf, ya, yb, yc, proj, wb_all, wo_all, l, norm_post_g[l], mods[l][2],
                       next_norm, seq, t["tm_merge"])
    return xf.reshape(bsz, seq, d)
```

```python
import functools
import math

import jax
import jax.numpy as jnp
import numpy as np
from jax import lax
from jax.experimental import pallas as pl
from jax.experimental.pallas import tpu as pltpu

F32 = jnp.float32
BF16 = jnp.bfloat16

HEAD_DIM = 128
BR_WIDTH = 1024
N_BRANCH = 3
ROPE_THETA = 500000.0
ROPE_DIM = HEAD_DIM // 4
ROPE_HALF = ROPE_DIM // 2
EPS = 1e-6
NEG_INF = -1e30
FORCE_SCORE = 1e6
DIFF_HEADS = BR_WIDTH // (2 * HEAD_DIM)
DIFF_NORM_EPS = 1e-5
NSA_HEADS = BR_WIDTH // HEAD_DIM
NSA_GROUPS = 2
NSA_REP = NSA_HEADS // NSA_GROUPS
CMP_LEN = 32
CMP_STRIDE = 16
SLC_LEN = 64
SLC_SHIFT = SLC_LEN.bit_length() - 1
SLC_TOPK = 16
WINDOW = 512
SB_HEADS = BR_WIDTH // HEAD_DIM
NSA_KV = NSA_GROUPS * HEAD_DIM
ATTN_SCALE = HEAD_DIM ** -0.5
Q_PRESCALE = ATTN_SCALE * math.log2(math.e)

LANES = 128
SLC_BLOCKS_PAD = 128
UNSELECTED_BIAS = 2.0 ** 30
SB_UNDERFLOW = -104.0
VMEM_LIMIT = 56 * 1024 * 1024

_O_AQ, _O_AK, _O_AV, _O_AZ = 0, 1024, 2048, 3072
_O_NQ, _O_NKC, _O_NVC, _O_NKS, _O_NVS, _O_NKW, _O_NVW = 4096, 5120, 5376, 5632, 5888, 6144, 6400
_O_NG, _O_NZ = 6656, 6680
_O_CQ, _O_CK, _O_CV, _O_CZ, _O_MG = 7704, 8728, 9752, 10776, 11800
N_GATES = 3 * NSA_HEADS
P_AQ, P_AK, P_NQ, P_NKC, P_NKS, P_NKW = 0, 1024, 2048, 3072, 3328, 3584
ROPE_COLS = 4096
P_MG = ROPE_COLS


def _rest_offsets(d_model):
    r0 = ROPE_COLS + N_BRANCH * d_model
    offs = dict(AV=r0, AZ=r0 + 1024, NZ=r0 + 2048, CQ=r0 + 3072, CK=r0 + 4096, CV=r0 + 5120,
                CZ=r0 + 6144, NVC=r0 + 7168, NVS=r0 + 7424, NVW=r0 + 7680, NG=r0 + 7936)
    offs["END"] = r0 + 7936 + NSA_GROUPS * LANES
    return offs


PREP_ROWS = 2 * LANES
GATES_PER_GROUP = N_GATES // NSA_GROUPS
_ZERO, _COPY, _GATES = 0, 1, 2


def _prep_plan(d_model, n_total):
    offs = _rest_offsets(d_model)
    pieces = [(P_AQ, _O_AQ, 1024), (P_AK, _O_AK, 1024), (P_NQ, _O_NQ, 1024), (P_NKC, _O_NKC, 256),
              (P_NKS, _O_NKS, 256), (P_NKW, _O_NKW, 256), (P_MG, _O_MG, N_BRANCH * d_model),
              (offs["AV"], _O_AV, 1024), (offs["AZ"], _O_AZ, 1024), (offs["NZ"], _O_NZ, 1024),
              (offs["CQ"], _O_CQ, 4096), (offs["NVC"], _O_NVC, 256), (offs["NVS"], _O_NVS, 256),
              (offs["NVW"], _O_NVW, 256), (offs["NG"], _O_NG, PREP_ROWS)]
    n_tiles = n_total // PREP_ROWS
    kind = np.full((n_tiles,), _ZERO, np.int32)
    src = np.zeros((n_tiles,), np.int32)
    for dest, first, width in pieces:
        for off in range(0, width, PREP_ROWS):
            assert (dest + off) % PREP_ROWS == 0 and (first + off) % 8 == 0
            kind[(dest + off) // PREP_ROWS] = _COPY
            src[(dest + off) // PREP_ROWS] = first + off
    assert NSA_GROUPS * LANES == PREP_ROWS
    kind[offs["NG"] // PREP_ROWS] = _GATES
    return jnp.asarray(kind), jnp.asarray(src // 8)


def _prep_kernel(kind_ref, src_ref, a_ref, o_ref):
    kind = kind_ref[pl.program_id(1)]

    @pl.when(kind == _ZERO)
    def _():
        o_ref[...] = jnp.zeros(o_ref.shape, BF16)

    @pl.when(kind == _COPY)
    def _():
        o_ref[...] = a_ref[0].astype(BF16)

    @pl.when(kind == _GATES)
    def _():
        o_ref[...] = jnp.zeros(o_ref.shape, BF16)
        for g in range(NSA_GROUPS):
            o_ref[pl.ds(g * LANES, GATES_PER_GROUP), :] = (
                a_ref[0, pl.ds(g * GATES_PER_GROUP, GATES_PER_GROUP), :].astype(BF16))


def _prep_w_in(w_in, d_model, n_total):
    wt = jnp.swapaxes(w_in, 1, 2)
    depth, n_in, d = wt.shape
    kind, src = _prep_plan(d_model, n_total)
    grid_spec = pltpu.PrefetchScalarGridSpec(
        num_scalar_prefetch=2,
        grid=(depth, n_total // PREP_ROWS),
        in_specs=[pl.BlockSpec((pl.Element(1), pl.Element(PREP_ROWS), pl.Element(d)),
                               lambda l, j, kind, src: (l, src[j] * 8, 0))],
        out_specs=pl.BlockSpec((None, PREP_ROWS, d), lambda l, j, kind, src: (l, j, 0)),
    )
    return pl.pallas_call(
        _prep_kernel,
        grid_spec=grid_spec,
        out_shape=jax.ShapeDtypeStruct((depth, n_total, d), BF16),
        compiler_params=_params("arbitrary", "arbitrary"),
        name="w_in_prep",
    )(kind, src, wt)


def _rope_tables(seq):
    pos = jnp.arange(seq, dtype=F32)
    inv = ROPE_THETA ** (-jnp.arange(0, ROPE_DIM, 2, dtype=F32) / ROPE_DIM)
    ang = pos[:, None] * inv[None, :]
    cos, sin = jnp.cos(ang), jnp.sin(ang)
    ones = jnp.ones((seq, HEAD_DIM - ROPE_DIM), F32)
    zh = jnp.zeros((seq, ROPE_HALF), F32)
    zr = jnp.zeros((seq, HEAD_DIM - ROPE_DIM), F32)
    c = jnp.concatenate([cos, cos, ones], axis=1)
    s1 = jnp.concatenate([-sin, zh, zr], axis=1)
    s2 = jnp.concatenate([zh, sin, zr], axis=1)
    return c, s1, s2


def _sigmoid(x):
    return 1.0 / (1.0 + jnp.exp(-x))


def _silu(x):
    return x * _sigmoid(x)


def _dot_nt(a, b):
    return lax.dot_general(a, b, (((1,), (1,)), ((), ())), preferred_element_type=F32)


def _dot(a, b):
    return jnp.dot(a, b, preferred_element_type=F32)


def _params(*sem):
    return pltpu.CompilerParams(dimension_semantics=sem, vmem_limit_bytes=VMEM_LIMIT)


def _ada_kernel(c_ref, w_ref, b_ref, o_ref):
    c = c_ref[...]
    o_ref[0] = _dot(_silu(c).astype(BF16), w_ref[0].astype(BF16)) + b_ref[0]


def _ada(c, w_ada, b_ada):
    depth, d, n3 = w_ada.shape
    bsz = c.shape[0]
    rows = max(8, bsz)
    cp = jnp.zeros((rows, d), F32).at[:bsz].set(c)
    tn = math.gcd(1024, n3)
    out = pl.pallas_call(
        _ada_kernel,
        grid=(depth, n3 // tn),
        in_specs=[pl.BlockSpec((rows, d), lambda l, j: (0, 0)),
                  pl.BlockSpec((1, d, tn), lambda l, j: (l, 0, j)),
                  pl.BlockSpec((1, 1, tn), lambda l, j: (l, 0, j))],
        out_specs=pl.BlockSpec((1, rows, tn), lambda l, j: (l, 0, j)),
        out_shape=jax.ShapeDtypeStruct((depth, rows, n3), F32),
        compiler_params=_params("arbitrary", "arbitrary"),
        name="ada_mod",
    )(cp, w_ada, b_ada.reshape(depth, 1, n3))
    return out[:, :bsz]


def _modulated_norm(x, g, scale, shift):
    y = x * lax.rsqrt(jnp.mean(x * x, axis=-1, keepdims=True) + EPS) * g
    return (y * (1.0 + scale) + shift).astype(BF16)


def _prenorm_kernel(x_ref, g_ref, sc_ref, sh_ref, h_ref):
    h_ref[...] = _modulated_norm(x_ref[...], g_ref[...], sc_ref[0], sh_ref[0])


def _prenorm(xf, g, scale, shift, seq, tm):
    rows, d = xf.shape
    tpb = seq // tm
    mod_spec = pl.BlockSpec((1, 1, d), lambda i: (i // tpb, 0, 0))
    return pl.pallas_call(
        _prenorm_kernel,
        grid=(rows // tm,),
        in_specs=[pl.BlockSpec((tm, d), lambda i: (i, 0)), pl.BlockSpec((1, d), lambda i: (0, 0)),
                  mod_spec, mod_spec],
        out_specs=pl.BlockSpec((tm, d), lambda i: (i, 0)),
        out_shape=jax.ShapeDtypeStruct((rows, d), BF16),
        compiler_params=_params("arbitrary"),
        name="pre_norm",
    )(xf, g.reshape(1, d), scale[:, None, :], shift[:, None, :])


def _inproj_kernel(h_ref, c_ref, s1_ref, s2_ref, w_ref, o_ref, *, n_rope_tiles, tn):
    j = pl.program_id(1)
    y = _dot_nt(h_ref[...], w_ref[...])

    @pl.when(j < n_rope_tiles)
    def _():
        c, s1, s2 = c_ref[...], s1_ref[...], s2_ref[...]
        for hh in range(tn // HEAD_DIM):
            sl = slice(hh * HEAD_DIM, (hh + 1) * HEAD_DIM)
            yh = y[:, sl]
            r = (yh * c + pltpu.roll(yh, HEAD_DIM - ROPE_HALF, 1) * s1
                 + pltpu.roll(yh, ROPE_HALF, 1) * s2)
            col = j * tn + hh * HEAD_DIM
            is_q = ((col >= P_AQ) & (col < P_AQ + BR_WIDTH)) | ((col >= P_NQ) & (col < P_NQ + BR_WIDTH))
            o_ref[:, sl] = (r * jnp.where(is_q, Q_PRESCALE, 1.0)).astype(BF16)

    @pl.when(j >= n_rope_tiles)
    def _():
        o_ref[...] = y.astype(BF16)


def _inproj(h, tables, wp_all, layer, seq, tm, tn):
    rows, d = h.shape
    n_total = wp_all.shape[1]
    tpb = seq // tm
    c, s1, s2 = tables
    tab_spec = pl.BlockSpec((tm, HEAD_DIM), lambda i, j: (i % tpb, 0))
    return pl.pallas_call(
        functools.partial(_inproj_kernel, n_rope_tiles=ROPE_COLS // tn, tn=tn),
        grid=(rows // tm, n_total // tn),
        in_specs=[pl.BlockSpec((tm, d), lambda i, j: (i, 0)),
                  tab_spec, tab_spec, tab_spec,
                  pl.BlockSpec((None, tn, d), lambda i, j: (layer, j, 0))],
        out_specs=pl.BlockSpec((tm, tn), lambda i, j: (i, j)),
        out_shape=jax.ShapeDtypeStruct((rows, n_total), BF16),
        compiler_params=_params("arbitrary", "arbitrary"),
        name="in_proj",
    )(h, c, s1, s2, wp_all)


def _softmax_update(carry, s, v):
    m, l, acc = carry
    m_new = jnp.maximum(m, jnp.max(s, axis=-1, keepdims=True))
    alpha = jnp.exp2(m - m_new)
    p = jnp.exp2(s - m_new)
    l = alpha * l + jnp.sum(p, axis=-1, keepdims=True)
    acc = alpha * acc + _dot(p.astype(BF16), v)
    return m_new, l, acc


def _softmax_init(rows, width):
    return (jnp.full((rows, 1), NEG_INF, F32), jnp.zeros((rows, 1), F32),
            jnp.zeros((rows, width), F32))


def _diff_kernel(q_ref, k_ref, v_ref, z_ref, ng_ref, lq1_ref, lk1_ref, lq2_ref, lk2_ref, o_ref,
                 *, tq, tk, lam_init):
    i = pl.program_id(2)
    t0 = i * tq
    q = q_ref[...]
    qs = (q[:, :HEAD_DIM], q[:, HEAD_DIM:])
    n_full = t0 // tk

    def step(jt, carry, masked):
        k0 = pl.multiple_of(jt * tk, tk)
        k = k_ref[pl.ds(k0, tk), :]
        v = v_ref[pl.ds(k0, tk), :]
        s = [_dot_nt(qs[cc], k[:, cc * HEAD_DIM:(cc + 1) * HEAD_DIM]) for cc in range(2)]
        if masked:
            kpos = k0 + lax.broadcasted_iota(jnp.int32, (tq, tk), 1)
            tpos = t0 + lax.broadcasted_iota(jnp.int32, (tq, tk), 0)
            s = [jnp.where(kpos <= tpos, sc, NEG_INF) for sc in s]
        return tuple(_softmax_update(carry[cc], s[cc], v) for cc in range(2))

    init = (_softmax_init(tq, 2 * HEAD_DIM), _softmax_init(tq, 2 * HEAD_DIM))
    carry = lax.fori_loop(0, n_full, lambda jt, c: step(jt, c, False), init)
    carry = step(n_full, carry, True)

    lam = (jnp.exp(jnp.sum(lq1_ref[...] * lk1_ref[...], axis=-1, keepdims=True))
           - jnp.exp(jnp.sum(lq2_ref[...] * lk2_ref[...], axis=-1, keepdims=True)) + lam_init)
    o0 = carry[0][2] / carry[0][1]
    o1 = carry[1][2] / carry[1][1]
    o = o0 - lam * o1
    o = o * lax.rsqrt(jnp.mean(o * o, axis=-1, keepdims=True) + DIFF_NORM_EPS) * ng_ref[...]
    o = o * (1.0 - lam_init)
    o_ref[...] = (o * _silu(z_ref[...].astype(F32))).astype(BF16)


def _diff_attn(proj, offs, norm_g, lq1, lk1, lq2, lk2, layer_idx, bsz, seq, tq, tk):
    nq = seq // tq
    w = 2 * HEAD_DIM
    lam_init = 0.8 - 0.6 * math.exp(-0.3 * layer_idx)
    vec = pl.BlockSpec((1, HEAD_DIM), lambda b, h, i: (0, 0))
    return pl.pallas_call(
        functools.partial(_diff_kernel, tq=tq, tk=tk, lam_init=lam_init),
        grid=(bsz, DIFF_HEADS, nq),
        in_specs=[pl.BlockSpec((tq, w), lambda b, h, i: (b * nq + i, P_AQ // w + h)),
                  pl.BlockSpec((seq, w), lambda b, h, i: (b, P_AK // w + h)),
                  pl.BlockSpec((seq, w), lambda b, h, i: (b, offs["AV"] // w + h)),
                  pl.BlockSpec((tq, w), lambda b, h, i: (b * nq + i, offs["AZ"] // w + h)),
                  pl.BlockSpec((1, w), lambda b, h, i: (0, 0)),
                  vec, vec, vec, vec],
        out_specs=pl.BlockSpec((tq, w), lambda b, h, i: (b * nq + i, h)),
        out_shape=jax.ShapeDtypeStruct((bsz * seq, BR_WIDTH), BF16),
        compiler_params=_params("arbitrary", "arbitrary", "arbitrary"),
        name="diff_attn",
    )(proj, proj, proj, proj, norm_g.reshape(1, w), lq1.reshape(1, -1), lk1.reshape(1, -1),
      lq2.reshape(1, -1), lk2.reshape(1, -1))


def _sb_kernel(q_ref, k_ref, v_ref, z_ref, o_ref, *, tq, tk):
    i = pl.program_id(2)
    t0 = i * tq
    n_full = t0 // tk
    qh = (q_ref[pl.ds(0, tk), :], q_ref[pl.ds(tk, tk), :])
    upper2 = jnp.where((lax.broadcasted_iota(jnp.int32, (2 * tk, tk), 0) & (tk - 1))
                       > lax.broadcasted_iota(jnp.int32, (2 * tk, tk), 1), 1.0, 0.0).astype(BF16)
    strict = (lax.broadcasted_iota(jnp.int32, (tk, tk), 1) < lax.broadcasted_iota(jnp.int32, (tk, tk), 0))

    def steps(items):
        kv = []
        for _, jt, _, _ in items:
            k0 = pl.multiple_of(jt * tk, tk)
            kv.append((k_ref[pl.ds(k0, tk), :], v_ref[pl.ds(k0, tk), :]))
        z = [_dot_nt(qh[it[0]], kv[n][0]) * ATTN_SCALE for n, it in enumerate(items)]
        log_1m = [-(jnp.maximum(zz, 0.0) + jnp.log(1.0 + jnp.exp(-jnp.abs(zz)))) for zz in z]
        log_1m = [jnp.where(strict, lm, 0.0) if it[3] else lm for lm, it in zip(log_1m, items)]
        stacked = []
        for lm in log_1m:
            hi = lm.astype(BF16)
            stacked.append(jnp.concatenate([hi, (lm - hi.astype(F32)).astype(BF16)], axis=1))
        after = [_dot(st, upper2) + it[2][0] for st, it in zip(stacked, items)]
        a = [jnp.exp(zz + lm + af) for zz, lm, af in zip(z, log_1m, after)]
        a = [jnp.where(strict, aa, 0.0) if it[3] else aa for aa, it in zip(a, items)]
        pv = [_dot(aa.astype(BF16), kv[n][1]) for n, aa in enumerate(a)]
        return [(it[2][0] + jnp.sum(lm, axis=-1, keepdims=True), it[2][1] + p)
                for it, lm, p in zip(items, log_1m, pv)]

    zero = (jnp.zeros((tk, 1), F32), jnp.zeros((tk, HEAD_DIM), F32))
    c_b, c_a = steps([(1, n_full + 1, zero, True), (0, n_full, zero, True)])
    (c_b,) = steps([(1, n_full, c_b, False)])

    def alive(ca, cb):
        return jnp.maximum(jnp.max(ca[0]), jnp.max(cb[0]))

    def cond(c):
        return (c[0] < n_full) & (c[1] > SB_UNDERFLOW)

    def body(c):
        jt = n_full - 1 - c[0]
        ca, cb = steps([(0, jt, c[2], False), (1, jt, c[3], False)])
        return c[0] + 1, alive(ca, cb), ca, cb

    _, _, c_a, c_b = lax.while_loop(cond, body, (jnp.int32(0), alive(c_a, c_b), c_a, c_b))
    zg = z_ref[...].astype(F32)
    o_ref[pl.ds(0, tk), :] = (c_a[1] * _silu(zg[:tk])).astype(BF16)
    o_ref[pl.ds(tk, tk), :] = (c_b[1] * _silu(zg[tk:])).astype(BF16)


def _sb_attn(proj, offs, bsz, seq, tq, tk):
    nq = seq // tq
    w = HEAD_DIM
    assert tq == 2 * tk
    return pl.pallas_call(
        functools.partial(_sb_kernel, tq=tq, tk=tk),
        grid=(bsz, SB_HEADS, nq),
        in_specs=[pl.BlockSpec((tq, w), lambda b, h, i: (b * nq + i, offs["CQ"] // w + h)),
                  pl.BlockSpec((seq, w), lambda b, h, i: (b, offs["CK"] // w + h)),
                  pl.BlockSpec((seq, w), lambda b, h, i: (b, offs["CV"] // w + h)),
                  pl.BlockSpec((tq, w), lambda b, h, i: (b * nq + i, offs["CZ"] // w + h))],
        out_specs=pl.BlockSpec((tq, w), lambda b, h, i: (b * nq + i, h)),
        out_shape=jax.ShapeDtypeStruct((bsz * seq, BR_WIDTH), BF16),
        compiler_params=_params("arbitrary", "arbitrary", "arbitrary"),
        name="stick_breaking",
    )(proj, proj, proj, proj)


def _compress_kernel(x_ref, pe_ref, w1_ref, w2_ref, o_ref, xs_ref, *, seq):
    nc = seq // CMP_STRIDE
    xs_ref[pl.ds(0, seq), :] = x_ref[...].astype(F32)
    xs_ref[pl.ds(seq, CMP_STRIDE), :] = jnp.zeros((CMP_STRIDE, HEAD_DIM), F32)
    hid = None
    for l in range(CMP_LEN):
        rows = xs_ref[pl.ds(l, nc, stride=CMP_STRIDE), :] + pe_ref[l:l + 1, :]
        t = _dot(rows.astype(BF16), w1_ref[l * HEAD_DIM:(l + 1) * HEAD_DIM, :])
        hid = t if hid is None else hid + t
    o_ref[0, 0] = _dot(_silu(hid).astype(BF16), w2_ref[...]).astype(BF16)


def _compress(proj, col_off, pe, w1, w2, bsz, seq):
    nc = seq // CMP_STRIDE
    w = HEAD_DIM
    return pl.pallas_call(
        functools.partial(_compress_kernel, seq=seq),
        grid=(bsz, NSA_GROUPS),
        in_specs=[pl.BlockSpec((seq, w), lambda b, g: (b, col_off // w + g)),
                  pl.BlockSpec((CMP_LEN, w), lambda b, g: (0, 0)),
                  pl.BlockSpec((CMP_LEN * w, w), lambda b, g: (0, 0)),
                  pl.BlockSpec((w, w), lambda b, g: (0, 0))],
        out_specs=pl.BlockSpec((1, 1, nc, w), lambda b, g: (b, g, 0, 0)),
        out_shape=jax.ShapeDtypeStruct((bsz, NSA_GROUPS, nc, w), BF16),
        scratch_shapes=[pltpu.VMEM((seq + CMP_STRIDE, w), F32)],
        compiler_params=_params("arbitrary", "arbitrary"),
        name="nsa_compress",
    )(proj, pe, w1.astype(BF16), w2.astype(BF16))


def _nsa_kernel(q_ref, kc_ref, vc_ref, ks_ref, vs_ref, kw_ref, vw_ref, g_ref, z_ref, o_ref, imp_ref,
                *, tq, tk, seq):
    i = pl.program_id(2)
    t0 = i * tq
    nc = seq // CMP_STRIDE
    nb = seq // SLC_LEN
    nbp = SLC_BLOCKS_PAD
    rep = NSA_REP
    q = q_ref[...]
    qh = [q[:, r * HEAD_DIM:(r + 1) * HEAD_DIM] for r in range(rep)]

    kc, vc = kc_ref[0, 0], vc_ref[0, 0]
    tpos_c = t0 + lax.broadcasted_iota(jnp.int32, (tq, nc), 0)
    ncol = lax.broadcasted_iota(jnp.int32, (tq, nc), 1)
    cvalid = CMP_STRIDE * ncol + (CMP_LEN - 1) <= tpos_c
    psum = jnp.zeros((tq, nc), F32)
    o_cmp = []
    for r in range(rep):
        s = jnp.where(cvalid, _dot_nt(qh[r], kc), NEG_INF)
        e = jnp.where(cvalid, jnp.exp2(s - jnp.max(s, axis=-1, keepdims=True)), 0.0)
        den = jnp.sum(e, axis=-1, keepdims=True)
        p = e / jnp.where(den > 0.0, den, 1.0)
        psum = psum + p
        o_cmp.append(_dot(p.astype(BF16), vc))

    nb8 = imp_ref.shape[0]
    jrow = lax.broadcasted_iota(jnp.int32, (nb8, nc), 0)
    ncol2 = lax.broadcasted_iota(jnp.int32, (nb8, nc), 1)
    overlap = jnp.where((CMP_STRIDE * ncol2 < SLC_LEN * jrow + SLC_LEN)
                        & (CMP_STRIDE * ncol2 + CMP_LEN > SLC_LEN * jrow)
                        & (ncol2 < nc - 1) & (jrow < nb), 1.0, 0.0).astype(BF16)
    p1 = psum.astype(BF16)
    r1 = psum - p1.astype(F32)
    p2 = r1.astype(BF16)
    p3 = (r1 - p2.astype(F32)).astype(BF16)
    imp = _dot_nt(overlap, p1) + _dot_nt(overlap, p2) + _dot_nt(overlap, p3)
    jt_ = lax.broadcasted_iota(jnp.int32, (nb8, tq), 0)
    tblk = jnp.right_shift(t0 + lax.broadcasted_iota(jnp.int32, (nb8, tq), 1), SLC_SHIFT)
    imp = jnp.where((jt_ == tblk) | (jt_ == 0), FORCE_SCORE, imp)
    imp = jnp.where(jt_ <= tblk, imp, NEG_INF)
    imp_ref[...] = imp

    def rank_body(jp, cnt):
        row = imp_ref[pl.ds(jp, 1), :]
        beats = (row > imp) | ((row == imp) & (jp < jt_))
        return cnt + jnp.where(beats, 1.0, 0.0)

    n_live = jnp.minimum((t0 + tq - 1) // SLC_LEN + 1, nb)
    cnt = lax.fori_loop(0, n_live, rank_body, jnp.zeros((nb8, tq), F32))
    keep = (cnt < float(min(SLC_TOPK, nb))) & (imp > 0.5 * NEG_INF)
    unsel_t = jnp.where(keep, 0.0, 1.0)
    if nbp > nb8:
        unsel_t = jnp.concatenate([unsel_t, jnp.zeros((nbp - nb8, tq), F32)], axis=0)
    unselected = unsel_t.T.astype(BF16)

    rows = rep * tq
    q_sel = jnp.concatenate([jnp.concatenate([qh[r], unselected], axis=1) for r in range(rep)], axis=0)
    q_win = jnp.concatenate(qh, axis=0)

    def slc_step(jt, carry, masked):
        k0 = pl.multiple_of(jt * tk, tk)
        k = ks_ref[pl.ds(k0, tk), :]
        v = vs_ref[pl.ds(k0, tk), :]
        kblk = jnp.right_shift(k0 + lax.broadcasted_iota(jnp.int32, (tk, nbp), 0), SLC_SHIFT)
        bias = jnp.where(lax.broadcasted_iota(jnp.int32, (tk, nbp), 1) == kblk,
                         -UNSELECTED_BIAS, 0.0).astype(BF16)
        s = _dot_nt(q_sel, jnp.concatenate([k, bias], axis=1))
        if masked:
            causal = (k0 + lax.broadcasted_iota(jnp.int32, (tq, tk), 1)
                      <= t0 + lax.broadcasted_iota(jnp.int32, (tq, tk), 0))
            s = s + jnp.concatenate([jnp.where(causal, 0.0, NEG_INF)] * rep, axis=0)
        return _softmax_update(carry, s, v)

    n_full = t0 // tk
    carry = lax.fori_loop(0, n_full, lambda jt, c: slc_step(jt, c, False), _softmax_init(rows, HEAD_DIM))
    m_s, l_s, acc_s = slc_step(n_full, carry, True)
    o_slc = acc_s / l_s

    span = min(WINDOW + tq, seq)
    start = pl.multiple_of(jnp.maximum(t0 - WINDOW, 0), tq)
    kpos = start + lax.broadcasted_iota(jnp.int32, (tq, span), 1)
    row_w = t0 + lax.broadcasted_iota(jnp.int32, (tq, span), 0)
    in_window = jnp.where(kpos <= row_w, jnp.where(kpos > row_w - WINDOW, 0.0, NEG_INF), NEG_INF)
    s = _dot_nt(q_win, kw_ref[pl.ds(start, span), :]) + jnp.concatenate([in_window] * rep, axis=0)
    p = jnp.exp2(s - jnp.max(s, axis=-1, keepdims=True))
    o_win = _dot(p.astype(BF16), vw_ref[pl.ds(start, span), :]) / jnp.sum(p, axis=-1, keepdims=True)

    gates = _sigmoid(g_ref[...].astype(F32))
    z = z_ref[...].astype(F32)
    for r in range(rep):
        rs = slice(r * tq, (r + 1) * tq)
        cs = slice(r * HEAD_DIM, (r + 1) * HEAD_DIM)
        o = (gates[:, 3 * r:3 * r + 1] * o_cmp[r] + gates[:, 3 * r + 1:3 * r + 2] * o_slc[rs]
             + gates[:, 3 * r + 2:3 * r + 3] * o_win[rs])
        o_ref[:, cs] = (o * _silu(z[:, cs])).astype(BF16)


def _nsa_attn(proj, offs, kcmp, vcmp, bsz, seq, tq, tk):
    nq = seq // tq
    nc = seq // CMP_STRIDE
    w = HEAD_DIM
    gw = NSA_REP * HEAD_DIM
    assert seq // SLC_LEN <= SLC_BLOCKS_PAD and tq & (tq - 1) == 0 and tk % tq == 0
    kv = lambda off: pl.BlockSpec((seq, w), lambda b, g, i: (b, off // w + g))
    cmp_spec = pl.BlockSpec((1, 1, nc, w), lambda b, g, i: (b, g, 0, 0))
    return pl.pallas_call(
        functools.partial(_nsa_kernel, tq=tq, tk=tk, seq=seq),
        grid=(bsz, NSA_GROUPS, nq),
        in_specs=[pl.BlockSpec((tq, gw), lambda b, g, i: (b * nq + i, P_NQ // gw + g)),
                  cmp_spec, cmp_spec,
                  kv(P_NKS), kv(offs["NVS"]), kv(P_NKW), kv(offs["NVW"]),
                  pl.BlockSpec((tq, LANES), lambda b, g, i: (b * nq + i, offs["NG"] // LANES + g)),
                  pl.BlockSpec((tq, gw), lambda b, g, i: (b * nq + i, offs["NZ"] // gw + g))],
        out_specs=pl.BlockSpec((tq, gw), lambda b, g, i: (b * nq + i, g)),
        out_shape=jax.ShapeDtypeStruct((bsz * seq, BR_WIDTH), BF16),
        scratch_shapes=[pltpu.VMEM((-(-(seq // SLC_LEN) // 8) * 8, tq), F32)],
        compiler_params=_params("arbitrary", "arbitrary", "arbitrary"),
        name="nsa_attn",
    )(proj, kcmp, vcmp, proj, proj, proj, proj, proj, proj)


def _merge_kernel(x_ref, ya_ref, yb_ref, yc_ref, mg0_ref, mg1_ref, mg2_ref, wb_ref, wo_ref, gp_ref,
                  gate_ref, *rest, emit_h):
    merged = None
    for n, (y_ref, mg_ref) in enumerate(((ya_ref, mg0_ref), (yb_ref, mg1_ref), (yc_ref, mg2_ref))):
        t = _dot(y_ref[...], wb_ref[n]) * _sigmoid(mg_ref[...].astype(F32))
        merged = t if merged is None else merged + t
    o = _dot(merged.astype(BF16), wo_ref[...])
    o = o * lax.rsqrt(jnp.mean(o * o, axis=-1, keepdims=True) + EPS) * gp_ref[...]
    x_new = x_ref[...] + gate_ref[0] * o
    if emit_h:
        g_next_ref, sc_next_ref, sh_next_ref, o_ref, h_ref = rest
        h_ref[...] = _modulated_norm(x_new, g_next_ref[...], sc_next_ref[0], sh_next_ref[0])
    else:
        (o_ref,) = rest
    o_ref[...] = x_new


def _merge(xf, ya, yb, yc, proj, wb_all, wo_all, layer, g_post, gate, next_norm, seq, tm):
    rows, d = xf.shape
    tpb = seq // tm
    row = lambda w_: pl.BlockSpec((tm, w_), lambda i: (i, 0))
    mg = lambda n: pl.BlockSpec((tm, d), lambda i: (i, P_MG // d + n))
    vec = pl.BlockSpec((1, d), lambda i: (0, 0))
    mod_spec = pl.BlockSpec((1, 1, d), lambda i: (i // tpb, 0, 0))
    const = pl.Buffered(1)
    emit_h = next_norm is not None
    in_specs = [row(d), row(BR_WIDTH), row(BR_WIDTH), row(BR_WIDTH), mg(0), mg(1), mg(2),
                pl.BlockSpec((None, N_BRANCH, BR_WIDTH, d), lambda i: (layer, 0, 0, 0), pipeline_mode=const),
                pl.BlockSpec((None, d, d), lambda i: (layer, 0, 0), pipeline_mode=const), vec, mod_spec]
    args = [xf, ya, yb, yc, proj, proj, proj, wb_all, wo_all, g_post.reshape(1, d),
            gate[:, None, :]]
    out_specs, out_shape = row(d), jax.ShapeDtypeStruct((rows, d), F32)
    if emit_h:
        g_next, sc_next, sh_next = next_norm
        in_specs += [vec, mod_spec, mod_spec]
        args += [g_next.reshape(1, d), sc_next[:, None, :], sh_next[:, None, :]]
        out_specs, out_shape = [out_specs, row(d)], [out_shape, jax.ShapeDtypeStruct((rows, d), BF16)]
    out = pl.pallas_call(
        functools.partial(_merge_kernel, emit_h=emit_h),
        grid=(rows // tm,),
        in_specs=in_specs,
        out_specs=out_specs,
        out_shape=out_shape,
        compiler_params=_params("arbitrary"),
        name="merge_out",
    )(*args)
    return (out[0], out[1]) if emit_h else (out, None)


def _tiles(seq, d_model):
    return dict(tm_in=min(1024, seq), tn_in=2048, tm_merge=min(256, seq),
                diff=(min(512, seq), min(512, seq)),
                sb=(min(512, seq), min(512, seq) // 2),
                nsa=(min(256, seq), min(512, seq)))


def kernel(x, c, norm_pre_g, norm_post_g, w_ada, b_ada, w_in, lambda_q1, lambda_k1, lambda_q2,
           lambda_k2, diff_norm_g, cmp_pe_k, cmp_w1_k, cmp_w2_k, cmp_pe_v, cmp_w1_v, cmp_w2_v,
           w_branch, w_out):
    bsz, seq, d = x.shape
    depth = w_in.shape[0]
    t = _tiles(seq, d)
    offs = _rest_offsets(d)
    assert d % 512 == 0 and ROPE_COLS % d == 0 and seq % t["tm_in"] == 0 and seq % 128 == 0
    n_total = -(-offs["END"] // t["tn_in"]) * t["tn_in"]
    tables = _rope_tables(seq)
    mod = _ada(c, w_ada, b_ada)
    xf = x.reshape(bsz * seq, d)
    mods = [jnp.split(mod[l], 3, axis=-1) for l in range(depth)]
    h = _prenorm(xf, norm_pre_g[0], mods[0][1], mods[0][0], seq, t["tm_merge"])
    wb_all, wo_all = w_branch.astype(BF16), w_out.astype(BF16)
    wp_all = _prep_w_in(w_in, d, n_total)
    for l in range(depth):
        proj = _inproj(h, tables, wp_all, l, seq, t["tm_in"], t["tn_in"])
        ya = _diff_attn(proj, offs, diff_norm_g[l], lambda_q1[l], lambda_k1[l], lambda_q2[l],
                        lambda_k2[l], l, bsz, seq, *t["diff"])
        kcmp = _compress(proj, P_NKC, cmp_pe_k[l], cmp_w1_k[l], cmp_w2_k[l], bsz, seq)
        vcmp = _compress(proj, offs["NVC"], cmp_pe_v[l], cmp_w1_v[l], cmp_w2_v[l], bsz, seq)
        yb = _nsa_attn(proj, offs, kcmp, vcmp, bsz, seq, *t["nsa"])
        yc = _sb_attn(proj, offs, bsz, seq, *t["sb"])
        next_norm = (norm_pre_g[l + 1], mods[l + 1][1], mods[l + 1][0]) if l + 1 < depth else None
        xf, h = _merge(xf, ya, yb, yc, proj, wb_all, wo_all, l, norm_post_g[l], mods[l][2],
                       next_norm, seq, t["tm_merge"])
    return xf.reshape(bsz, seq, d)
```

```python
import functools
import math

import jax
import jax.numpy as jnp
import numpy as np
from jax import lax
from jax.experimental import pallas as pl
from jax.experimental.pallas import tpu as pltpu

F32 = jnp.float32
BF16 = jnp.bfloat16

HEAD_DIM = 128
BR_WIDTH = 1024
N_BRANCH = 3
ROPE_THETA = 500000.0
ROPE_DIM = HEAD_DIM // 4
ROPE_HALF = ROPE_DIM // 2
EPS = 1e-6
NEG_INF = -1e30
FORCE_SCORE = 1e6
DIFF_HEADS = BR_WIDTH // (2 * HEAD_DIM)
DIFF_NORM_EPS = 1e-5
NSA_HEADS = BR_WIDTH // HEAD_DIM
NSA_GROUPS = 2
NSA_REP = NSA_HEADS // NSA_GROUPS
CMP_LEN = 32
CMP_STRIDE = 16
SLC_LEN = 64
SLC_SHIFT = SLC_LEN.bit_length() - 1
SLC_TOPK = 16
WINDOW = 512
SB_HEADS = BR_WIDTH // HEAD_DIM
NSA_KV = NSA_GROUPS * HEAD_DIM
ATTN_SCALE = HEAD_DIM ** -0.5
Q_PRESCALE = ATTN_SCALE * math.log2(math.e)

LANES = 128
SLC_BLOCKS_PAD = 128
UNSELECTED_BIAS = 2.0 ** 30
SB_UNDERFLOW = -104.0
VMEM_LIMIT = 56 * 1024 * 1024

_O_AQ, _O_AK, _O_AV, _O_AZ = 0, 1024, 2048, 3072
_O_NQ, _O_NKC, _O_NVC, _O_NKS, _O_NVS, _O_NKW, _O_NVW = 4096, 5120, 5376, 5632, 5888, 6144, 6400
_O_NG, _O_NZ = 6656, 6680
_O_CQ, _O_CK, _O_CV, _O_CZ, _O_MG = 7704, 8728, 9752, 10776, 11800
N_GATES = 3 * NSA_HEADS
P_AQ, P_AK, P_NQ, P_NKC, P_NKS, P_NKW = 0, 1024, 2048, 3072, 3328, 3584
ROPE_COLS = 4096
P_MG = ROPE_COLS


def _rest_offsets(d_model):
    r0 = ROPE_COLS + N_BRANCH * d_model
    offs = dict(AV=r0, AZ=r0 + 1024, NZ=r0 + 2048, CQ=r0 + 3072, CK=r0 + 4096, CV=r0 + 5120,
                CZ=r0 + 6144, NVC=r0 + 7168, NVS=r0 + 7424, NVW=r0 + 7680, NG=r0 + 7936)
    offs["END"] = r0 + 7936 + NSA_GROUPS * LANES
    return offs


PREP_ROWS = 2 * LANES
GATES_PER_GROUP = N_GATES // NSA_GROUPS
_ZERO, _COPY, _GATES = 0, 1, 2


def _prep_plan(d_model, n_total):
    offs = _rest_offsets(d_model)
    pieces = [(P_AQ, _O_AQ, 1024), (P_AK, _O_AK, 1024), (P_NQ, _O_NQ, 1024), (P_NKC, _O_NKC, 256),
              (P_NKS, _O_NKS, 256), (P_NKW, _O_NKW, 256), (P_MG, _O_MG, N_BRANCH * d_model),
              (offs["AV"], _O_AV, 1024), (offs["AZ"], _O_AZ, 1024), (offs["NZ"], _O_NZ, 1024),
              (offs["CQ"], _O_CQ, 4096), (offs["NVC"], _O_NVC, 256), (offs["NVS"], _O_NVS, 256),
              (offs["NVW"], _O_NVW, 256), (offs["NG"], _O_NG, PREP_ROWS)]
    n_tiles = n_total // PREP_ROWS
    kind = np.full((n_tiles,), _ZERO, np.int32)
    src = np.zeros((n_tiles,), np.int32)
    for dest, first, width in pieces:
        for off in range(0, width, PREP_ROWS):
            assert (dest + off) % PREP_ROWS == 0 and (first + off) % 8 == 0
            kind[(dest + off) // PREP_ROWS] = _COPY
            src[(dest + off) // PREP_ROWS] = first + off
    assert NSA_GROUPS * LANES == PREP_ROWS
    kind[offs["NG"] // PREP_ROWS] = _GATES
    return jnp.asarray(kind), jnp.asarray(src // 8)


def _prep_kernel(kind_ref, src_ref, a_ref, o_ref):
    kind = kind_ref[pl.program_id(1)]

    @pl.when(kind == _ZERO)
    def _():
        o_ref[...] = jnp.zeros(o_ref.shape, BF16)

    @pl.when(kind == _COPY)
    def _():
        o_ref[...] = a_ref[0].astype(BF16)

    @pl.when(kind == _GATES)
    def _():
        o_ref[...] = jnp.zeros(o_ref.shape, BF16)
        for g in range(NSA_GROUPS):
            o_ref[pl.ds(g * LANES, GATES_PER_GROUP), :] = (
                a_ref[0, pl.ds(g * GATES_PER_GROUP, GATES_PER_GROUP), :].astype(BF16))


def _prep_w_in(w_in, d_model, n_total):
    wt = jnp.swapaxes(w_in, 1, 2)
    depth, n_in, d = wt.shape
    kind, src = _prep_plan(d_model, n_total)
    grid_spec = pltpu.PrefetchScalarGridSpec(
        num_scalar_prefetch=2,
        grid=(depth, n_total // PREP_ROWS),
        in_specs=[pl.BlockSpec((pl.Element(1), pl.Element(PREP_ROWS), pl.Element(d)),
                               lambda l, j, kind, src: (l, src[j] * 8, 0))],
        out_specs=pl.BlockSpec((None, PREP_ROWS, d), lambda l, j, kind, src: (l, j, 0)),
    )
    return pl.pallas_call(
        _prep_kernel,
        grid_spec=grid_spec,
        out_shape=jax.ShapeDtypeStruct((depth, n_total, d), BF16),
        compiler_params=_params("arbitrary", "arbitrary"),
        name="w_in_prep",
    )(kind, src, wt)


def _rope_tables(seq):
    pos = jnp.arange(seq, dtype=F32)
    inv = ROPE_THETA ** (-jnp.arange(0, ROPE_DIM, 2, dtype=F32) / ROPE_DIM)
    ang = pos[:, None] * inv[None, :]
    cos, sin = jnp.cos(ang), jnp.sin(ang)
    ones = jnp.ones((seq, HEAD_DIM - ROPE_DIM), F32)
    zh = jnp.zeros((seq, ROPE_HALF), F32)
    zr = jnp.zeros((seq, HEAD_DIM - ROPE_DIM), F32)
    c = jnp.concatenate([cos, cos, ones], axis=1)
    s1 = jnp.concatenate([-sin, zh, zr], axis=1)
    s2 = jnp.concatenate([zh, sin, zr], axis=1)
    return c, s1, s2


def _sigmoid(x):
    return 1.0 / (1.0 + jnp.exp(-x))


def _silu(x):
    return x * _sigmoid(x)


def _dot_nt(a, b):
    return lax.dot_general(a, b, (((1,), (1,)), ((), ())), preferred_element_type=F32)


def _dot(a, b):
    return jnp.dot(a, b, preferred_element_type=F32)


def _params(*sem):
    return pltpu.CompilerParams(dimension_semantics=sem, vmem_limit_bytes=VMEM_LIMIT)


def _ada_kernel(c_ref, w_ref, b_ref, o_ref):
    c = c_ref[...]
    o_ref[0] = _dot(_silu(c).astype(BF16), w_ref[0].astype(BF16)) + b_ref[0]


def _ada(c, w_ada, b_ada):
    depth, d, n3 = w_ada.shape
    bsz = c.shape[0]
    rows = max(8, bsz)
    cp = jnp.zeros((rows, d), F32).at[:bsz].set(c)
    tn = math.gcd(1024, n3)
    out = pl.pallas_call(
        _ada_kernel,
        grid=(depth, n3 // tn),
        in_specs=[pl.BlockSpec((rows, d), lambda l, j: (0, 0)),
                  pl.BlockSpec((1, d, tn), lambda l, j: (l, 0, j)),
                  pl.BlockSpec((1, 1, tn), lambda l, j: (l, 0, j))],
        out_specs=pl.BlockSpec((1, rows, tn), lambda l, j: (l, 0, j)),
        out_shape=jax.ShapeDtypeStruct((depth, rows, n3), F32),
        compiler_params=_params("arbitrary", "arbitrary"),
        name="ada_mod",
    )(cp, w_ada, b_ada.reshape(depth, 1, n3))
    return out[:, :bsz]


def _modulated_norm(x, g, scale, shift):
    y = x * lax.rsqrt(jnp.mean(x * x, axis=-1, keepdims=True) + EPS) * g
    return (y * (1.0 + scale) + shift).astype(BF16)


def _prenorm_kernel(x_ref, g_ref, sc_ref, sh_ref, h_ref):
    h_ref[...] = _modulated_norm(x_ref[...], g_ref[...], sc_ref[0], sh_ref[0])


def _prenorm(xf, g, scale, shift, seq, tm):
    rows, d = xf.shape
    tpb = seq // tm
    mod_spec = pl.BlockSpec((1, 1, d), lambda i: (i // tpb, 0, 0))
    return pl.pallas_call(
        _prenorm_kernel,
        grid=(rows // tm,),
        in_specs=[pl.BlockSpec((tm, d), lambda i: (i, 0)), pl.BlockSpec((1, d), lambda i: (0, 0)),
                  mod_spec, mod_spec],
        out_specs=pl.BlockSpec((tm, d), lambda i: (i, 0)),
        out_shape=jax.ShapeDtypeStruct((rows, d), BF16),
        compiler_params=_params("arbitrary"),
        name="pre_norm",
    )(xf, g.reshape(1, d), scale[:, None, :], shift[:, None, :])


def _inproj_kernel(h_ref, c_ref, s1_ref, s2_ref, w_ref, o_ref, *, n_rope_tiles, tn):
    j = pl.program_id(1)
    y = _dot_nt(h_ref[...], w_ref[...])

    @pl.when(j < n_rope_tiles)
    def _():
        c, s1, s2 = c_ref[...], s1_ref[...], s2_ref[...]
        for hh in range(tn // HEAD_DIM):
            sl = slice(hh * HEAD_DIM, (hh + 1) * HEAD_DIM)
            yh = y[:, sl]
            r = (yh * c + pltpu.roll(yh, HEAD_DIM - ROPE_HALF, 1) * s1
                 + pltpu.roll(yh, ROPE_HALF, 1) * s2)
            col = j * tn + hh * HEAD_DIM
            is_q = ((col >= P_AQ) & (col < P_AQ + BR_WIDTH)) | ((col >= P_NQ) & (col < P_NQ + BR_WIDTH))
            o_ref[:, sl] = (r * jnp.where(is_q, Q_PRESCALE, 1.0)).astype(BF16)

    @pl.when(j >= n_rope_tiles)
    def _():
        o_ref[...] = y.astype(BF16)


def _inproj(h, tables, wp_all, layer, seq, tm, tn):
    rows, d = h.shape
    n_total = wp_all.shape[1]
    tpb = seq // tm
    c, s1, s2 = tables
    tab_spec = pl.BlockSpec((tm, HEAD_DIM), lambda i, j: (i % tpb, 0))
    return pl.pallas_call(
        functools.partial(_inproj_kernel, n_rope_tiles=ROPE_COLS // tn, tn=tn),
        grid=(rows // tm, n_total // tn),
        in_specs=[pl.BlockSpec((tm, d), lambda i, j: (i, 0)),
                  tab_spec, tab_spec, tab_spec,
                  pl.BlockSpec((None, tn, d), lambda i, j: (layer, j, 0))],
        out_specs=pl.BlockSpec((tm, tn), lambda i, j: (i, j)),
        out_shape=jax.ShapeDtypeStruct((rows, n_total), BF16),
        compiler_params=_params("arbitrary", "arbitrary"),
        name="in_proj",
    )(h, c, s1, s2, wp_all)


def _softmax_update(carry, s, v):
    m, l, acc = carry
    m_new = jnp.maximum(m, jnp.max(s, axis=-1, keepdims=True))
    alpha = jnp.exp2(m - m_new)
    p = jnp.exp2(s - m_new)
    l = alpha * l + jnp.sum(p, axis=-1, keepdims=True)
    acc = alpha * acc + _dot(p.astype(BF16), v)
    return m_new, l, acc


def _softmax_init(rows, width):
    return (jnp.full((rows, 1), NEG_INF, F32), jnp.zeros((rows, 1), F32),
            jnp.zeros((rows, width), F32))


def _diff_kernel(q_ref, k_ref, v_ref, z_ref, ng_ref, lq1_ref, lk1_ref, lq2_ref, lk2_ref, o_ref,
                 *, tq, tk, lam_init):
    i = pl.program_id(2)
    t0 = i * tq
    q = q_ref[...]
    qs = (q[:, :HEAD_DIM], q[:, HEAD_DIM:])
    n_full = t0 // tk

    def step(jt, carry, masked):
        k0 = pl.multiple_of(jt * tk, tk)
        k = k_ref[pl.ds(k0, tk), :]
        v = v_ref[pl.ds(k0, tk), :]
        s = [_dot_nt(qs[cc], k[:, cc * HEAD_DIM:(cc + 1) * HEAD_DIM]) for cc in range(2)]
        if masked:
            kpos = k0 + lax.broadcasted_iota(jnp.int32, (tq, tk), 1)
            tpos = t0 + lax.broadcasted_iota(jnp.int32, (tq, tk), 0)
            s = [jnp.where(kpos <= tpos, sc, NEG_INF) for sc in s]
        return tuple(_softmax_update(carry[cc], s[cc], v) for cc in range(2))

    init = (_softmax_init(tq, 2 * HEAD_DIM), _softmax_init(tq, 2 * HEAD_DIM))
    carry = lax.fori_loop(0, n_full, lambda jt, c: step(jt, c, False), init)
    carry = step(n_full, carry, True)

    lam = (jnp.exp(jnp.sum(lq1_ref[...] * lk1_ref[...], axis=-1, keepdims=True))
           - jnp.exp(jnp.sum(lq2_ref[...] * lk2_ref[...], axis=-1, keepdims=True)) + lam_init)
    o0 = carry[0][2] / carry[0][1]
    o1 = carry[1][2] / carry[1][1]
    o = o0 - lam * o1
    o = o * lax.rsqrt(jnp.mean(o * o, axis=-1, keepdims=True) + DIFF_NORM_EPS) * ng_ref[...]
    o = o * (1.0 - lam_init)
    o_ref[...] = (o * _silu(z_ref[...].astype(F32))).astype(BF16)


def _diff_attn(proj, offs, norm_g, lq1, lk1, lq2, lk2, layer_idx, bsz, seq, tq, tk):
    nq = seq // tq
    w = 2 * HEAD_DIM
    lam_init = 0.8 - 0.6 * math.exp(-0.3 * layer_idx)
    assert tk % tq == 0
    vec = pl.BlockSpec((1, HEAD_DIM), lambda b, h, i: (0, 0))
    return pl.pallas_call(
        functools.partial(_diff_kernel, tq=tq, tk=tk, lam_init=lam_init),
        grid=(bsz, DIFF_HEADS, nq),
        in_specs=[pl.BlockSpec((tq, w), lambda b, h, i: (b * nq + i, P_AQ // w + h)),
                  pl.BlockSpec((seq, w), lambda b, h, i: (b, P_AK // w + h)),
                  pl.BlockSpec((seq, w), lambda b, h, i: (b, offs["AV"] // w + h)),
                  pl.BlockSpec((tq, w), lambda b, h, i: (b * nq + i, offs["AZ"] // w + h)),
                  pl.BlockSpec((1, w), lambda b, h, i: (0, 0)),
                  vec, vec, vec, vec],
        out_specs=pl.BlockSpec((tq, w), lambda b, h, i: (b * nq + i, h)),
        out_shape=jax.ShapeDtypeStruct((bsz * seq, BR_WIDTH), BF16),
        compiler_params=_params("arbitrary", "arbitrary", "arbitrary"),
        name="diff_attn",
    )(proj, proj, proj, proj, norm_g.reshape(1, w), lq1.reshape(1, -1), lk1.reshape(1, -1),
      lq2.reshape(1, -1), lk2.reshape(1, -1))


def _sb_kernel(q_ref, k_ref, v_ref, z_ref, o_ref, *, tq, tk):
    i = pl.program_id(2)
    t0 = i * tq
    n_full = t0 // tk
    qh = (q_ref[pl.ds(0, tk), :], q_ref[pl.ds(tk, tk), :])
    upper2 = jnp.where((lax.broadcasted_iota(jnp.int32, (2 * tk, tk), 0) & (tk - 1))
                       > lax.broadcasted_iota(jnp.int32, (2 * tk, tk), 1), 1.0, 0.0).astype(BF16)
    strict = (lax.broadcasted_iota(jnp.int32, (tk, tk), 1) < lax.broadcasted_iota(jnp.int32, (tk, tk), 0))

    def steps(items):
        kv = []
        for _, jt, _, _ in items:
            k0 = pl.multiple_of(jt * tk, tk)
            kv.append((k_ref[pl.ds(k0, tk), :], v_ref[pl.ds(k0, tk), :]))
        z = [_dot_nt(qh[it[0]], kv[n][0]) * ATTN_SCALE for n, it in enumerate(items)]
        log_1m = [-(jnp.maximum(zz, 0.0) + jnp.log(1.0 + jnp.exp(-jnp.abs(zz)))) for zz in z]
        log_1m = [jnp.where(strict, lm, 0.0) if it[3] else lm for lm, it in zip(log_1m, items)]
        stacked = []
        for lm in log_1m:
            hi = lm.astype(BF16)
            stacked.append(jnp.concatenate([hi, (lm - hi.astype(F32)).astype(BF16)], axis=1))
        after = [_dot(st, upper2) + it[2][0] for st, it in zip(stacked, items)]
        a = [jnp.exp(zz + lm + af) for zz, lm, af in zip(z, log_1m, after)]
        a = [jnp.where(strict, aa, 0.0) if it[3] else aa for aa, it in zip(a, items)]
        pv = [_dot(aa.astype(BF16), kv[n][1]) for n, aa in enumerate(a)]
        return [(it[2][0] + jnp.sum(lm, axis=-1, keepdims=True), it[2][1] + p)
                for it, lm, p in zip(items, log_1m, pv)]

    zero = (jnp.zeros((tk, 1), F32), jnp.zeros((tk, HEAD_DIM), F32))
    c_b, c_a = steps([(1, n_full + 1, zero, True), (0, n_full, zero, True)])
    (c_b,) = steps([(1, n_full, c_b, False)])

    def alive(ca, cb):
        return jnp.maximum(jnp.max(ca[0]), jnp.max(cb[0]))

    def cond(c):
        return (c[0] < n_full) & (c[1] > SB_UNDERFLOW)

    def body(c):
        jt = n_full - 1 - c[0]
        ca, cb = steps([(0, jt, c[2], False), (1, jt, c[3], False)])
        return c[0] + 1, alive(ca, cb), ca, cb

    _, _, c_a, c_b = lax.while_loop(cond, body, (jnp.int32(0), alive(c_a, c_b), c_a, c_b))
    zg = z_ref[...].astype(F32)
    o_ref[pl.ds(0, tk), :] = (c_a[1] * _silu(zg[:tk])).astype(BF16)
    o_ref[pl.ds(tk, tk), :] = (c_b[1] * _silu(zg[tk:])).astype(BF16)


def _sb_attn(proj, offs, bsz, seq, tq, tk):
    nq = seq // tq
    w = HEAD_DIM
    assert tq == 2 * tk
    return pl.pallas_call(
        functools.partial(_sb_kernel, tq=tq, tk=tk),
        grid=(bsz, SB_HEADS, nq),
        in_specs=[pl.BlockSpec((tq, w), lambda b, h, i: (b * nq + i, offs["CQ"] // w + h)),
                  pl.BlockSpec((seq, w), lambda b, h, i: (b, offs["CK"] // w + h)),
                  pl.BlockSpec((seq, w), lambda b, h, i: (b, offs["CV"] // w + h)),
                  pl.BlockSpec((tq, w), lambda b, h, i: (b * nq + i, offs["CZ"] // w + h))],
        out_specs=pl.BlockSpec((tq, w), lambda b, h, i: (b * nq + i, h)),
        out_shape=jax.ShapeDtypeStruct((bsz * seq, BR_WIDTH), BF16),
        compiler_params=_params("arbitrary", "arbitrary", "arbitrary"),
        name="stick_breaking",
    )(proj, proj, proj, proj)


def _compress_kernel(x_ref, pe_ref, w1_ref, w2_ref, o_ref, xs_ref, *, seq):
    nc = seq // CMP_STRIDE
    xs_ref[pl.ds(0, seq), :] = x_ref[...].astype(F32)
    xs_ref[pl.ds(seq, CMP_STRIDE), :] = jnp.zeros((CMP_STRIDE, HEAD_DIM), F32)
    hid = None
    for l in range(CMP_LEN):
        rows = xs_ref[pl.ds(l, nc, stride=CMP_STRIDE), :] + pe_ref[l:l + 1, :]
        t = _dot(rows.astype(BF16), w1_ref[l * HEAD_DIM:(l + 1) * HEAD_DIM, :])
        hid = t if hid is None else hid + t
    o_ref[0, 0] = _dot(_silu(hid).astype(BF16), w2_ref[...]).astype(BF16)


def _compress(proj, col_off, pe, w1, w2, bsz, seq):
    nc = seq // CMP_STRIDE
    w = HEAD_DIM
    return pl.pallas_call(
        functools.partial(_compress_kernel, seq=seq),
        grid=(bsz, NSA_GROUPS),
        in_specs=[pl.BlockSpec((seq, w), lambda b, g: (b, col_off // w + g)),
                  pl.BlockSpec((CMP_LEN, w), lambda b, g: (0, 0)),
                  pl.BlockSpec((CMP_LEN * w, w), lambda b, g: (0, 0)),
                  pl.BlockSpec((w, w), lambda b, g: (0, 0))],
        out_specs=pl.BlockSpec((1, 1, nc, w), lambda b, g: (b, g, 0, 0)),
        out_shape=jax.ShapeDtypeStruct((bsz, NSA_GROUPS, nc, w), BF16),
        scratch_shapes=[pltpu.VMEM((seq + CMP_STRIDE, w), F32)],
        compiler_params=_params("arbitrary", "arbitrary"),
        name="nsa_compress",
    )(proj, pe, w1.astype(BF16), w2.astype(BF16))


def _nsa_kernel(q_ref, kc_ref, vc_ref, ks_ref, vs_ref, kw_ref, vw_ref, g_ref, z_ref, o_ref, imp_ref,
                *, tq, tk, seq):
    i = pl.program_id(2)
    t0 = i * tq
    nc = seq // CMP_STRIDE
    nb = seq // SLC_LEN
    nbp = SLC_BLOCKS_PAD
    rep = NSA_REP
    q = q_ref[...]
    qh = [q[:, r * HEAD_DIM:(r + 1) * HEAD_DIM] for r in range(rep)]

    kc, vc = kc_ref[0, 0], vc_ref[0, 0]
    tpos_c = t0 + lax.broadcasted_iota(jnp.int32, (tq, nc), 0)
    ncol = lax.broadcasted_iota(jnp.int32, (tq, nc), 1)
    cvalid = CMP_STRIDE * ncol + (CMP_LEN - 1) <= tpos_c
    psum = jnp.zeros((tq, nc), F32)
    o_cmp = []
    for r in range(rep):
        s = jnp.where(cvalid, _dot_nt(qh[r], kc), NEG_INF)
        e = jnp.where(cvalid, jnp.exp2(s - jnp.max(s, axis=-1, keepdims=True)), 0.0)
        den = jnp.sum(e, axis=-1, keepdims=True)
        p = e / jnp.where(den > 0.0, den, 1.0)
        psum = psum + p
        o_cmp.append(_dot(p.astype(BF16), vc))

    nb8 = imp_ref.shape[0]
    jrow = lax.broadcasted_iota(jnp.int32, (nb8, nc), 0)
    ncol2 = lax.broadcasted_iota(jnp.int32, (nb8, nc), 1)
    overlap = jnp.where((CMP_STRIDE * ncol2 < SLC_LEN * jrow + SLC_LEN)
                        & (CMP_STRIDE * ncol2 + CMP_LEN > SLC_LEN * jrow)
                        & (ncol2 < nc - 1) & (jrow < nb), 1.0, 0.0).astype(BF16)
    p1 = psum.astype(BF16)
    r1 = psum - p1.astype(F32)
    p2 = r1.astype(BF16)
    p3 = (r1 - p2.astype(F32)).astype(BF16)
    imp = _dot_nt(overlap, p1) + _dot_nt(overlap, p2) + _dot_nt(overlap, p3)
    jt_ = lax.broadcasted_iota(jnp.int32, (nb8, tq), 0)
    tblk = jnp.right_shift(t0 + lax.broadcasted_iota(jnp.int32, (nb8, tq), 1), SLC_SHIFT)
    imp = jnp.where((jt_ == tblk) | (jt_ == 0), FORCE_SCORE, imp)
    imp = jnp.where(jt_ <= tblk, imp, NEG_INF)
    imp_ref[...] = imp

    def rank_body(jp, cnt):
        row = imp_ref[pl.ds(jp, 1), :]
        beats = (row > imp) | ((row == imp) & (jp < jt_))
        return cnt + jnp.where(beats, 1.0, 0.0)

    n_live = jnp.minimum((t0 + tq - 1) // SLC_LEN + 1, nb)
    cnt = lax.fori_loop(0, n_live, rank_body, jnp.zeros((nb8, tq), F32))
    keep = (cnt < float(min(SLC_TOPK, nb))) & (imp > 0.5 * NEG_INF)
    unsel_t = jnp.where(keep, 0.0, 1.0)
    if nbp > nb8:
        unsel_t = jnp.concatenate([unsel_t, jnp.zeros((nbp - nb8, tq), F32)], axis=0)
    unselected = unsel_t.T.astype(BF16)

    rows = rep * tq
    q_sel = jnp.concatenate([jnp.concatenate([qh[r], unselected], axis=1) for r in range(rep)], axis=0)
    q_win = jnp.concatenate(qh, axis=0)

    def slc_step(jt, carry, masked):
        k0 = pl.multiple_of(jt * tk, tk)
        k = ks_ref[pl.ds(k0, tk), :]
        v = vs_ref[pl.ds(k0, tk), :]
        kblk = jnp.right_shift(k0 + lax.broadcasted_iota(jnp.int32, (tk, nbp), 0), SLC_SHIFT)
        bias = jnp.where(lax.broadcasted_iota(jnp.int32, (tk, nbp), 1) == kblk,
                         -UNSELECTED_BIAS, 0.0).astype(BF16)
        s = _dot_nt(q_sel, jnp.concatenate([k, bias], axis=1))
        if masked:
            causal = (k0 + lax.broadcasted_iota(jnp.int32, (tq, tk), 1)
                      <= t0 + lax.broadcasted_iota(jnp.int32, (tq, tk), 0))
            s = s + jnp.concatenate([jnp.where(causal, 0.0, NEG_INF)] * rep, axis=0)
        return _softmax_update(carry, s, v)

    n_full = t0 // tk
    carry = lax.fori_loop(0, n_full, lambda jt, c: slc_step(jt, c, False), _softmax_init(rows, HEAD_DIM))
    m_s, l_s, acc_s = slc_step(n_full, carry, True)
    o_slc = acc_s / l_s

    span = min(WINDOW + tq, seq)
    start = pl.multiple_of(jnp.maximum(t0 - WINDOW, 0), tq)
    kpos = start + lax.broadcasted_iota(jnp.int32, (tq, span), 1)
    row_w = t0 + lax.broadcasted_iota(jnp.int32, (tq, span), 0)
    in_window = jnp.where(kpos <= row_w, jnp.where(kpos > row_w - WINDOW, 0.0, NEG_INF), NEG_INF)
    s = _dot_nt(q_win, kw_ref[pl.ds(start, span), :]) + jnp.concatenate([in_window] * rep, axis=0)
    p = jnp.exp2(s - jnp.max(s, axis=-1, keepdims=True))
    o_win = _dot(p.astype(BF16), vw_ref[pl.ds(start, span), :]) / jnp.sum(p, axis=-1, keepdims=True)

    gates = _sigmoid(g_ref[...].astype(F32))
    z = z_ref[...].astype(F32)
    for r in range(rep):
        rs = slice(r * tq, (r + 1) * tq)
        cs = slice(r * HEAD_DIM, (r + 1) * HEAD_DIM)
        o = (gates[:, 3 * r:3 * r + 1] * o_cmp[r] + gates[:, 3 * r + 1:3 * r + 2] * o_slc[rs]
             + gates[:, 3 * r + 2:3 * r + 3] * o_win[rs])
        o_ref[:, cs] = (o * _silu(z[:, cs])).astype(BF16)


def _nsa_attn(proj, offs, kcmp, vcmp, bsz, seq, tq, tk):
    nq = seq // tq
    nc = seq // CMP_STRIDE
    w = HEAD_DIM
    gw = NSA_REP * HEAD_DIM
    assert seq // SLC_LEN <= SLC_BLOCKS_PAD and tq & (tq - 1) == 0 and tk % tq == 0
    kv = lambda off: pl.BlockSpec((seq, w), lambda b, g, i: (b, off // w + g))
    cmp_spec = pl.BlockSpec((1, 1, nc, w), lambda b, g, i: (b, g, 0, 0))
    return pl.pallas_call(
        functools.partial(_nsa_kernel, tq=tq, tk=tk, seq=seq),
        grid=(bsz, NSA_GROUPS, nq),
        in_specs=[pl.BlockSpec((tq, gw), lambda b, g, i: (b * nq + i, P_NQ // gw + g)),
                  cmp_spec, cmp_spec,
                  kv(P_NKS), kv(offs["NVS"]), kv(P_NKW), kv(offs["NVW"]),
                  pl.BlockSpec((tq, LANES), lambda b, g, i: (b * nq + i, offs["NG"] // LANES + g)),
                  pl.BlockSpec((tq, gw), lambda b, g, i: (b * nq + i, offs["NZ"] // gw + g))],
        out_specs=pl.BlockSpec((tq, gw), lambda b, g, i: (b * nq + i, g)),
        out_shape=jax.ShapeDtypeStruct((bsz * seq, BR_WIDTH), BF16),
        scratch_shapes=[pltpu.VMEM((-(-(seq // SLC_LEN) // 8) * 8, tq), F32)],
        compiler_params=_params("arbitrary", "arbitrary", "arbitrary"),
        name="nsa_attn",
    )(proj, kcmp, vcmp, proj, proj, proj, proj, proj, proj)


def _merge_kernel(x_ref, ya_ref, yb_ref, yc_ref, mg0_ref, mg1_ref, mg2_ref, wb_ref, wo_ref, gp_ref,
                  gate_ref, *rest, emit_h):
    merged = None
    for n, (y_ref, mg_ref) in enumerate(((ya_ref, mg0_ref), (yb_ref, mg1_ref), (yc_ref, mg2_ref))):
        t = _dot(y_ref[...], wb_ref[n]) * _sigmoid(mg_ref[...].astype(F32))
        merged = t if merged is None else merged + t
    o = _dot(merged.astype(BF16), wo_ref[...])
    o = o * lax.rsqrt(jnp.mean(o * o, axis=-1, keepdims=True) + EPS) * gp_ref[...]
    x_new = x_ref[...] + gate_ref[0] * o
    if emit_h:
        g_next_ref, sc_next_ref, sh_next_ref, o_ref, h_ref = rest
        h_ref[...] = _modulated_norm(x_new, g_next_ref[...], sc_next_ref[0], sh_next_ref[0])
    else:
        (o_ref,) = rest
    o_ref[...] = x_new


def _merge(xf, ya, yb, yc, proj, wb_all, wo_all, layer, g_post, gate, next_norm, seq, tm):
    rows, d = xf.shape
    tpb = seq // tm
    row = lambda w_: pl.BlockSpec((tm, w_), lambda i: (i, 0))
    mg = lambda n: pl.BlockSpec((tm, d), lambda i: (i, P_MG // d + n))
    vec = pl.BlockSpec((1, d), lambda i: (0, 0))
    mod_spec = pl.BlockSpec((1, 1, d), lambda i: (i // tpb, 0, 0))
    const = pl.Buffered(1)
    emit_h = next_norm is not None
    in_specs = [row(d), row(BR_WIDTH), row(BR_WIDTH), row(BR_WIDTH), mg(0), mg(1), mg(2),
                pl.BlockSpec((None, N_BRANCH, BR_WIDTH, d), lambda i: (layer, 0, 0, 0), pipeline_mode=const),
                pl.BlockSpec((None, d, d), lambda i: (layer, 0, 0), pipeline_mode=const), vec, mod_spec]
    args = [xf, ya, yb, yc, proj, proj, proj, wb_all, wo_all, g_post.reshape(1, d),
            gate[:, None, :]]
    out_specs, out_shape = row(d), jax.ShapeDtypeStruct((rows, d), F32)
    if emit_h:
        g_next, sc_next, sh_next = next_norm
        in_specs += [vec, mod_spec, mod_spec]
        args += [g_next.reshape(1, d), sc_next[:, None, :], sh_next[:, None, :]]
        out_specs, out_shape = [out_specs, row(d)], [out_shape, jax.ShapeDtypeStruct((rows, d), BF16)]
    out = pl.pallas_call(
        functools.partial(_merge_kernel, emit_h=emit_h),
        grid=(rows // tm,),
        in_specs=in_specs,
        out_specs=out_specs,
        out_shape=out_shape,
        compiler_params=_params("arbitrary"),
        name="merge_out",
    )(*args)
    return (out[0], out[1]) if emit_h else (out, None)


def _tiles(seq, d_model):
    return dict(tm_in=min(1024, seq), tn_in=2048, tm_merge=min(256, seq),
                diff=(min(1024, seq), min(1024, seq)),
                sb=(min(512, seq), min(512, seq) // 2),
                nsa=(min(512, seq), min(512, seq)))


def kernel(x, c, norm_pre_g, norm_post_g, w_ada, b_ada, w_in, lambda_q1, lambda_k1, lambda_q2,
           lambda_k2, diff_norm_g, cmp_pe_k, cmp_w1_k, cmp_w2_k, cmp_pe_v, cmp_w1_v, cmp_w2_v,
           w_branch, w_out):
    bsz, seq, d = x.shape
    depth = w_in.shape[0]
    t = _tiles(seq, d)
    offs = _rest_offsets(d)
    assert d % 512 == 0 and ROPE_COLS % d == 0 and seq % t["tm_in"] == 0 and seq % 128 == 0
    n_total = -(-offs["END"] // t["tn_in"]) * t["tn_in"]
    tables = _rope_tables(seq)
    mod = _ada(c, w_ada, b_ada)
    xf = x.reshape(bsz * seq, d)
    mods = [jnp.split(mod[l], 3, axis=-1) for l in range(depth)]
    h = _prenorm(xf, norm_pre_g[0], mods[0][1], mods[0][0], seq, t["tm_merge"])
    wb_all, wo_all = w_branch.astype(BF16), w_out.astype(BF16)
    wp_all = _prep_w_in(w_in, d, n_total)
    for l in range(depth):
        proj = _inproj(h, tables, wp_all, l, seq, t["tm_in"], t["tn_in"])
        ya = _diff_attn(proj, offs, diff_norm_g[l], lambda_q1[l], lambda_k1[l], lambda_q2[l],
                        lambda_k2[l], l, bsz, seq, *t["diff"])
        kcmp = _compress(proj, P_NKC, cmp_pe_k[l], cmp_w1_k[l], cmp_w2_k[l], bsz, seq)
        vcmp = _compress(proj, offs["NVC"], cmp_pe_v[l], cmp_w1_v[l], cmp_w2_v[l], bsz, seq)
        yb = _nsa_attn(proj, offs, kcmp, vcmp, bsz, seq, *t["nsa"])
        yc = _sb_attn(proj, offs, bsz, seq, *t["sb"])
        next_norm = (norm_pre_g[l + 1], mods[l + 1][1], mods[l + 1][0]) if l + 1 < depth else None
        xf, h = _merge(xf, ya, yb, yc, proj, wb_all, wo_all, l, norm_post_g[l], mods[l][2],
                       next_norm, seq, t["tm_merge"])
    return xf.reshape(bsz, seq, d)
```

```python
import functools
import math

import jax
import jax.numpy as jnp
import numpy as np
from jax import lax
from jax.experimental import pallas as pl
from jax.experimental.pallas import tpu as pltpu

F32 = jnp.float32
BF16 = jnp.bfloat16

HEAD_DIM = 128
BR_WIDTH = 1024
N_BRANCH = 3
ROPE_THETA = 500000.0
ROPE_DIM = HEAD_DIM // 4
ROPE_HALF = ROPE_DIM // 2
EPS = 1e-6
NEG_INF = -1e30
FORCE_SCORE = 1e6
DIFF_HEADS = BR_WIDTH // (2 * HEAD_DIM)
DIFF_NORM_EPS = 1e-5
NSA_HEADS = BR_WIDTH // HEAD_DIM
NSA_GROUPS = 2
NSA_REP = NSA_HEADS // NSA_GROUPS
CMP_LEN = 32
CMP_STRIDE = 16
SLC_LEN = 64
SLC_SHIFT = SLC_LEN.bit_length() - 1
SLC_TOPK = 16
WINDOW = 512
SB_HEADS = BR_WIDTH // HEAD_DIM
NSA_KV = NSA_GROUPS * HEAD_DIM
ATTN_SCALE = HEAD_DIM ** -0.5
Q_PRESCALE = ATTN_SCALE * math.log2(math.e)

LANES = 128
SLC_BLOCKS_PAD = 128
UNSELECTED_BIAS = 2.0 ** 30
SB_UNDERFLOW = -104.0
VMEM_LIMIT = 56 * 1024 * 1024

_O_AQ, _O_AK, _O_AV, _O_AZ = 0, 1024, 2048, 3072
_O_NQ, _O_NKC, _O_NVC, _O_NKS, _O_NVS, _O_NKW, _O_NVW = 4096, 5120, 5376, 5632, 5888, 6144, 6400
_O_NG, _O_NZ = 6656, 6680
_O_CQ, _O_CK, _O_CV, _O_CZ, _O_MG = 7704, 8728, 9752, 10776, 11800
N_GATES = 3 * NSA_HEADS
P_AQ, P_AK, P_NQ, P_NKC, P_NKS, P_NKW = 0, 1024, 2048, 3072, 3328, 3584
ROPE_COLS = 4096
ROPE_PERM = (list(range(0, ROPE_HALF)) + list(range(ROPE_DIM, HEAD_DIM // 2 + ROPE_HALF))
             + list(range(ROPE_HALF, ROPE_DIM)) + list(range(HEAD_DIM // 2 + ROPE_HALF, HEAD_DIM)))
P_MG = ROPE_COLS


def _rest_offsets(d_model):
    r0 = ROPE_COLS + N_BRANCH * d_model
    offs = dict(AV=r0, AZ=r0 + 1024, NZ=r0 + 2048, CQ=r0 + 3072, CK=r0 + 4096, CV=r0 + 5120,
                CZ=r0 + 6144, NVC=r0 + 7168, NVS=r0 + 7424, NVW=r0 + 7680, NG=r0 + 7936)
    offs["END"] = r0 + 7936 + NSA_GROUPS * LANES
    return offs


PREP_ROWS = 2 * LANES
GATES_PER_GROUP = N_GATES // NSA_GROUPS
_ZERO, _COPY, _GATES, _COPY_ROPE = 0, 1, 2, 3


def _prep_plan(d_model, n_total):
    offs = _rest_offsets(d_model)
    pieces = [(P_AQ, _O_AQ, 1024), (P_AK, _O_AK, 1024), (P_NQ, _O_NQ, 1024), (P_NKC, _O_NKC, 256),
              (P_NKS, _O_NKS, 256), (P_NKW, _O_NKW, 256), (P_MG, _O_MG, N_BRANCH * d_model),
              (offs["AV"], _O_AV, 1024), (offs["AZ"], _O_AZ, 1024), (offs["NZ"], _O_NZ, 1024),
              (offs["CQ"], _O_CQ, 4096), (offs["NVC"], _O_NVC, 256), (offs["NVS"], _O_NVS, 256),
              (offs["NVW"], _O_NVW, 256), (offs["NG"], _O_NG, PREP_ROWS)]
    n_tiles = n_total // PREP_ROWS
    kind = np.full((n_tiles,), _ZERO, np.int32)
    src = np.zeros((n_tiles,), np.int32)
    for dest, first, width in pieces:
        for off in range(0, width, PREP_ROWS):
            assert (dest + off) % PREP_ROWS == 0 and (first + off) % 8 == 0
            kind[(dest + off) // PREP_ROWS] = _COPY_ROPE if dest + off < ROPE_COLS else _COPY
            src[(dest + off) // PREP_ROWS] = first + off
    assert NSA_GROUPS * LANES == PREP_ROWS
    kind[offs["NG"] // PREP_ROWS] = _GATES
    return jnp.asarray(kind), jnp.asarray(src // 8)


def _prep_kernel(kind_ref, src_ref, a_ref, o_ref):
    kind = kind_ref[pl.program_id(1)]

    @pl.when(kind == _ZERO)
    def _():
        o_ref[...] = jnp.zeros(o_ref.shape, BF16)

    @pl.when(kind == _COPY)
    def _():
        o_ref[...] = a_ref[0].astype(BF16)

    @pl.when(kind == _COPY_ROPE)
    def _():
        runs = ((0, 0, ROPE_HALF), (ROPE_HALF, ROPE_DIM, HEAD_DIM // 2 - ROPE_HALF),
                (HEAD_DIM // 2, ROPE_HALF, ROPE_HALF), (HEAD_DIM // 2 + ROPE_HALF, HEAD_DIM // 2 + ROPE_HALF,
                                                       HEAD_DIM // 2 - ROPE_HALF))
        for head in range(PREP_ROWS // HEAD_DIM):
            for dst, src, n in runs:
                o_ref[pl.ds(head * HEAD_DIM + dst, n), :] = a_ref[0, pl.ds(head * HEAD_DIM + src, n), :].astype(BF16)

    @pl.when(kind == _GATES)
    def _():
        o_ref[...] = jnp.zeros(o_ref.shape, BF16)
        for g in range(NSA_GROUPS):
            o_ref[pl.ds(g * LANES, GATES_PER_GROUP), :] = (
                a_ref[0, pl.ds(g * GATES_PER_GROUP, GATES_PER_GROUP), :].astype(BF16))


def _prep_w_in(w_in, d_model, n_total):
    wt = jnp.swapaxes(w_in, 1, 2)
    depth, n_in, d = wt.shape
    kind, src = _prep_plan(d_model, n_total)
    grid_spec = pltpu.PrefetchScalarGridSpec(
        num_scalar_prefetch=2,
        grid=(depth, n_total // PREP_ROWS),
        in_specs=[pl.BlockSpec((pl.Element(1), pl.Element(PREP_ROWS), pl.Element(d)),
                               lambda l, j, kind, src: (l, src[j] * 8, 0))],
        out_specs=pl.BlockSpec((None, PREP_ROWS, d), lambda l, j, kind, src: (l, j, 0)),
    )
    return pl.pallas_call(
        _prep_kernel,
        grid_spec=grid_spec,
        out_shape=jax.ShapeDtypeStruct((depth, n_total, d), BF16),
        compiler_params=_params("arbitrary", "arbitrary"),
        name="w_in_prep",
    )(kind, src, wt)


def _rope_tables(seq):
    pos = jnp.arange(seq, dtype=F32)
    inv = ROPE_THETA ** (-jnp.arange(0, ROPE_DIM, 2, dtype=F32) / ROPE_DIM)
    ang = pos[:, None] * inv[None, :]
    cos, sin = jnp.cos(ang), jnp.sin(ang)
    ones = jnp.ones((seq, HEAD_DIM // 2 - ROPE_HALF), F32)
    zeros = jnp.zeros((seq, HEAD_DIM // 2 - ROPE_HALF), F32)
    c = jnp.concatenate([cos, ones, cos, ones], axis=1)
    s = jnp.concatenate([-sin, zeros, sin, zeros], axis=1)
    return c, s


def _sigmoid(x):
    return 1.0 / (1.0 + jnp.exp(-x))


def _silu(x):
    return x * _sigmoid(x)


def _dot_nt(a, b):
    return lax.dot_general(a, b, (((1,), (1,)), ((), ())), preferred_element_type=F32)


def _dot(a, b):
    return jnp.dot(a, b, preferred_element_type=F32)


def _params(*sem):
    return pltpu.CompilerParams(dimension_semantics=sem, vmem_limit_bytes=VMEM_LIMIT)


def _ada_kernel(c_ref, w_ref, b_ref, o_ref):
    c = c_ref[...]
    o_ref[0] = _dot(_silu(c).astype(BF16), w_ref[0].astype(BF16)) + b_ref[0]


def _ada(c, w_ada, b_ada):
    depth, d, n3 = w_ada.shape
    bsz = c.shape[0]
    rows = max(8, bsz)
    cp = jnp.zeros((rows, d), F32).at[:bsz].set(c)
    tn = math.gcd(1024, n3)
    out = pl.pallas_call(
        _ada_kernel,
        grid=(depth, n3 // tn),
        in_specs=[pl.BlockSpec((rows, d), lambda l, j: (0, 0)),
                  pl.BlockSpec((1, d, tn), lambda l, j: (l, 0, j)),
                  pl.BlockSpec((1, 1, tn), lambda l, j: (l, 0, j))],
        out_specs=pl.BlockSpec((1, rows, tn), lambda l, j: (l, 0, j)),
        out_shape=jax.ShapeDtypeStruct((depth, rows, n3), F32),
        compiler_params=_params("arbitrary", "arbitrary"),
        name="ada_mod",
    )(cp, w_ada, b_ada.reshape(depth, 1, n3))
    return out[:, :bsz]


def _modulated_norm(x, g, scale, shift):
    y = x * lax.rsqrt(jnp.mean(x * x, axis=-1, keepdims=True) + EPS) * g
    return (y * (1.0 + scale) + shift).astype(BF16)


def _prenorm_kernel(x_ref, g_ref, sc_ref, sh_ref, h_ref):
    h_ref[...] = _modulated_norm(x_ref[...], g_ref[...], sc_ref[0], sh_ref[0])


def _prenorm(xf, g, scale, shift, seq, tm):
    rows, d = xf.shape
    tpb = seq // tm
    mod_spec = pl.BlockSpec((1, 1, d), lambda i: (i // tpb, 0, 0))
    return pl.pallas_call(
        _prenorm_kernel,
        grid=(rows // tm,),
        in_specs=[pl.BlockSpec((tm, d), lambda i: (i, 0)), pl.BlockSpec((1, d), lambda i: (0, 0)),
                  mod_spec, mod_spec],
        out_specs=pl.BlockSpec((tm, d), lambda i: (i, 0)),
        out_shape=jax.ShapeDtypeStruct((rows, d), BF16),
        compiler_params=_params("arbitrary"),
        name="pre_norm",
    )(xf, g.reshape(1, d), scale[:, None, :], shift[:, None, :])


def _inproj_kernel(h_ref, c_ref, s_ref, w_ref, o_ref, *, n_rope_tiles, tn):
    j = pl.program_id(1)
    y = _dot_nt(h_ref[...], w_ref[...])

    @pl.when(j < n_rope_tiles)
    def _():
        c, s = c_ref[...], s_ref[...]
        for hh in range(tn // HEAD_DIM):
            sl = slice(hh * HEAD_DIM, (hh + 1) * HEAD_DIM)
            yh = y[:, sl]
            r = yh * c + pltpu.roll(yh, HEAD_DIM // 2, 1) * s
            col = j * tn + hh * HEAD_DIM
            is_q = ((col >= P_AQ) & (col < P_AQ + BR_WIDTH)) | ((col >= P_NQ) & (col < P_NQ + BR_WIDTH))
            o_ref[:, sl] = (r * jnp.where(is_q, Q_PRESCALE, 1.0)).astype(BF16)

    @pl.when(j >= n_rope_tiles)
    def _():
        o_ref[...] = y.astype(BF16)


def _inproj(h, tables, wp_all, layer, seq, tm, tn):
    rows, d = h.shape
    n_total = wp_all.shape[1]
    tpb = seq // tm
    c, s = tables
    tab_spec = pl.BlockSpec((tm, HEAD_DIM), lambda i, j: (i % tpb, 0))
    return pl.pallas_call(
        functools.partial(_inproj_kernel, n_rope_tiles=ROPE_COLS // tn, tn=tn),
        grid=(rows // tm, n_total // tn),
        in_specs=[pl.BlockSpec((tm, d), lambda i, j: (i, 0)),
                  tab_spec, tab_spec,
                  pl.BlockSpec((None, tn, d), lambda i, j: (layer, j, 0))],
        out_specs=pl.BlockSpec((tm, tn), lambda i, j: (i, j)),
        out_shape=jax.ShapeDtypeStruct((rows, n_total), BF16),
        compiler_params=_params("arbitrary", "arbitrary"),
        name="in_proj",
    )(h, c, s, wp_all)


def _softmax_update(carry, s, v):
    m, l, acc = carry
    m_new = jnp.maximum(m, jnp.max(s, axis=-1, keepdims=True))
    alpha = jnp.exp2(m - m_new)
    p = jnp.exp2(s - m_new)
    l = alpha * l + jnp.sum(p, axis=-1, keepdims=True)
    acc = alpha * acc + _dot(p.astype(BF16), v)
    return m_new, l, acc


def _softmax_init(rows, width):
    return (jnp.full((rows, 1), NEG_INF, F32), jnp.zeros((rows, 1), F32),
            jnp.zeros((rows, width), F32))


def _diff_kernel(q_ref, k_ref, v_ref, z_ref, ng_ref, lq1_ref, lk1_ref, lq2_ref, lk2_ref, o_ref,
                 *, tq, tk, lam_init):
    i = pl.program_id(2)
    t0 = i * tq
    q = q_ref[...]
    qs = (q[:, :HEAD_DIM], q[:, HEAD_DIM:])
    n_full = t0 // tk

    def step(jt, carry, masked):
        k0 = pl.multiple_of(jt * tk, tk)
        k = k_ref[pl.ds(k0, tk), :]
        v = v_ref[pl.ds(k0, tk), :]
        s = [_dot_nt(qs[cc], k[:, cc * HEAD_DIM:(cc + 1) * HEAD_DIM]) for cc in range(2)]
        if masked:
            kpos = k0 + lax.broadcasted_iota(jnp.int32, (tq, tk), 1)
            tpos = t0 + lax.broadcasted_iota(jnp.int32, (tq, tk), 0)
            s = [jnp.where(kpos <= tpos, sc, NEG_INF) for sc in s]
        return tuple(_softmax_update(carry[cc], s[cc], v) for cc in range(2))

    init = (_softmax_init(tq, 2 * HEAD_DIM), _softmax_init(tq, 2 * HEAD_DIM))
    carry = lax.fori_loop(0, n_full, lambda jt, c: step(jt, c, False), init)
    carry = step(n_full, carry, True)

    lam = (jnp.exp(jnp.sum(lq1_ref[...] * lk1_ref[...], axis=-1, keepdims=True))
           - jnp.exp(jnp.sum(lq2_ref[...] * lk2_ref[...], axis=-1, keepdims=True)) + lam_init)
    o0 = carry[0][2] / carry[0][1]
    o1 = carry[1][2] / carry[1][1]
    o = o0 - lam * o1
    o = o * lax.rsqrt(jnp.mean(o * o, axis=-1, keepdims=True) + DIFF_NORM_EPS) * ng_ref[...]
    o = o * (1.0 - lam_init)
    o_ref[...] = (o * _silu(z_ref[...].astype(F32))).astype(BF16)


def _diff_attn(proj, offs, norm_g, lq1, lk1, lq2, lk2, layer_idx, bsz, seq, tq, tk):
    nq = seq // tq
    w = 2 * HEAD_DIM
    lam_init = 0.8 - 0.6 * math.exp(-0.3 * layer_idx)
    assert tk % tq == 0
    vec = pl.BlockSpec((1, HEAD_DIM), lambda b, h, i: (0, 0))
    return pl.pallas_call(
        functools.partial(_diff_kernel, tq=tq, tk=tk, lam_init=lam_init),
        grid=(bsz, DIFF_HEADS, nq),
        in_specs=[pl.BlockSpec((tq, w), lambda b, h, i: (b * nq + i, P_AQ // w + h)),
                  pl.BlockSpec((seq, w), lambda b, h, i: (b, P_AK // w + h)),
                  pl.BlockSpec((seq, w), lambda b, h, i: (b, offs["AV"] // w + h)),
                  pl.BlockSpec((tq, w), lambda b, h, i: (b * nq + i, offs["AZ"] // w + h)),
                  pl.BlockSpec((1, w), lambda b, h, i: (0, 0)),
                  vec, vec, vec, vec],
        out_specs=pl.BlockSpec((tq, w), lambda b, h, i: (b * nq + i, h)),
        out_shape=jax.ShapeDtypeStruct((bsz * seq, BR_WIDTH), BF16),
        compiler_params=_params("arbitrary", "arbitrary", "arbitrary"),
        name="diff_attn",
    )(proj, proj, proj, proj, norm_g.reshape(1, w), lq1.reshape(1, -1), lk1.reshape(1, -1),
      lq2.reshape(1, -1), lk2.reshape(1, -1))


def _sb_kernel(q_ref, k_ref, v_ref, z_ref, o_ref, *, tq, tk):
    i = pl.program_id(2)
    t0 = i * tq
    n_full = t0 // tk
    qh = (q_ref[pl.ds(0, tk), :], q_ref[pl.ds(tk, tk), :])
    upper2 = jnp.where((lax.broadcasted_iota(jnp.int32, (2 * tk, tk), 0) & (tk - 1))
                       > lax.broadcasted_iota(jnp.int32, (2 * tk, tk), 1), 1.0, 0.0).astype(BF16)
    strict = (lax.broadcasted_iota(jnp.int32, (tk, tk), 1) < lax.broadcasted_iota(jnp.int32, (tk, tk), 0))

    def steps(items):
        kv = []
        for _, jt, _, _ in items:
            k0 = pl.multiple_of(jt * tk, tk)
            kv.append((k_ref[pl.ds(k0, tk), :], v_ref[pl.ds(k0, tk), :]))
        z = [_dot_nt(qh[it[0]], kv[n][0]) * ATTN_SCALE for n, it in enumerate(items)]
        log_1m = [-(jnp.maximum(zz, 0.0) + jnp.log(1.0 + jnp.exp(-jnp.abs(zz)))) for zz in z]
        log_1m = [jnp.where(strict, lm, 0.0) if it[3] else lm for lm, it in zip(log_1m, items)]
        stacked = []
        for lm in log_1m:
            hi = lm.astype(BF16)
            stacked.append(jnp.concatenate([hi, (lm - hi.astype(F32)).astype(BF16)], axis=1))
        after = [_dot(st, upper2) + it[2][0] for st, it in zip(stacked, items)]
        a = [jnp.exp(zz + lm + af) for zz, lm, af in zip(z, log_1m, after)]
        a = [jnp.where(strict, aa, 0.0) if it[3] else aa for aa, it in zip(a, items)]
        pv = [_dot(aa.astype(BF16), kv[n][1]) for n, aa in enumerate(a)]
        return [(it[2][0] + jnp.sum(lm, axis=-1, keepdims=True), it[2][1] + p)
                for it, lm, p in zip(items, log_1m, pv)]

    zero = (jnp.zeros((tk, 1), F32), jnp.zeros((tk, HEAD_DIM), F32))
    c_b, c_a = steps([(1, n_full + 1, zero, True), (0, n_full, zero, True)])
    (c_b,) = steps([(1, n_full, c_b, False)])

    def alive(ca, cb):
        return jnp.maximum(jnp.max(ca[0]), jnp.max(cb[0]))

    def cond(c):
        return (c[0] < n_full) & (c[1] > SB_UNDERFLOW)

    def body(c):
        jt = n_full - 1 - c[0]
        ca, cb = steps([(0, jt, c[2], False), (1, jt, c[3], False)])
        return c[0] + 1, alive(ca, cb), ca, cb

    _, _, c_a, c_b = lax.while_loop(cond, body, (jnp.int32(0), alive(c_a, c_b), c_a, c_b))
    zg = z_ref[...].astype(F32)
    o_ref[pl.ds(0, tk), :] = (c_a[1] * _silu(zg[:tk])).astype(BF16)
    o_ref[pl.ds(tk, tk), :] = (c_b[1] * _silu(zg[tk:])).astype(BF16)


def _sb_attn(proj, offs, bsz, seq, tq, tk):
    nq = seq // tq
    w = HEAD_DIM
    assert tq == 2 * tk
    return pl.pallas_call(
        functools.partial(_sb_kernel, tq=tq, tk=tk),
        grid=(bsz, SB_HEADS, nq),
        in_specs=[pl.BlockSpec((tq, w), lambda b, h, i: (b * nq + i, offs["CQ"] // w + h)),
                  pl.BlockSpec((seq, w), lambda b, h, i: (b, offs["CK"] // w + h)),
                  pl.BlockSpec((seq, w), lambda b, h, i: (b, offs["CV"] // w + h)),
                  pl.BlockSpec((tq, w), lambda b, h, i: (b * nq + i, offs["CZ"] // w + h))],
        out_specs=pl.BlockSpec((tq, w), lambda b, h, i: (b * nq + i, h)),
        out_shape=jax.ShapeDtypeStruct((bsz * seq, BR_WIDTH), BF16),
        compiler_params=_params("arbitrary", "arbitrary", "arbitrary"),
        name="stick_breaking",
    )(proj, proj, proj, proj)


def _compress_kernel(x_ref, pe_ref, w1_ref, w2_ref, o_ref, xs_ref, *, seq):
    nc = seq // CMP_STRIDE
    xs_ref[pl.ds(0, seq), :] = x_ref[...].astype(F32)
    xs_ref[pl.ds(seq, CMP_STRIDE), :] = jnp.zeros((CMP_STRIDE, HEAD_DIM), F32)
    hid = None
    for l in range(CMP_LEN):
        rows = xs_ref[pl.ds(l, nc, stride=CMP_STRIDE), :] + pe_ref[l:l + 1, :]
        t = _dot(rows.astype(BF16), w1_ref[l * HEAD_DIM:(l + 1) * HEAD_DIM, :])
        hid = t if hid is None else hid + t
    o_ref[0, 0] = _dot(_silu(hid).astype(BF16), w2_ref[...]).astype(BF16)


def _compress(proj, col_off, pe, w1, w2, bsz, seq):
    nc = seq // CMP_STRIDE
    w = HEAD_DIM
    return pl.pallas_call(
        functools.partial(_compress_kernel, seq=seq),
        grid=(bsz, NSA_GROUPS),
        in_specs=[pl.BlockSpec((seq, w), lambda b, g: (b, col_off // w + g)),
                  pl.BlockSpec((CMP_LEN, w), lambda b, g: (0, 0)),
                  pl.BlockSpec((CMP_LEN * w, w), lambda b, g: (0, 0)),
                  pl.BlockSpec((w, w), lambda b, g: (0, 0))],
        out_specs=pl.BlockSpec((1, 1, nc, w), lambda b, g: (b, g, 0, 0)),
        out_shape=jax.ShapeDtypeStruct((bsz, NSA_GROUPS, nc, w), BF16),
        scratch_shapes=[pltpu.VMEM((seq + CMP_STRIDE, w), F32)],
        compiler_params=_params("arbitrary", "arbitrary"),
        name="nsa_compress",
    )(proj, pe, w1.astype(BF16), w2.astype(BF16))


def _nsa_kernel(q_ref, kc_ref, vc_ref, ks_ref, vs_ref, kw_ref, vw_ref, g_ref, z_ref, o_ref, imp_ref,
                *, tq, tk, seq):
    i = pl.program_id(2)
    t0 = i * tq
    nc = seq // CMP_STRIDE
    nb = seq // SLC_LEN
    nbp = SLC_BLOCKS_PAD
    rep = NSA_REP
    q = q_ref[...]
    qh = [q[:, r * HEAD_DIM:(r + 1) * HEAD_DIM] for r in range(rep)]

    kc, vc = kc_ref[0, 0], vc_ref[0, 0]
    tpos_c = t0 + lax.broadcasted_iota(jnp.int32, (tq, nc), 0)
    ncol = lax.broadcasted_iota(jnp.int32, (tq, nc), 1)
    cvalid = CMP_STRIDE * ncol + (CMP_LEN - 1) <= tpos_c
    psum = jnp.zeros((tq, nc), F32)
    o_cmp = []
    for r in range(rep):
        s = jnp.where(cvalid, _dot_nt(qh[r], kc), NEG_INF)
        e = jnp.where(cvalid, jnp.exp2(s - jnp.max(s, axis=-1, keepdims=True)), 0.0)
        den = jnp.sum(e, axis=-1, keepdims=True)
        p = e / jnp.where(den > 0.0, den, 1.0)
        psum = psum + p
        o_cmp.append(_dot(p.astype(BF16), vc))

    nb8 = imp_ref.shape[0]
    jrow = lax.broadcasted_iota(jnp.int32, (nb8, nc), 0)
    ncol2 = lax.broadcasted_iota(jnp.int32, (nb8, nc), 1)
    overlap = jnp.where((CMP_STRIDE * ncol2 < SLC_LEN * jrow + SLC_LEN)
                        & (CMP_STRIDE * ncol2 + CMP_LEN > SLC_LEN * jrow)
                        & (ncol2 < nc - 1) & (jrow < nb), 1.0, 0.0).astype(BF16)
    p1 = psum.astype(BF16)
    r1 = psum - p1.astype(F32)
    p2 = r1.astype(BF16)
    p3 = (r1 - p2.astype(F32)).astype(BF16)
    imp = _dot_nt(overlap, p1) + _dot_nt(overlap, p2) + _dot_nt(overlap, p3)
    jt_ = lax.broadcasted_iota(jnp.int32, (nb8, tq), 0)
    tblk = jnp.right_shift(t0 + lax.broadcasted_iota(jnp.int32, (nb8, tq), 1), SLC_SHIFT)
    imp = jnp.where((jt_ == tblk) | (jt_ == 0), FORCE_SCORE, imp)
    imp = jnp.where(jt_ <= tblk, imp, NEG_INF)
    imp_ref[...] = imp

    def rank_body(jp, cnt):
        row = imp_ref[pl.ds(jp, 1), :]
        beats = (row > imp) | ((row == imp) & (jp < jt_))
        return cnt + jnp.where(beats, 1.0, 0.0)

    n_live = jnp.minimum((t0 + tq - 1) // SLC_LEN + 1, nb)
    cnt = lax.fori_loop(0, n_live, rank_body, jnp.zeros((nb8, tq), F32))
    keep = (cnt < float(min(SLC_TOPK, nb))) & (imp > 0.5 * NEG_INF)
    unsel_t = jnp.where(keep, 0.0, 1.0)
    if nbp > nb8:
        unsel_t = jnp.concatenate([unsel_t, jnp.zeros((nbp - nb8, tq), F32)], axis=0)
    unselected = unsel_t.T.astype(BF16)

    rows = rep * tq
    q_sel = jnp.concatenate([jnp.concatenate([qh[r], unselected], axis=1) for r in range(rep)], axis=0)
    q_win = jnp.concatenate(qh, axis=0)

    def slc_step(jt, carry, masked):
        k0 = pl.multiple_of(jt * tk, tk)
        k = ks_ref[pl.ds(k0, tk), :]
        v = vs_ref[pl.ds(k0, tk), :]
        kblk = jnp.right_shift(k0 + lax.broadcasted_iota(jnp.int32, (tk, nbp), 0), SLC_SHIFT)
        bias = jnp.where(lax.broadcasted_iota(jnp.int32, (tk, nbp), 1) == kblk,
                         -UNSELECTED_BIAS, 0.0).astype(BF16)
        s = _dot_nt(q_sel, jnp.concatenate([k, bias], axis=1))
        if masked:
            causal = (k0 + lax.broadcasted_iota(jnp.int32, (tq, tk), 1)
                      <= t0 + lax.broadcasted_iota(jnp.int32, (tq, tk), 0))
            s = s + jnp.concatenate([jnp.where(causal, 0.0, NEG_INF)] * rep, axis=0)
        return _softmax_update(carry, s, v)

    n_full = t0 // tk
    carry = lax.fori_loop(0, n_full, lambda jt, c: slc_step(jt, c, False), _softmax_init(rows, HEAD_DIM))
    m_s, l_s, acc_s = slc_step(n_full, carry, True)
    o_slc = acc_s / l_s

    span = min(WINDOW + tq, seq)
    start = pl.multiple_of(jnp.maximum(t0 - WINDOW, 0), tq)
    kpos = start + lax.broadcasted_iota(jnp.int32, (tq, span), 1)
    row_w = t0 + lax.broadcasted_iota(jnp.int32, (tq, span), 0)
    in_window = jnp.where(kpos <= row_w, jnp.where(kpos > row_w - WINDOW, 0.0, NEG_INF), NEG_INF)
    s = _dot_nt(q_win, kw_ref[pl.ds(start, span), :]) + jnp.concatenate([in_window] * rep, axis=0)
    p = jnp.exp2(s - jnp.max(s, axis=-1, keepdims=True))
    o_win = _dot(p.astype(BF16), vw_ref[pl.ds(start, span), :]) / jnp.sum(p, axis=-1, keepdims=True)

    gates = _sigmoid(g_ref[...].astype(F32))
    z = z_ref[...].astype(F32)
    for r in range(rep):
        rs = slice(r * tq, (r + 1) * tq)
        cs = slice(r * HEAD_DIM, (r + 1) * HEAD_DIM)
        o = (gates[:, 3 * r:3 * r + 1] * o_cmp[r] + gates[:, 3 * r + 1:3 * r + 2] * o_slc[rs]
             + gates[:, 3 * r + 2:3 * r + 3] * o_win[rs])
        o_ref[:, cs] = (o * _silu(z[:, cs])).astype(BF16)


def _nsa_attn(proj, offs, kcmp, vcmp, bsz, seq, tq, tk):
    nq = seq // tq
    nc = seq // CMP_STRIDE
    w = HEAD_DIM
    gw = NSA_REP * HEAD_DIM
    assert seq // SLC_LEN <= SLC_BLOCKS_PAD and tq & (tq - 1) == 0 and tk % tq == 0
    kv = lambda off: pl.BlockSpec((seq, w), lambda b, g, i: (b, off // w + g))
    cmp_spec = pl.BlockSpec((1, 1, nc, w), lambda b, g, i: (b, g, 0, 0))
    return pl.pallas_call(
        functools.partial(_nsa_kernel, tq=tq, tk=tk, seq=seq),
        grid=(bsz, NSA_GROUPS, nq),
        in_specs=[pl.BlockSpec((tq, gw), lambda b, g, i: (b * nq + i, P_NQ // gw + g)),
                  cmp_spec, cmp_spec,
                  kv(P_NKS), kv(offs["NVS"]), kv(P_NKW), kv(offs["NVW"]),
                  pl.BlockSpec((tq, LANES), lambda b, g, i: (b * nq + i, offs["NG"] // LANES + g)),
                  pl.BlockSpec((tq, gw), lambda b, g, i: (b * nq + i, offs["NZ"] // gw + g))],
        out_specs=pl.BlockSpec((tq, gw), lambda b, g, i: (b * nq + i, g)),
        out_shape=jax.ShapeDtypeStruct((bsz * seq, BR_WIDTH), BF16),
        scratch_shapes=[pltpu.VMEM((-(-(seq // SLC_LEN) // 8) * 8, tq), F32)],
        compiler_params=_params("arbitrary", "arbitrary", "arbitrary"),
        name="nsa_attn",
    )(proj, kcmp, vcmp, proj, proj, proj, proj, proj, proj)


def _merge_kernel(x_ref, ya_ref, yb_ref, yc_ref, mg0_ref, mg1_ref, mg2_ref, wb_ref, wo_ref, gp_ref,
                  gate_ref, *rest, emit_h):
    merged = None
    for n, (y_ref, mg_ref) in enumerate(((ya_ref, mg0_ref), (yb_ref, mg1_ref), (yc_ref, mg2_ref))):
        t = _dot(y_ref[...], wb_ref[n]) * _sigmoid(mg_ref[...].astype(F32))
        merged = t if merged is None else merged + t
    o = _dot(merged.astype(BF16), wo_ref[...])
    o = o * lax.rsqrt(jnp.mean(o * o, axis=-1, keepdims=True) + EPS) * gp_ref[...]
    x_new = x_ref[...] + gate_ref[0] * o
    if emit_h:
        g_next_ref, sc_next_ref, sh_next_ref, o_ref, h_ref = rest
        h_ref[...] = _modulated_norm(x_new, g_next_ref[...], sc_next_ref[0], sh_next_ref[0])
    else:
        (o_ref,) = rest
    o_ref[...] = x_new


def _merge(xf, ya, yb, yc, proj, wb_all, wo_all, layer, g_post, gate, next_norm, seq, tm):
    rows, d = xf.shape
    tpb = seq // tm
    row = lambda w_: pl.BlockSpec((tm, w_), lambda i: (i, 0))
    mg = lambda n: pl.BlockSpec((tm, d), lambda i: (i, P_MG // d + n))
    vec = pl.BlockSpec((1, d), lambda i: (0, 0))
    mod_spec = pl.BlockSpec((1, 1, d), lambda i: (i // tpb, 0, 0))
    const = pl.Buffered(1)
    emit_h = next_norm is not None
    in_specs = [row(d), row(BR_WIDTH), row(BR_WIDTH), row(BR_WIDTH), mg(0), mg(1), mg(2),
                pl.BlockSpec((None, N_BRANCH, BR_WIDTH, d), lambda i: (layer, 0, 0, 0), pipeline_mode=const),
                pl.BlockSpec((None, d, d), lambda i: (layer, 0, 0), pipeline_mode=const), vec, mod_spec]
    args = [xf, ya, yb, yc, proj, proj, proj, wb_all, wo_all, g_post.reshape(1, d),
            gate[:, None, :]]
    out_specs, out_shape = row(d), jax.ShapeDtypeStruct((rows, d), F32)
    if emit_h:
        g_next, sc_next, sh_next = next_norm
        in_specs += [vec, mod_spec, mod_spec]
        args += [g_next.reshape(1, d), sc_next[:, None, :], sh_next[:, None, :]]
        out_specs, out_shape = [out_specs, row(d)], [out_shape, jax.ShapeDtypeStruct((rows, d), BF16)]
    out = pl.pallas_call(
        functools.partial(_merge_kernel, emit_h=emit_h),
        grid=(rows // tm,),
        in_specs=in_specs,
        out_specs=out_specs,
        out_shape=out_shape,
        compiler_params=_params("arbitrary"),
        name="merge_out",
    )(*args)
    return (out[0], out[1]) if emit_h else (out, None)


def _tiles(seq, d_model):
    return dict(tm_in=min(1024, seq), tn_in=2048, tm_merge=min(256, seq),
                diff=(min(1024, seq), min(1024, seq)),
                sb=(min(512, seq), min(512, seq) // 2),
                nsa=(min(512, seq), min(512, seq)))


def kernel(x, c, norm_pre_g, norm_post_g, w_ada, b_ada, w_in, lambda_q1, lambda_k1, lambda_q2,
           lambda_k2, diff_norm_g, cmp_pe_k, cmp_w1_k, cmp_w2_k, cmp_pe_v, cmp_w1_v, cmp_w2_v,
           w_branch, w_out):
    bsz, seq, d = x.shape
    depth = w_in.shape[0]
    t = _tiles(seq, d)
    offs = _rest_offsets(d)
    assert d % 512 == 0 and ROPE_COLS % d == 0 and seq % t["tm_in"] == 0 and seq % 128 == 0
    n_total = -(-offs["END"] // t["tn_in"]) * t["tn_in"]
    tables = _rope_tables(seq)
    mod = _ada(c, w_ada, b_ada)
    xf = x.reshape(bsz * seq, d)
    mods = [jnp.split(mod[l], 3, axis=-1) for l in range(depth)]
    h = _prenorm(xf, norm_pre_g[0], mods[0][1], mods[0][0], seq, t["tm_merge"])
    wb_all, wo_all = w_branch.astype(BF16), w_out.astype(BF16)
    wp_all = _prep_w_in(w_in, d, n_total)
    for l in range(depth):
        proj = _inproj(h, tables, wp_all, l, seq, t["tm_in"], t["tn_in"])
        ya = _diff_attn(proj, offs, diff_norm_g[l], lambda_q1[l], lambda_k1[l], lambda_q2[l],
                        lambda_k2[l], l, bsz, seq, *t["diff"])
        perm = jnp.asarray(ROPE_PERM)
        w1_k = cmp_w1_k[l].reshape(CMP_LEN, HEAD_DIM, HEAD_DIM)[:, perm, :].reshape(CMP_LEN * HEAD_DIM, HEAD_DIM)
        kcmp = _compress(proj, P_NKC, cmp_pe_k[l][:, perm], w1_k, cmp_w2_k[l][:, perm], bsz, seq)
        vcmp = _compress(proj, offs["NVC"], cmp_pe_v[l], cmp_w1_v[l], cmp_w2_v[l], bsz, seq)
        yb = _nsa_attn(proj, offs, kcmp, vcmp, bsz, seq, *t["nsa"])
        yc = _sb_attn(proj, offs, bsz, seq, *t["sb"])
        next_norm = (norm_pre_g[l + 1], mods[l + 1][1], mods[l + 1][0]) if l + 1 < depth else None
        xf, h = _merge(xf, ya, yb, yc, proj, wb_all, wo_all, l, norm_post_g[l], mods[l][2],
                       next_norm, seq, t["tm_merge"])
    return xf.reshape(bsz, seq, d)
```

```python
import functools
import math

import jax
import jax.numpy as jnp
import numpy as np
from jax import lax
from jax.experimental import pallas as pl
from jax.experimental.pallas import tpu as pltpu

F32 = jnp.float32
BF16 = jnp.bfloat16

HEAD_DIM = 128
BR_WIDTH = 1024
N_BRANCH = 3
ROPE_THETA = 500000.0
ROPE_DIM = HEAD_DIM // 4
ROPE_HALF = ROPE_DIM // 2
EPS = 1e-6
NEG_INF = -1e30
FORCE_SCORE = 1e6
DIFF_HEADS = BR_WIDTH // (2 * HEAD_DIM)
DIFF_NORM_EPS = 1e-5
NSA_HEADS = BR_WIDTH // HEAD_DIM
NSA_GROUPS = 2
NSA_REP = NSA_HEADS // NSA_GROUPS
CMP_LEN = 32
CMP_STRIDE = 16
SLC_LEN = 64
SLC_SHIFT = SLC_LEN.bit_length() - 1
SLC_TOPK = 16
WINDOW = 512
SB_HEADS = BR_WIDTH // HEAD_DIM
NSA_KV = NSA_GROUPS * HEAD_DIM
ATTN_SCALE = HEAD_DIM ** -0.5
Q_PRESCALE = ATTN_SCALE * math.log2(math.e)

LANES = 128
SLC_BLOCKS_PAD = 128
WINDOW_ROWS = 128
UNSELECTED_BIAS = 2.0 ** 30
SB_UNDERFLOW = -104.0
VMEM_LIMIT = 56 * 1024 * 1024

_O_AQ, _O_AK, _O_AV, _O_AZ = 0, 1024, 2048, 3072
_O_NQ, _O_NKC, _O_NVC, _O_NKS, _O_NVS, _O_NKW, _O_NVW = 4096, 5120, 5376, 5632, 5888, 6144, 6400
_O_NG, _O_NZ = 6656, 6680
_O_CQ, _O_CK, _O_CV, _O_CZ, _O_MG = 7704, 8728, 9752, 10776, 11800
N_GATES = 3 * NSA_HEADS
P_AQ, P_AK, P_NQ, P_NKC, P_NKS, P_NKW = 0, 1024, 2048, 3072, 3328, 3584
ROPE_COLS = 4096
ROPE_PERM = (list(range(0, ROPE_HALF)) + list(range(ROPE_DIM, HEAD_DIM // 2 + ROPE_HALF))
             + list(range(ROPE_HALF, ROPE_DIM)) + list(range(HEAD_DIM // 2 + ROPE_HALF, HEAD_DIM)))
P_MG = ROPE_COLS


def _rest_offsets(d_model):
    r0 = ROPE_COLS + N_BRANCH * d_model
    offs = dict(AV=r0, AZ=r0 + 1024, NZ=r0 + 2048, CQ=r0 + 3072, CK=r0 + 4096, CV=r0 + 5120,
                CZ=r0 + 6144, NVC=r0 + 7168, NVS=r0 + 7424, NVW=r0 + 7680, NG=r0 + 7936)
    offs["END"] = r0 + 7936 + NSA_GROUPS * LANES
    return offs


PREP_ROWS = 2 * LANES
GATES_PER_GROUP = N_GATES // NSA_GROUPS
_ZERO, _COPY, _GATES, _COPY_ROPE = 0, 1, 2, 3


def _prep_plan(d_model, n_total):
    offs = _rest_offsets(d_model)
    pieces = [(P_AQ, _O_AQ, 1024), (P_AK, _O_AK, 1024), (P_NQ, _O_NQ, 1024), (P_NKC, _O_NKC, 256),
              (P_NKS, _O_NKS, 256), (P_NKW, _O_NKW, 256), (P_MG, _O_MG, N_BRANCH * d_model),
              (offs["AV"], _O_AV, 1024), (offs["AZ"], _O_AZ, 1024), (offs["NZ"], _O_NZ, 1024),
              (offs["CQ"], _O_CQ, 4096), (offs["NVC"], _O_NVC, 256), (offs["NVS"], _O_NVS, 256),
              (offs["NVW"], _O_NVW, 256), (offs["NG"], _O_NG, PREP_ROWS)]
    n_tiles = n_total // PREP_ROWS
    kind = np.full((n_tiles,), _ZERO, np.int32)
    src = np.zeros((n_tiles,), np.int32)
    for dest, first, width in pieces:
        for off in range(0, width, PREP_ROWS):
            assert (dest + off) % PREP_ROWS == 0 and (first + off) % 8 == 0
            kind[(dest + off) // PREP_ROWS] = _COPY_ROPE if dest + off < ROPE_COLS else _COPY
            src[(dest + off) // PREP_ROWS] = first + off
    assert NSA_GROUPS * LANES == PREP_ROWS
    kind[offs["NG"] // PREP_ROWS] = _GATES
    return jnp.asarray(kind), jnp.asarray(src // 8)


def _prep_kernel(kind_ref, src_ref, a_ref, o_ref):
    kind = kind_ref[pl.program_id(1)]

    @pl.when(kind == _ZERO)
    def _():
        o_ref[...] = jnp.zeros(o_ref.shape, BF16)

    @pl.when(kind == _COPY)
    def _():
        o_ref[...] = a_ref[0].astype(BF16)

    @pl.when(kind == _COPY_ROPE)
    def _():
        runs = ((0, 0, ROPE_HALF), (ROPE_HALF, ROPE_DIM, HEAD_DIM // 2 - ROPE_HALF),
                (HEAD_DIM // 2, ROPE_HALF, ROPE_HALF), (HEAD_DIM // 2 + ROPE_HALF, HEAD_DIM // 2 + ROPE_HALF,
                                                       HEAD_DIM // 2 - ROPE_HALF))
        for head in range(PREP_ROWS // HEAD_DIM):
            for dst, src, n in runs:
                o_ref[pl.ds(head * HEAD_DIM + dst, n), :] = a_ref[0, pl.ds(head * HEAD_DIM + src, n), :].astype(BF16)

    @pl.when(kind == _GATES)
    def _():
        o_ref[...] = jnp.zeros(o_ref.shape, BF16)
        for g in range(NSA_GROUPS):
            o_ref[pl.ds(g * LANES, GATES_PER_GROUP), :] = (
                a_ref[0, pl.ds(g * GATES_PER_GROUP, GATES_PER_GROUP), :].astype(BF16))


def _prep_w_in(w_in, d_model, n_total):
    wt = jnp.swapaxes(w_in, 1, 2)
    depth, n_in, d = wt.shape
    kind, src = _prep_plan(d_model, n_total)
    grid_spec = pltpu.PrefetchScalarGridSpec(
        num_scalar_prefetch=2,
        grid=(depth, n_total // PREP_ROWS),
        in_specs=[pl.BlockSpec((pl.Element(1), pl.Element(PREP_ROWS), pl.Element(d)),
                               lambda l, j, kind, src: (l, src[j] * 8, 0))],
        out_specs=pl.BlockSpec((None, PREP_ROWS, d), lambda l, j, kind, src: (l, j, 0)),
    )
    return pl.pallas_call(
        _prep_kernel,
        grid_spec=grid_spec,
        out_shape=jax.ShapeDtypeStruct((depth, n_total, d), BF16),
        compiler_params=_params("arbitrary", "arbitrary"),
        name="w_in_prep",
    )(kind, src, wt)


def _rope_tables(seq):
    pos = jnp.arange(seq, dtype=F32)
    inv = ROPE_THETA ** (-jnp.arange(0, ROPE_DIM, 2, dtype=F32) / ROPE_DIM)
    ang = pos[:, None] * inv[None, :]
    cos, sin = jnp.cos(ang), jnp.sin(ang)
    ones = jnp.ones((seq, HEAD_DIM // 2 - ROPE_HALF), F32)
    zeros = jnp.zeros((seq, HEAD_DIM // 2 - ROPE_HALF), F32)
    c = jnp.concatenate([cos, ones, cos, ones], axis=1)
    s = jnp.concatenate([-sin, zeros, sin, zeros], axis=1)
    return c, s


def _sigmoid(x):
    return 1.0 / (1.0 + jnp.exp(-x))


def _silu(x):
    return x * _sigmoid(x)


def _dot_nt(a, b):
    return lax.dot_general(a, b, (((1,), (1,)), ((), ())), preferred_element_type=F32)


def _dot(a, b):
    return jnp.dot(a, b, preferred_element_type=F32)


def _params(*sem):
    return pltpu.CompilerParams(dimension_semantics=sem, vmem_limit_bytes=VMEM_LIMIT)


def _ada_kernel(c_ref, w_ref, b_ref, o_ref):
    c = c_ref[...]
    o_ref[0] = _dot(_silu(c).astype(BF16), w_ref[0].astype(BF16)) + b_ref[0]


def _ada(c, w_ada, b_ada):
    depth, d, n3 = w_ada.shape
    bsz = c.shape[0]
    rows = max(8, bsz)
    cp = jnp.zeros((rows, d), F32).at[:bsz].set(c)
    tn = math.gcd(1024, n3)
    out = pl.pallas_call(
        _ada_kernel,
        grid=(depth, n3 // tn),
        in_specs=[pl.BlockSpec((rows, d), lambda l, j: (0, 0)),
                  pl.BlockSpec((1, d, tn), lambda l, j: (l, 0, j)),
                  pl.BlockSpec((1, 1, tn), lambda l, j: (l, 0, j))],
        out_specs=pl.BlockSpec((1, rows, tn), lambda l, j: (l, 0, j)),
        out_shape=jax.ShapeDtypeStruct((depth, rows, n3), F32),
        compiler_params=_params("arbitrary", "arbitrary"),
        name="ada_mod",
    )(cp, w_ada, b_ada.reshape(depth, 1, n3))
    return out[:, :bsz]


def _modulated_norm(x, g, scale, shift):
    y = x * lax.rsqrt(jnp.mean(x * x, axis=-1, keepdims=True) + EPS) * g
    return (y * (1.0 + scale) + shift).astype(BF16)


def _prenorm_kernel(x_ref, g_ref, sc_ref, sh_ref, h_ref):
    h_ref[...] = _modulated_norm(x_ref[...], g_ref[...], sc_ref[0], sh_ref[0])


def _prenorm(xf, g, scale, shift, seq, tm):
    rows, d = xf.shape
    tpb = seq // tm
    mod_spec = pl.BlockSpec((1, 1, d), lambda i: (i // tpb, 0, 0))
    return pl.pallas_call(
        _prenorm_kernel,
        grid=(rows // tm,),
        in_specs=[pl.BlockSpec((tm, d), lambda i: (i, 0)), pl.BlockSpec((1, d), lambda i: (0, 0)),
                  mod_spec, mod_spec],
        out_specs=pl.BlockSpec((tm, d), lambda i: (i, 0)),
        out_shape=jax.ShapeDtypeStruct((rows, d), BF16),
        compiler_params=_params("arbitrary"),
        name="pre_norm",
    )(xf, g.reshape(1, d), scale[:, None, :], shift[:, None, :])


def _inproj_kernel(h_ref, c_ref, s_ref, w_ref, o_ref, *, n_rope_tiles, tn):
    j = pl.program_id(1)
    y = _dot_nt(h_ref[...], w_ref[...])

    @pl.when(j < n_rope_tiles)
    def _():
        c, s = c_ref[...], s_ref[...]
        for hh in range(tn // HEAD_DIM):
            sl = slice(hh * HEAD_DIM, (hh + 1) * HEAD_DIM)
            yh = y[:, sl]
            r = yh * c + pltpu.roll(yh, HEAD_DIM // 2, 1) * s
            col = j * tn + hh * HEAD_DIM
            is_q = ((col >= P_AQ) & (col < P_AQ + BR_WIDTH)) | ((col >= P_NQ) & (col < P_NQ + BR_WIDTH))
            o_ref[:, sl] = (r * jnp.where(is_q, Q_PRESCALE, 1.0)).astype(BF16)

    @pl.when(j >= n_rope_tiles)
    def _():
        o_ref[...] = y.astype(BF16)


def _inproj(h, tables, wp_all, layer, seq, tm, tn):
    rows, d = h.shape
    n_total = wp_all.shape[1]
    tpb = seq // tm
    c, s = tables
    tab_spec = pl.BlockSpec((tm, HEAD_DIM), lambda i, j: (i % tpb, 0))
    return pl.pallas_call(
        functools.partial(_inproj_kernel, n_rope_tiles=ROPE_COLS // tn, tn=tn),
        grid=(rows // tm, n_total // tn),
        in_specs=[pl.BlockSpec((tm, d), lambda i, j: (i, 0)),
                  tab_spec, tab_spec,
                  pl.BlockSpec((None, tn, d), lambda i, j: (layer, j, 0))],
        out_specs=pl.BlockSpec((tm, tn), lambda i, j: (i, j)),
        out_shape=jax.ShapeDtypeStruct((rows, n_total), BF16),
        compiler_params=_params("arbitrary", "arbitrary"),
        name="in_proj",
    )(h, c, s, wp_all)


def _softmax_update(carry, s, v):
    m, l, acc = carry
    m_new = jnp.maximum(m, jnp.max(s, axis=-1, keepdims=True))
    alpha = jnp.exp2(m - m_new)
    p = jnp.exp2(s - m_new)
    l = alpha * l + jnp.sum(p, axis=-1, keepdims=True)
    acc = alpha * acc + _dot(p.astype(BF16), v)
    return m_new, l, acc


def _softmax_init(rows, width):
    return (jnp.full((rows, 1), NEG_INF, F32), jnp.zeros((rows, 1), F32),
            jnp.zeros((rows, width), F32))


def _diff_kernel(q_ref, k_ref, v_ref, z_ref, ng_ref, lq1_ref, lk1_ref, lq2_ref, lk2_ref, o_ref,
                 *, tq, tk, lam_init):
    i = pl.program_id(2)
    t0 = i * tq
    q = q_ref[...]
    qs = (q[:, :HEAD_DIM], q[:, HEAD_DIM:])
    n_full = t0 // tk

    def step(jt, carry, masked):
        k0 = pl.multiple_of(jt * tk, tk)
        k = k_ref[pl.ds(k0, tk), :]
        v = v_ref[pl.ds(k0, tk), :]
        s = [_dot_nt(qs[cc], k[:, cc * HEAD_DIM:(cc + 1) * HEAD_DIM]) for cc in range(2)]
        if masked:
            kpos = k0 + lax.broadcasted_iota(jnp.int32, (tq, tk), 1)
            tpos = t0 + lax.broadcasted_iota(jnp.int32, (tq, tk), 0)
            s = [jnp.where(kpos <= tpos, sc, NEG_INF) for sc in s]
        return tuple(_softmax_update(carry[cc], s[cc], v) for cc in range(2))

    init = (_softmax_init(tq, 2 * HEAD_DIM), _softmax_init(tq, 2 * HEAD_DIM))
    carry = lax.fori_loop(0, n_full, lambda jt, c: step(jt, c, False), init)
    carry = step(n_full, carry, True)

    lam = (jnp.exp(jnp.sum(lq1_ref[...] * lk1_ref[...], axis=-1, keepdims=True))
           - jnp.exp(jnp.sum(lq2_ref[...] * lk2_ref[...], axis=-1, keepdims=True)) + lam_init)
    o0 = carry[0][2] / carry[0][1]
    o1 = carry[1][2] / carry[1][1]
    o = o0 - lam * o1
    o = o * lax.rsqrt(jnp.mean(o * o, axis=-1, keepdims=True) + DIFF_NORM_EPS) * ng_ref[...]
    o = o * (1.0 - lam_init)
    o_ref[...] = (o * _silu(z_ref[...].astype(F32))).astype(BF16)


def _diff_attn(proj, offs, norm_g, lq1, lk1, lq2, lk2, layer_idx, bsz, seq, tq, tk):
    nq = seq // tq
    w = 2 * HEAD_DIM
    lam_init = 0.8 - 0.6 * math.exp(-0.3 * layer_idx)
    assert tk % tq == 0
    vec = pl.BlockSpec((1, HEAD_DIM), lambda b, h, i: (0, 0))
    return pl.pallas_call(
        functools.partial(_diff_kernel, tq=tq, tk=tk, lam_init=lam_init),
        grid=(bsz, DIFF_HEADS, nq),
        in_specs=[pl.BlockSpec((tq, w), lambda b, h, i: (b * nq + i, P_AQ // w + h)),
                  pl.BlockSpec((seq, w), lambda b, h, i: (b, P_AK // w + h)),
                  pl.BlockSpec((seq, w), lambda b, h, i: (b, offs["AV"] // w + h)),
                  pl.BlockSpec((tq, w), lambda b, h, i: (b * nq + i, offs["AZ"] // w + h)),
                  pl.BlockSpec((1, w), lambda b, h, i: (0, 0)),
                  vec, vec, vec, vec],
        out_specs=pl.BlockSpec((tq, w), lambda b, h, i: (b * nq + i, h)),
        out_shape=jax.ShapeDtypeStruct((bsz * seq, BR_WIDTH), BF16),
        compiler_params=_params("arbitrary", "arbitrary", "arbitrary"),
        name="diff_attn",
    )(proj, proj, proj, proj, norm_g.reshape(1, w), lq1.reshape(1, -1), lk1.reshape(1, -1),
      lq2.reshape(1, -1), lk2.reshape(1, -1))


def _sb_kernel(q_ref, k_ref, v_ref, z_ref, o_ref, *, tq, tk):
    i = pl.program_id(2)
    t0 = i * tq
    n_full = t0 // tk
    qh = (q_ref[pl.ds(0, tk), :], q_ref[pl.ds(tk, tk), :])
    upper2 = jnp.where((lax.broadcasted_iota(jnp.int32, (2 * tk, tk), 0) & (tk - 1))
                       > lax.broadcasted_iota(jnp.int32, (2 * tk, tk), 1), 1.0, 0.0).astype(BF16)
    strict = (lax.broadcasted_iota(jnp.int32, (tk, tk), 1) < lax.broadcasted_iota(jnp.int32, (tk, tk), 0))

    def steps(items):
        kv = []
        for _, jt, _, _ in items:
            k0 = pl.multiple_of(jt * tk, tk)
            kv.append((k_ref[pl.ds(k0, tk), :], v_ref[pl.ds(k0, tk), :]))
        z = [_dot_nt(qh[it[0]], kv[n][0]) * ATTN_SCALE for n, it in enumerate(items)]
        log_1m = [-(jnp.maximum(zz, 0.0) + jnp.log(1.0 + jnp.exp(-jnp.abs(zz)))) for zz in z]
        log_1m = [jnp.where(strict, lm, 0.0) if it[3] else lm for lm, it in zip(log_1m, items)]
        stacked = []
        for lm in log_1m:
            hi = lm.astype(BF16)
            stacked.append(jnp.concatenate([hi, (lm - hi.astype(F32)).astype(BF16)], axis=1))
        after = [_dot(st, upper2) + it[2][0] for st, it in zip(stacked, items)]
        a = [jnp.exp(zz + lm + af) for zz, lm, af in zip(z, log_1m, after)]
        a = [jnp.where(strict, aa, 0.0) if it[3] else aa for aa, it in zip(a, items)]
        pv = [_dot(aa.astype(BF16), kv[n][1]) for n, aa in enumerate(a)]
        return [(it[2][0] + jnp.sum(lm, axis=-1, keepdims=True), it[2][1] + p)
                for it, lm, p in zip(items, log_1m, pv)]

    zero = (jnp.zeros((tk, 1), F32), jnp.zeros((tk, HEAD_DIM), F32))
    c_b, c_a = steps([(1, n_full + 1, zero, True), (0, n_full, zero, True)])
    (c_b,) = steps([(1, n_full, c_b, False)])

    def alive(ca, cb):
        return jnp.maximum(jnp.max(ca[0]), jnp.max(cb[0]))

    def cond(c):
        return (c[0] < n_full) & (c[1] > SB_UNDERFLOW)

    def body(c):
        jt = n_full - 1 - c[0]
        ca, cb = steps([(0, jt, c[2], False), (1, jt, c[3], False)])
        return c[0] + 1, alive(ca, cb), ca, cb

    _, _, c_a, c_b = lax.while_loop(cond, body, (jnp.int32(0), alive(c_a, c_b), c_a, c_b))
    zg = z_ref[...].astype(F32)
    o_ref[pl.ds(0, tk), :] = (c_a[1] * _silu(zg[:tk])).astype(BF16)
    o_ref[pl.ds(tk, tk), :] = (c_b[1] * _silu(zg[tk:])).astype(BF16)


def _sb_attn(proj, offs, bsz, seq, tq, tk):
    nq = seq // tq
    w = HEAD_DIM
    assert tq == 2 * tk
    return pl.pallas_call(
        functools.partial(_sb_kernel, tq=tq, tk=tk),
        grid=(bsz, SB_HEADS, nq),
        in_specs=[pl.BlockSpec((tq, w), lambda b, h, i: (b * nq + i, offs["CQ"] // w + h)),
                  pl.BlockSpec((seq, w), lambda b, h, i: (b, offs["CK"] // w + h)),
                  pl.BlockSpec((seq, w), lambda b, h, i: (b, offs["CV"] // w + h)),
                  pl.BlockSpec((tq, w), lambda b, h, i: (b * nq + i, offs["CZ"] // w + h))],
        out_specs=pl.BlockSpec((tq, w), lambda b, h, i: (b * nq + i, h)),
        out_shape=jax.ShapeDtypeStruct((bsz * seq, BR_WIDTH), BF16),
        compiler_params=_params("arbitrary", "arbitrary", "arbitrary"),
        name="stick_breaking",
    )(proj, proj, proj, proj)


def _compress_kernel(x_ref, pe_ref, w1_ref, w2_ref, o_ref, xs_ref, *, seq):
    nc = seq // CMP_STRIDE
    xs_ref[pl.ds(0, seq), :] = x_ref[...].astype(F32)
    xs_ref[pl.ds(seq, CMP_STRIDE), :] = jnp.zeros((CMP_STRIDE, HEAD_DIM), F32)
    hid = None
    for l in range(CMP_LEN):
        rows = xs_ref[pl.ds(l, nc, stride=CMP_STRIDE), :] + pe_ref[l:l + 1, :]
        t = _dot(rows.astype(BF16), w1_ref[l * HEAD_DIM:(l + 1) * HEAD_DIM, :])
        hid = t if hid is None else hid + t
    o_ref[0, 0] = _dot(_silu(hid).astype(BF16), w2_ref[...]).astype(BF16)


def _compress(proj, col_off, pe, w1, w2, bsz, seq):
    nc = seq // CMP_STRIDE
    w = HEAD_DIM
    return pl.pallas_call(
        functools.partial(_compress_kernel, seq=seq),
        grid=(bsz, NSA_GROUPS),
        in_specs=[pl.BlockSpec((seq, w), lambda b, g: (b, col_off // w + g)),
                  pl.BlockSpec((CMP_LEN, w), lambda b, g: (0, 0)),
                  pl.BlockSpec((CMP_LEN * w, w), lambda b, g: (0, 0)),
                  pl.BlockSpec((w, w), lambda b, g: (0, 0))],
        out_specs=pl.BlockSpec((1, 1, nc, w), lambda b, g: (b, g, 0, 0)),
        out_shape=jax.ShapeDtypeStruct((bsz, NSA_GROUPS, nc, w), BF16),
        scratch_shapes=[pltpu.VMEM((seq + CMP_STRIDE, w), F32)],
        compiler_params=_params("arbitrary", "arbitrary"),
        name="nsa_compress",
    )(proj, pe, w1.astype(BF16), w2.astype(BF16))


def _nsa_kernel(q_ref, kc_ref, vc_ref, ks_ref, vs_ref, kw_ref, vw_ref, g_ref, z_ref, o_ref, imp_ref,
                *, tq, tk, seq):
    i = pl.program_id(2)
    t0 = i * tq
    nc = seq // CMP_STRIDE
    nb = seq // SLC_LEN
    nbp = SLC_BLOCKS_PAD
    rep = NSA_REP
    q = q_ref[...]
    qh = [q[:, r * HEAD_DIM:(r + 1) * HEAD_DIM] for r in range(rep)]

    kc, vc = kc_ref[0, 0], vc_ref[0, 0]
    tpos_c = t0 + lax.broadcasted_iota(jnp.int32, (tq, nc), 0)
    ncol = lax.broadcasted_iota(jnp.int32, (tq, nc), 1)
    cvalid = CMP_STRIDE * ncol + (CMP_LEN - 1) <= tpos_c
    psum = jnp.zeros((tq, nc), F32)
    o_cmp = []
    for r in range(rep):
        s = jnp.where(cvalid, _dot_nt(qh[r], kc), NEG_INF)
        e = jnp.where(cvalid, jnp.exp2(s - jnp.max(s, axis=-1, keepdims=True)), 0.0)
        den = jnp.sum(e, axis=-1, keepdims=True)
        p = e / jnp.where(den > 0.0, den, 1.0)
        psum = psum + p
        o_cmp.append(_dot(p.astype(BF16), vc))

    nb8 = imp_ref.shape[0]
    jrow = lax.broadcasted_iota(jnp.int32, (nb8, nc), 0)
    ncol2 = lax.broadcasted_iota(jnp.int32, (nb8, nc), 1)
    overlap = jnp.where((CMP_STRIDE * ncol2 < SLC_LEN * jrow + SLC_LEN)
                        & (CMP_STRIDE * ncol2 + CMP_LEN > SLC_LEN * jrow)
                        & (ncol2 < nc - 1) & (jrow < nb), 1.0, 0.0).astype(BF16)
    p1 = psum.astype(BF16)
    r1 = psum - p1.astype(F32)
    p2 = r1.astype(BF16)
    p3 = (r1 - p2.astype(F32)).astype(BF16)
    imp = _dot_nt(overlap, p1) + _dot_nt(overlap, p2) + _dot_nt(overlap, p3)
    jt_ = lax.broadcasted_iota(jnp.int32, (nb8, tq), 0)
    tblk = jnp.right_shift(t0 + lax.broadcasted_iota(jnp.int32, (nb8, tq), 1), SLC_SHIFT)
    imp = jnp.where((jt_ == tblk) | (jt_ == 0), FORCE_SCORE, imp)
    imp = jnp.where(jt_ <= tblk, imp, NEG_INF)
    imp_ref[...] = imp

    def rank_body(jp, cnt):
        row = imp_ref[pl.ds(jp, 1), :]
        beats = (row > imp) | ((row == imp) & (jp < jt_))
        return cnt + jnp.where(beats, 1.0, 0.0)

    n_live = jnp.minimum((t0 + tq - 1) // SLC_LEN + 1, nb)
    cnt = lax.fori_loop(0, n_live, rank_body, jnp.zeros((nb8, tq), F32))
    keep = (cnt < float(min(SLC_TOPK, nb))) & (imp > 0.5 * NEG_INF)
    unsel_t = jnp.where(keep, 0.0, 1.0)
    if nbp > nb8:
        unsel_t = jnp.concatenate([unsel_t, jnp.zeros((nbp - nb8, tq), F32)], axis=0)
    unselected = unsel_t.T.astype(BF16)

    rows = rep * tq
    q_sel = jnp.concatenate([jnp.concatenate([qh[r], unselected], axis=1) for r in range(rep)], axis=0)

    def slc_step(jt, carry, masked):
        k0 = pl.multiple_of(jt * tk, tk)
        k = ks_ref[pl.ds(k0, tk), :]
        v = vs_ref[pl.ds(k0, tk), :]
        kblk = jnp.right_shift(k0 + lax.broadcasted_iota(jnp.int32, (tk, nbp), 0), SLC_SHIFT)
        bias = jnp.where(lax.broadcasted_iota(jnp.int32, (tk, nbp), 1) == kblk,
                         -UNSELECTED_BIAS, 0.0).astype(BF16)
        s = _dot_nt(q_sel, jnp.concatenate([k, bias], axis=1))
        if masked:
            causal = (k0 + lax.broadcasted_iota(jnp.int32, (tq, tk), 1)
                      <= t0 + lax.broadcasted_iota(jnp.int32, (tq, tk), 0))
            s = s + jnp.concatenate([jnp.where(causal, 0.0, NEG_INF)] * rep, axis=0)
        return _softmax_update(carry, s, v)

    n_full = t0 // tk
    carry = lax.fori_loop(0, n_full, lambda jt, c: slc_step(jt, c, False), _softmax_init(rows, HEAD_DIM))
    m_s, l_s, acc_s = slc_step(n_full, carry, True)
    o_slc = acc_s / l_s

    tw = min(tq, WINDOW_ROWS)
    span = min(WINDOW + tw, seq)
    o_win = []
    for w in range(tq // tw):
        r0 = t0 + w * tw
        start = pl.multiple_of(jnp.maximum(r0 - WINDOW, 0), tw)
        kpos = start + lax.broadcasted_iota(jnp.int32, (tw, span), 1)
        row_w = r0 + lax.broadcasted_iota(jnp.int32, (tw, span), 0)
        in_window = jnp.where(kpos <= row_w, jnp.where(kpos > row_w - WINDOW, 0.0, NEG_INF), NEG_INF)
        q_w = jnp.concatenate([qh[r][w * tw:(w + 1) * tw] for r in range(rep)], axis=0)
        s = _dot_nt(q_w, kw_ref[pl.ds(start, span), :]) + jnp.concatenate([in_window] * rep, axis=0)
        p = jnp.exp2(s - jnp.max(s, axis=-1, keepdims=True))
        o_win.append(_dot(p.astype(BF16), vw_ref[pl.ds(start, span), :]) / jnp.sum(p, axis=-1, keepdims=True))

    gates = _sigmoid(g_ref[...].astype(F32))
    z = z_ref[...].astype(F32)
    for r in range(rep):
        rs = slice(r * tq, (r + 1) * tq)
        cs = slice(r * HEAD_DIM, (r + 1) * HEAD_DIM)
        o_win_r = jnp.concatenate([o[r * tw:(r + 1) * tw] for o in o_win], axis=0)
        o = (gates[:, 3 * r:3 * r + 1] * o_cmp[r] + gates[:, 3 * r + 1:3 * r + 2] * o_slc[rs]
             + gates[:, 3 * r + 2:3 * r + 3] * o_win_r)
        o_ref[:, cs] = (o * _silu(z[:, cs])).astype(BF16)


def _nsa_attn(proj, offs, kcmp, vcmp, bsz, seq, tq, tk):
    nq = seq // tq
    nc = seq // CMP_STRIDE
    w = HEAD_DIM
    gw = NSA_REP * HEAD_DIM
    assert seq // SLC_LEN <= SLC_BLOCKS_PAD and tq & (tq - 1) == 0 and tk % tq == 0
    kv = lambda off: pl.BlockSpec((seq, w), lambda b, g, i: (b, off // w + g))
    cmp_spec = pl.BlockSpec((1, 1, nc, w), lambda b, g, i: (b, g, 0, 0))
    return pl.pallas_call(
        functools.partial(_nsa_kernel, tq=tq, tk=tk, seq=seq),
        grid=(bsz, NSA_GROUPS, nq),
        in_specs=[pl.BlockSpec((tq, gw), lambda b, g, i: (b * nq + i, P_NQ // gw + g)),
                  cmp_spec, cmp_spec,
                  kv(P_NKS), kv(offs["NVS"]), kv(P_NKW), kv(offs["NVW"]),
                  pl.BlockSpec((tq, LANES), lambda b, g, i: (b * nq + i, offs["NG"] // LANES + g)),
                  pl.BlockSpec((tq, gw), lambda b, g, i: (b * nq + i, offs["NZ"] // gw + g))],
        out_specs=pl.BlockSpec((tq, gw), lambda b, g, i: (b * nq + i, g)),
        out_shape=jax.ShapeDtypeStruct((bsz * seq, BR_WIDTH), BF16),
        scratch_shapes=[pltpu.VMEM((-(-(seq // SLC_LEN) // 8) * 8, tq), F32)],
        compiler_params=_params("arbitrary", "arbitrary", "arbitrary"),
        name="nsa_attn",
    )(proj, kcmp, vcmp, proj, proj, proj, proj, proj, proj)


def _merge_kernel(x_ref, ya_ref, yb_ref, yc_ref, mg0_ref, mg1_ref, mg2_ref, wb_ref, wo_ref, gp_ref,
                  gate_ref, *rest, emit_h):
    merged = None
    for n, (y_ref, mg_ref) in enumerate(((ya_ref, mg0_ref), (yb_ref, mg1_ref), (yc_ref, mg2_ref))):
        t = _dot(y_ref[...], wb_ref[n]) * _sigmoid(mg_ref[...].astype(F32))
        merged = t if merged is None else merged + t
    o = _dot(merged.astype(BF16), wo_ref[...])
    o = o * lax.rsqrt(jnp.mean(o * o, axis=-1, keepdims=True) + EPS) * gp_ref[...]
    x_new = x_ref[...] + gate_ref[0] * o
    if emit_h:
        g_next_ref, sc_next_ref, sh_next_ref, o_ref, h_ref = rest
        h_ref[...] = _modulated_norm(x_new, g_next_ref[...], sc_next_ref[0], sh_next_ref[0])
    else:
        (o_ref,) = rest
    o_ref[...] = x_new


def _merge(xf, ya, yb, yc, proj, wb_all, wo_all, layer, g_post, gate, next_norm, seq, tm):
    rows, d = xf.shape
    tpb = seq // tm
    row = lambda w_: pl.BlockSpec((tm, w_), lambda i: (i, 0))
    mg = lambda n: pl.BlockSpec((tm, d), lambda i: (i, P_MG // d + n))
    vec = pl.BlockSpec((1, d), lambda i: (0, 0))
    mod_spec = pl.BlockSpec((1, 1, d), lambda i: (i // tpb, 0, 0))
    const = pl.Buffered(1)
    emit_h = next_norm is not None
    in_specs = [row(d), row(BR_WIDTH), row(BR_WIDTH), row(BR_WIDTH), mg(0), mg(1), mg(2),
                pl.BlockSpec((None, N_BRANCH, BR_WIDTH, d), lambda i: (layer, 0, 0, 0), pipeline_mode=const),
                pl.BlockSpec((None, d, d), lambda i: (layer, 0, 0), pipeline_mode=const), vec, mod_spec]
    args = [xf, ya, yb, yc, proj, proj, proj, wb_all, wo_all, g_post.reshape(1, d),
            gate[:, None, :]]
    out_specs, out_shape = row(d), jax.ShapeDtypeStruct((rows, d), F32)
    if emit_h:
        g_next, sc_next, sh_next = next_norm
        in_specs += [vec, mod_spec, mod_spec]
        args += [g_next.reshape(1, d), sc_next[:, None, :], sh_next[:, None, :]]
        out_specs, out_shape = [out_specs, row(d)], [out_shape, jax.ShapeDtypeStruct((rows, d), BF16)]
    out = pl.pallas_call(
        functools.partial(_merge_kernel, emit_h=emit_h),
        grid=(rows // tm,),
        in_specs=in_specs,
        out_specs=out_specs,
        out_shape=out_shape,
        compiler_params=_params("arbitrary"),
        name="merge_out",
    )(*args)
    return (out[0], out[1]) if emit_h else (out, None)


def _tiles(seq, d_model):
    return dict(tm_in=min(1024, seq), tn_in=2048, tm_merge=min(256, seq),
                diff=(min(1024, seq), min(1024, seq)),
                sb=(min(512, seq), min(512, seq) // 2),
                nsa=(min(512, seq), min(512, seq)))


def kernel(x, c, norm_pre_g, norm_post_g, w_ada, b_ada, w_in, lambda_q1, lambda_k1, lambda_q2,
           lambda_k2, diff_norm_g, cmp_pe_k, cmp_w1_k, cmp_w2_k, cmp_pe_v, cmp_w1_v, cmp_w2_v,
           w_branch, w_out):
    bsz, seq, d = x.shape
    depth = w_in.shape[0]
    t = _tiles(seq, d)
    offs = _rest_offsets(d)
    assert d % 512 == 0 and ROPE_COLS % d == 0 and seq % t["tm_in"] == 0 and seq % 128 == 0
    n_total = -(-offs["END"] // t["tn_in"]) * t["tn_in"]
    tables = _rope_tables(seq)
    mod = _ada(c, w_ada, b_ada)
    xf = x.reshape(bsz * seq, d)
    mods = [jnp.split(mod[l], 3, axis=-1) for l in range(depth)]
    h = _prenorm(xf, norm_pre_g[0], mods[0][1], mods[0][0], seq, t["tm_merge"])
    wb_all, wo_all = w_branch.astype(BF16), w_out.astype(BF16)
    wp_all = _prep_w_in(w_in, d, n_total)
    for l in range(depth):
        proj = _inproj(h, tables, wp_all, l, seq, t["tm_in"], t["tn_in"])
        ya = _diff_attn(proj, offs, diff_norm_g[l], lambda_q1[l], lambda_k1[l], lambda_q2[l],
                        lambda_k2[l], l, bsz, seq, *t["diff"])
        perm = jnp.asarray(ROPE_PERM)
        w1_k = cmp_w1_k[l].reshape(CMP_LEN, HEAD_DIM, HEAD_DIM)[:, perm, :].reshape(CMP_LEN * HEAD_DIM, HEAD_DIM)
        kcmp = _compress(proj, P_NKC, cmp_pe_k[l][:, perm], w1_k, cmp_w2_k[l][:, perm], bsz, seq)
        vcmp = _compress(proj, offs["NVC"], cmp_pe_v[l], cmp_w1_v[l], cmp_w2_v[l], bsz, seq)
        yb = _nsa_attn(proj, offs, kcmp, vcmp, bsz, seq, *t["nsa"])
        yc = _sb_attn(proj, offs, bsz, seq, *t["sb"])
        next_norm = (norm_pre_g[l + 1], mods[l + 1][1], mods[l + 1][0]) if l + 1 < depth else None
        xf, h = _merge(xf, ya, yb, yc, proj, wb_all, wo_all, l, norm_post_g[l], mods[l][2],
                       next_norm, seq, t["tm_merge"])
    return xf.reshape(bsz, seq, d)
```

```python
import functools
import math

import jax
import jax.numpy as jnp
import numpy as np
from jax import lax
from jax.experimental import pallas as pl
from jax.experimental.pallas import tpu as pltpu

F32 = jnp.float32
BF16 = jnp.bfloat16

HEAD_DIM = 128
BR_WIDTH = 1024
N_BRANCH = 3
ROPE_THETA = 500000.0
ROPE_DIM = HEAD_DIM // 4
ROPE_HALF = ROPE_DIM // 2
EPS = 1e-6
NEG_INF = -1e30
FORCE_SCORE = 1e6
DIFF_HEADS = BR_WIDTH // (2 * HEAD_DIM)
DIFF_NORM_EPS = 1e-5
NSA_HEADS = BR_WIDTH // HEAD_DIM
NSA_GROUPS = 2
NSA_REP = NSA_HEADS // NSA_GROUPS
CMP_LEN = 32
CMP_STRIDE = 16
SLC_LEN = 64
SLC_SHIFT = SLC_LEN.bit_length() - 1
SLC_TOPK = 16
WINDOW = 512
SB_HEADS = BR_WIDTH // HEAD_DIM
NSA_KV = NSA_GROUPS * HEAD_DIM
ATTN_SCALE = HEAD_DIM ** -0.5
Q_PRESCALE = ATTN_SCALE * math.log2(math.e)

LANES = 128
SLC_BLOCKS_PAD = 128
WINDOW_ROWS = 128
UNSELECTED_BIAS = 2.0 ** 30
SB_UNDERFLOW = -104.0
VMEM_LIMIT = 56 * 1024 * 1024

_O_AQ, _O_AK, _O_AV, _O_AZ = 0, 1024, 2048, 3072
_O_NQ, _O_NKC, _O_NVC, _O_NKS, _O_NVS, _O_NKW, _O_NVW = 4096, 5120, 5376, 5632, 5888, 6144, 6400
_O_NG, _O_NZ = 6656, 6680
_O_CQ, _O_CK, _O_CV, _O_CZ, _O_MG = 7704, 8728, 9752, 10776, 11800
N_GATES = 3 * NSA_HEADS
P_AQ, P_AK, P_NQ, P_NKC, P_NKS, P_NKW = 0, 1024, 2048, 3072, 3328, 3584
ROPE_COLS = 4096
ROPE_PERM = (list(range(0, ROPE_HALF)) + list(range(ROPE_DIM, HEAD_DIM // 2 + ROPE_HALF))
             + list(range(ROPE_HALF, ROPE_DIM)) + list(range(HEAD_DIM // 2 + ROPE_HALF, HEAD_DIM)))
P_MG = ROPE_COLS


def _rest_offsets(d_model):
    r0 = ROPE_COLS + N_BRANCH * d_model
    offs = dict(AV=r0, AZ=r0 + 1024, NZ=r0 + 2048, CQ=r0 + 3072, CK=r0 + 4096, CV=r0 + 5120,
                CZ=r0 + 6144, NVC=r0 + 7168, NVS=r0 + 7424, NVW=r0 + 7680, NG=r0 + 7936)
    offs["END"] = r0 + 7936 + NSA_GROUPS * LANES
    return offs


PREP_ROWS = 2 * LANES
GATES_PER_GROUP = N_GATES // NSA_GROUPS
_ZERO, _COPY, _GATES, _COPY_ROPE = 0, 1, 2, 3


def _prep_plan(d_model, n_total):
    offs = _rest_offsets(d_model)
    pieces = [(P_AQ, _O_AQ, 1024), (P_AK, _O_AK, 1024), (P_NQ, _O_NQ, 1024), (P_NKC, _O_NKC, 256),
              (P_NKS, _O_NKS, 256), (P_NKW, _O_NKW, 256), (P_MG, _O_MG, N_BRANCH * d_model),
              (offs["AV"], _O_AV, 1024), (offs["AZ"], _O_AZ, 1024), (offs["NZ"], _O_NZ, 1024),
              (offs["CQ"], _O_CQ, 4096), (offs["NVC"], _O_NVC, 256), (offs["NVS"], _O_NVS, 256),
              (offs["NVW"], _O_NVW, 256), (offs["NG"], _O_NG, PREP_ROWS)]
    n_tiles = n_total // PREP_ROWS
    kind = np.full((n_tiles,), _ZERO, np.int32)
    src = np.zeros((n_tiles,), np.int32)
    for dest, first, width in pieces:
        for off in range(0, width, PREP_ROWS):
            assert (dest + off) % PREP_ROWS == 0 and (first + off) % 8 == 0
            kind[(dest + off) // PREP_ROWS] = _COPY_ROPE if dest + off < ROPE_COLS else _COPY
            src[(dest + off) // PREP_ROWS] = first + off
    assert NSA_GROUPS * LANES == PREP_ROWS
    kind[offs["NG"] // PREP_ROWS] = _GATES
    return jnp.asarray(kind), jnp.asarray(src // 8)


def _prep_kernel(kind_ref, src_ref, a_ref, o_ref):
    kind = kind_ref[pl.program_id(1)]

    @pl.when(kind == _ZERO)
    def _():
        o_ref[...] = jnp.zeros(o_ref.shape, BF16)

    @pl.when(kind == _COPY)
    def _():
        o_ref[...] = a_ref[0].astype(BF16)

    @pl.when(kind == _COPY_ROPE)
    def _():
        runs = ((0, 0, ROPE_HALF), (ROPE_HALF, ROPE_DIM, HEAD_DIM // 2 - ROPE_HALF),
                (HEAD_DIM // 2, ROPE_HALF, ROPE_HALF), (HEAD_DIM // 2 + ROPE_HALF, HEAD_DIM // 2 + ROPE_HALF,
                                                       HEAD_DIM // 2 - ROPE_HALF))
        for head in range(PREP_ROWS // HEAD_DIM):
            for dst, src, n in runs:
                o_ref[pl.ds(head * HEAD_DIM + dst, n), :] = a_ref[0, pl.ds(head * HEAD_DIM + src, n), :].astype(BF16)

    @pl.when(kind == _GATES)
    def _():
        o_ref[...] = jnp.zeros(o_ref.shape, BF16)
        for g in range(NSA_GROUPS):
            o_ref[pl.ds(g * LANES, GATES_PER_GROUP), :] = (
                a_ref[0, pl.ds(g * GATES_PER_GROUP, GATES_PER_GROUP), :].astype(BF16))


def _prep_w_in(w_in, d_model, n_total):
    wt = jnp.swapaxes(w_in, 1, 2)
    depth, n_in, d = wt.shape
    kind, src = _prep_plan(d_model, n_total)
    grid_spec = pltpu.PrefetchScalarGridSpec(
        num_scalar_prefetch=2,
        grid=(depth, n_total // PREP_ROWS),
        in_specs=[pl.BlockSpec((pl.Element(1), pl.Element(PREP_ROWS), pl.Element(d)),
                               lambda l, j, kind, src: (l, src[j] * 8, 0))],
        out_specs=pl.BlockSpec((None, PREP_ROWS, d), lambda l, j, kind, src: (l, j, 0)),
    )
    return pl.pallas_call(
        _prep_kernel,
        grid_spec=grid_spec,
        out_shape=jax.ShapeDtypeStruct((depth, n_total, d), BF16),
        compiler_params=_params("arbitrary", "arbitrary"),
        name="w_in_prep",
    )(kind, src, wt)


def _rope_tables(seq):
    pos = jnp.arange(seq, dtype=F32)
    inv = ROPE_THETA ** (-jnp.arange(0, ROPE_DIM, 2, dtype=F32) / ROPE_DIM)
    ang = pos[:, None] * inv[None, :]
    cos, sin = jnp.cos(ang), jnp.sin(ang)
    ones = jnp.ones((seq, HEAD_DIM // 2 - ROPE_HALF), F32)
    zeros = jnp.zeros((seq, HEAD_DIM // 2 - ROPE_HALF), F32)
    c = jnp.concatenate([cos, ones, cos, ones], axis=1)
    s = jnp.concatenate([-sin, zeros, sin, zeros], axis=1)
    return c, s


def _sigmoid(x):
    return 1.0 / (1.0 + jnp.exp(-x))


def _silu(x):
    return x * _sigmoid(x)


def _dot_nt(a, b):
    return lax.dot_general(a, b, (((1,), (1,)), ((), ())), preferred_element_type=F32)


def _dot(a, b):
    return jnp.dot(a, b, preferred_element_type=F32)


def _params(*sem):
    return pltpu.CompilerParams(dimension_semantics=sem, vmem_limit_bytes=VMEM_LIMIT)


def _ada_kernel(c_ref, w_ref, b_ref, o_ref):
    c = c_ref[...]
    o_ref[0] = _dot(_silu(c).astype(BF16), w_ref[0].astype(BF16)) + b_ref[0]


def _ada(c, w_ada, b_ada):
    depth, d, n3 = w_ada.shape
    bsz = c.shape[0]
    rows = max(8, bsz)
    cp = jnp.zeros((rows, d), F32).at[:bsz].set(c)
    tn = math.gcd(1024, n3)
    out = pl.pallas_call(
        _ada_kernel,
        grid=(depth, n3 // tn),
        in_specs=[pl.BlockSpec((rows, d), lambda l, j: (0, 0)),
                  pl.BlockSpec((1, d, tn), lambda l, j: (l, 0, j)),
                  pl.BlockSpec((1, 1, tn), lambda l, j: (l, 0, j))],
        out_specs=pl.BlockSpec((1, rows, tn), lambda l, j: (l, 0, j)),
        out_shape=jax.ShapeDtypeStruct((depth, rows, n3), F32),
        compiler_params=_params("arbitrary", "arbitrary"),
        name="ada_mod",
    )(cp, w_ada, b_ada.reshape(depth, 1, n3))
    return out[:, :bsz]


def _modulated_norm(x, g, scale, shift):
    y = x * lax.rsqrt(jnp.mean(x * x, axis=-1, keepdims=True) + EPS) * g
    return (y * (1.0 + scale) + shift).astype(BF16)


def _prenorm_kernel(x_ref, g_ref, sc_ref, sh_ref, h_ref):
    h_ref[...] = _modulated_norm(x_ref[...], g_ref[...], sc_ref[0], sh_ref[0])


def _prenorm(xf, g, scale, shift, seq, tm):
    rows, d = xf.shape
    tpb = seq // tm
    mod_spec = pl.BlockSpec((1, 1, d), lambda i: (i // tpb, 0, 0))
    return pl.pallas_call(
        _prenorm_kernel,
        grid=(rows // tm,),
        in_specs=[pl.BlockSpec((tm, d), lambda i: (i, 0)), pl.BlockSpec((1, d), lambda i: (0, 0)),
                  mod_spec, mod_spec],
        out_specs=pl.BlockSpec((tm, d), lambda i: (i, 0)),
        out_shape=jax.ShapeDtypeStruct((rows, d), BF16),
        compiler_params=_params("arbitrary"),
        name="pre_norm",
    )(xf, g.reshape(1, d), scale[:, None, :], shift[:, None, :])


def _inproj_kernel(h_ref, c_ref, s_ref, w_ref, o_ref, *, n_rope_tiles, tn):
    j = pl.program_id(1)
    y = _dot_nt(h_ref[...], w_ref[...])

    @pl.when(j < n_rope_tiles)
    def _():
        c, s = c_ref[...], s_ref[...]
        for hh in range(tn // HEAD_DIM):
            sl = slice(hh * HEAD_DIM, (hh + 1) * HEAD_DIM)
            yh = y[:, sl]
            r = yh * c + pltpu.roll(yh, HEAD_DIM // 2, 1) * s
            col = j * tn + hh * HEAD_DIM
            is_q = ((col >= P_AQ) & (col < P_AQ + BR_WIDTH)) | ((col >= P_NQ) & (col < P_NQ + BR_WIDTH))
            o_ref[:, sl] = (r * jnp.where(is_q, Q_PRESCALE, 1.0)).astype(BF16)

    @pl.when(j >= n_rope_tiles)
    def _():
        o_ref[...] = y.astype(BF16)


def _inproj(h, tables, wp_all, layer, seq, tm, tn):
    rows, d = h.shape
    n_total = wp_all.shape[1]
    tpb = seq // tm
    c, s = tables
    tab_spec = pl.BlockSpec((tm, HEAD_DIM), lambda i, j: (i % tpb, 0))
    return pl.pallas_call(
        functools.partial(_inproj_kernel, n_rope_tiles=ROPE_COLS // tn, tn=tn),
        grid=(rows // tm, n_total // tn),
        in_specs=[pl.BlockSpec((tm, d), lambda i, j: (i, 0)),
                  tab_spec, tab_spec,
                  pl.BlockSpec((None, tn, d), lambda i, j: (layer, j, 0))],
        out_specs=pl.BlockSpec((tm, tn), lambda i, j: (i, j)),
        out_shape=jax.ShapeDtypeStruct((rows, n_total), BF16),
        compiler_params=_params("arbitrary", "arbitrary"),
        name="in_proj",
    )(h, c, s, wp_all)


def _softmax_update(carry, s, v):
    m, l, acc = carry
    m_new = jnp.maximum(m, jnp.max(s, axis=-1, keepdims=True))
    alpha = jnp.exp2(m - m_new)
    p = jnp.exp2(s - m_new)
    l = alpha * l + jnp.sum(p, axis=-1, keepdims=True)
    acc = alpha * acc + _dot(p.astype(BF16), v)
    return m_new, l, acc


def _softmax_init(rows, width):
    return (jnp.full((rows, 1), NEG_INF, F32), jnp.zeros((rows, 1), F32),
            jnp.zeros((rows, width), F32))


def _diff_kernel(q_ref, k_ref, v_ref, z_ref, ng_ref, lq1_ref, lk1_ref, lq2_ref, lk2_ref, o_ref,
                 *, tq, tk, lam_init):
    i = pl.program_id(2)
    t0 = i * tq
    q = q_ref[...]
    qs = (q[:, :HEAD_DIM], q[:, HEAD_DIM:])
    n_full = t0 // tk

    def step(jt, carry, masked):
        k0 = pl.multiple_of(jt * tk, tk)
        k = k_ref[pl.ds(k0, tk), :]
        v = v_ref[pl.ds(k0, tk), :]
        s = [_dot_nt(qs[cc], k[:, cc * HEAD_DIM:(cc + 1) * HEAD_DIM]) for cc in range(2)]
        if masked:
            kpos = k0 + lax.broadcasted_iota(jnp.int32, (tq, tk), 1)
            tpos = t0 + lax.broadcasted_iota(jnp.int32, (tq, tk), 0)
            s = [jnp.where(kpos <= tpos, sc, NEG_INF) for sc in s]
        return tuple(_softmax_update(carry[cc], s[cc], v) for cc in range(2))

    init = (_softmax_init(tq, 2 * HEAD_DIM), _softmax_init(tq, 2 * HEAD_DIM))
    carry = lax.fori_loop(0, n_full, lambda jt, c: step(jt, c, False), init)
    carry = step(n_full, carry, True)

    lam = (jnp.exp(jnp.sum(lq1_ref[...] * lk1_ref[...], axis=-1, keepdims=True))
           - jnp.exp(jnp.sum(lq2_ref[...] * lk2_ref[...], axis=-1, keepdims=True)) + lam_init)
    o0 = carry[0][2] / carry[0][1]
    o1 = carry[1][2] / carry[1][1]
    o = o0 - lam * o1
    o = o * lax.rsqrt(jnp.mean(o * o, axis=-1, keepdims=True) + DIFF_NORM_EPS) * ng_ref[...]
    o = o * (1.0 - lam_init)
    o_ref[...] = (o * _silu(z_ref[...].astype(F32))).astype(BF16)


def _diff_attn(proj, offs, norm_g, lq1, lk1, lq2, lk2, layer_idx, bsz, seq, tq, tk):
    nq = seq // tq
    w = 2 * HEAD_DIM
    lam_init = 0.8 - 0.6 * math.exp(-0.3 * layer_idx)
    assert tk % tq == 0
    vec = pl.BlockSpec((1, HEAD_DIM), lambda b, h, i: (0, 0))
    return pl.pallas_call(
        functools.partial(_diff_kernel, tq=tq, tk=tk, lam_init=lam_init),
        grid=(bsz, DIFF_HEADS, nq),
        in_specs=[pl.BlockSpec((tq, w), lambda b, h, i: (b * nq + i, P_AQ // w + h)),
                  pl.BlockSpec((seq, w), lambda b, h, i: (b, P_AK // w + h)),
                  pl.BlockSpec((seq, w), lambda b, h, i: (b, offs["AV"] // w + h)),
                  pl.BlockSpec((tq, w), lambda b, h, i: (b * nq + i, offs["AZ"] // w + h)),
                  pl.BlockSpec((1, w), lambda b, h, i: (0, 0)),
                  vec, vec, vec, vec],
        out_specs=pl.BlockSpec((tq, w), lambda b, h, i: (b * nq + i, h)),
        out_shape=jax.ShapeDtypeStruct((bsz * seq, BR_WIDTH), BF16),
        compiler_params=_params("arbitrary", "arbitrary", "arbitrary"),
        name="diff_attn",
    )(proj, proj, proj, proj, norm_g.reshape(1, w), lq1.reshape(1, -1), lk1.reshape(1, -1),
      lq2.reshape(1, -1), lk2.reshape(1, -1))


def _sb_kernel(q_ref, k_ref, v_ref, z_ref, o_ref, *, tq, tk):
    i = pl.program_id(2)
    t0 = i * tq
    n_full = t0 // tk
    qh = (q_ref[pl.ds(0, tk), :], q_ref[pl.ds(tk, tk), :])
    upper2 = jnp.where((lax.broadcasted_iota(jnp.int32, (2 * tk, tk), 0) & (tk - 1))
                       > lax.broadcasted_iota(jnp.int32, (2 * tk, tk), 1), 1.0, 0.0).astype(BF16)
    strict = (lax.broadcasted_iota(jnp.int32, (tk, tk), 1) < lax.broadcasted_iota(jnp.int32, (tk, tk), 0))

    def steps(items):
        kv = []
        for _, jt, _, _ in items:
            k0 = pl.multiple_of(jt * tk, tk)
            kv.append((k_ref[pl.ds(k0, tk), :], v_ref[pl.ds(k0, tk), :]))
        z = [_dot_nt(qh[it[0]], kv[n][0]) * ATTN_SCALE for n, it in enumerate(items)]
        log_1m = [-(jnp.maximum(zz, 0.0) + jnp.log(1.0 + jnp.exp(-jnp.abs(zz)))) for zz in z]
        log_1m = [jnp.where(strict, lm, 0.0) if it[3] else lm for lm, it in zip(log_1m, items)]
        stacked = []
        for lm in log_1m:
            hi = lm.astype(BF16)
            stacked.append(jnp.concatenate([hi, (lm - hi.astype(F32)).astype(BF16)], axis=1))
        after = [_dot(st, upper2) + it[2][0] for st, it in zip(stacked, items)]
        a = [jnp.exp(zz + lm + af) for zz, lm, af in zip(z, log_1m, after)]
        a = [jnp.where(strict, aa, 0.0) if it[3] else aa for aa, it in zip(a, items)]
        pv = [_dot(aa.astype(BF16), kv[n][1]) for n, aa in enumerate(a)]
        return [(it[2][0] + jnp.sum(lm, axis=-1, keepdims=True), it[2][1] + p)
                for it, lm, p in zip(items, log_1m, pv)]

    zero = (jnp.zeros((tk, 1), F32), jnp.zeros((tk, HEAD_DIM), F32))
    c_b, c_a = steps([(1, n_full + 1, zero, True), (0, n_full, zero, True)])
    (c_b,) = steps([(1, n_full, c_b, False)])

    def alive(ca, cb):
        return jnp.maximum(jnp.max(ca[0]), jnp.max(cb[0]))

    def cond(c):
        return (c[0] < n_full) & (c[1] > SB_UNDERFLOW)

    def body(c):
        jt = n_full - 1 - c[0]
        ca, cb = steps([(0, jt, c[2], False), (1, jt, c[3], False)])
        return c[0] + 1, alive(ca, cb), ca, cb

    _, _, c_a, c_b = lax.while_loop(cond, body, (jnp.int32(0), alive(c_a, c_b), c_a, c_b))
    zg = z_ref[...].astype(F32)
    o_ref[pl.ds(0, tk), :] = (c_a[1] * _silu(zg[:tk])).astype(BF16)
    o_ref[pl.ds(tk, tk), :] = (c_b[1] * _silu(zg[tk:])).astype(BF16)


def _sb_attn(proj, offs, bsz, seq, tq, tk):
    nq = seq // tq
    w = HEAD_DIM
    assert tq == 2 * tk
    return pl.pallas_call(
        functools.partial(_sb_kernel, tq=tq, tk=tk),
        grid=(bsz, SB_HEADS, nq),
        in_specs=[pl.BlockSpec((tq, w), lambda b, h, i: (b * nq + i, offs["CQ"] // w + h)),
                  pl.BlockSpec((seq, w), lambda b, h, i: (b, offs["CK"] // w + h)),
                  pl.BlockSpec((seq, w), lambda b, h, i: (b, offs["CV"] // w + h)),
                  pl.BlockSpec((tq, w), lambda b, h, i: (b * nq + i, offs["CZ"] // w + h))],
        out_specs=pl.BlockSpec((tq, w), lambda b, h, i: (b * nq + i, h)),
        out_shape=jax.ShapeDtypeStruct((bsz * seq, BR_WIDTH), BF16),
        compiler_params=_params("arbitrary", "arbitrary", "arbitrary"),
        name="stick_breaking",
    )(proj, proj, proj, proj)


def _compress_kernel(x_ref, pe_ref, w1_ref, w2_ref, o_ref, xs_ref, *, seq):
    nc = seq // CMP_STRIDE
    xs_ref[pl.ds(0, seq), :] = x_ref[...].astype(F32)
    xs_ref[pl.ds(seq, CMP_STRIDE), :] = jnp.zeros((CMP_STRIDE, HEAD_DIM), F32)
    hid = None
    for l in range(CMP_LEN):
        rows = xs_ref[pl.ds(l, nc, stride=CMP_STRIDE), :] + pe_ref[l:l + 1, :]
        t = _dot(rows.astype(BF16), w1_ref[l * HEAD_DIM:(l + 1) * HEAD_DIM, :])
        hid = t if hid is None else hid + t
    o_ref[0, 0] = _dot(_silu(hid).astype(BF16), w2_ref[...]).astype(BF16)


def _compress(proj, col_off, pe, w1, w2, bsz, seq):
    nc = seq // CMP_STRIDE
    w = HEAD_DIM
    return pl.pallas_call(
        functools.partial(_compress_kernel, seq=seq),
        grid=(bsz, NSA_GROUPS),
        in_specs=[pl.BlockSpec((seq, w), lambda b, g: (b, col_off // w + g)),
                  pl.BlockSpec((CMP_LEN, w), lambda b, g: (0, 0)),
                  pl.BlockSpec((CMP_LEN * w, w), lambda b, g: (0, 0)),
                  pl.BlockSpec((w, w), lambda b, g: (0, 0))],
        out_specs=pl.BlockSpec((1, 1, nc, w), lambda b, g: (b, g, 0, 0)),
        out_shape=jax.ShapeDtypeStruct((bsz, NSA_GROUPS, nc, w), BF16),
        scratch_shapes=[pltpu.VMEM((seq + CMP_STRIDE, w), F32)],
        compiler_params=_params("arbitrary", "arbitrary"),
        name="nsa_compress",
    )(proj, pe, w1.astype(BF16), w2.astype(BF16))


def _nsa_kernel(q_ref, kc_ref, vc_ref, ks_ref, vs_ref, kw_ref, vw_ref, g_ref, z_ref, o_ref, imp_ref,
                *, tq, tk, seq):
    i = pl.program_id(2)
    t0 = i * tq
    nc = seq // CMP_STRIDE
    nb = seq // SLC_LEN
    nbp = SLC_BLOCKS_PAD
    rep = NSA_REP
    q = q_ref[...]
    qh = [q[:, r * HEAD_DIM:(r + 1) * HEAD_DIM] for r in range(rep)]

    kc, vc = kc_ref[0, 0], vc_ref[0, 0]
    tpos_c = t0 + lax.broadcasted_iota(jnp.int32, (tq, nc), 0)
    ncol = lax.broadcasted_iota(jnp.int32, (tq, nc), 1)
    cvalid = CMP_STRIDE * ncol + (CMP_LEN - 1) <= tpos_c
    psum = jnp.zeros((tq, nc), F32)
    o_cmp = []
    for r in range(rep):
        s = jnp.where(cvalid, _dot_nt(qh[r], kc), NEG_INF)
        e = jnp.where(cvalid, jnp.exp2(s - jnp.max(s, axis=-1, keepdims=True)), 0.0)
        den = jnp.sum(e, axis=-1, keepdims=True)
        p = e / jnp.where(den > 0.0, den, 1.0)
        psum = psum + p
        o_cmp.append(_dot(p.astype(BF16), vc))

    nb8 = imp_ref.shape[0]
    jrow = lax.broadcasted_iota(jnp.int32, (nb8, nc), 0)
    ncol2 = lax.broadcasted_iota(jnp.int32, (nb8, nc), 1)
    overlap = jnp.where((CMP_STRIDE * ncol2 < SLC_LEN * jrow + SLC_LEN)
                        & (CMP_STRIDE * ncol2 + CMP_LEN > SLC_LEN * jrow)
                        & (ncol2 < nc - 1) & (jrow < nb), 1.0, 0.0).astype(BF16)
    p1 = psum.astype(BF16)
    r1 = psum - p1.astype(F32)
    p2 = r1.astype(BF16)
    p3 = (r1 - p2.astype(F32)).astype(BF16)
    imp = _dot_nt(overlap, p1) + _dot_nt(overlap, p2) + _dot_nt(overlap, p3)
    jt_ = lax.broadcasted_iota(jnp.int32, (nb8, tq), 0)
    tblk = jnp.right_shift(t0 + lax.broadcasted_iota(jnp.int32, (nb8, tq), 1), SLC_SHIFT)
    imp = jnp.where((jt_ == tblk) | (jt_ == 0), FORCE_SCORE, imp)
    imp = jnp.where(jt_ <= tblk, imp, NEG_INF)
    imp_ref[...] = imp

    def rank_body(jp, cnt):
        row = imp_ref[pl.ds(jp, 1), :]
        tie = jnp.where(jp < jt_, 1.0, 0.0)
        return cnt + jnp.where(row > imp, 1.0, jnp.where(row == imp, tie, 0.0))

    n_live = jnp.minimum((t0 + tq - 1) // SLC_LEN + 1, nb)
    cnt = lax.fori_loop(0, n_live, rank_body, jnp.zeros((nb8, tq), F32))
    keep = (cnt < float(min(SLC_TOPK, nb))) & (imp > 0.5 * NEG_INF)
    unsel_t = jnp.where(keep, 0.0, 1.0)
    if nbp > nb8:
        unsel_t = jnp.concatenate([unsel_t, jnp.zeros((nbp - nb8, tq), F32)], axis=0)
    unselected = unsel_t.T.astype(BF16)

    rows = rep * tq
    q_sel = jnp.concatenate([jnp.concatenate([qh[r], unselected], axis=1) for r in range(rep)], axis=0)

    def slc_step(jt, carry, masked):
        k0 = pl.multiple_of(jt * tk, tk)
        k = ks_ref[pl.ds(k0, tk), :]
        v = vs_ref[pl.ds(k0, tk), :]
        kblk = jnp.right_shift(k0 + lax.broadcasted_iota(jnp.int32, (tk, nbp), 0), SLC_SHIFT)
        bias = jnp.where(lax.broadcasted_iota(jnp.int32, (tk, nbp), 1) == kblk,
                         -UNSELECTED_BIAS, 0.0).astype(BF16)
        s = _dot_nt(q_sel, jnp.concatenate([k, bias], axis=1))
        if masked:
            causal = (k0 + lax.broadcasted_iota(jnp.int32, (tq, tk), 1)
                      <= t0 + lax.broadcasted_iota(jnp.int32, (tq, tk), 0))
            s = s + jnp.concatenate([jnp.where(causal, 0.0, NEG_INF)] * rep, axis=0)
        return _softmax_update(carry, s, v)

    n_full = t0 // tk
    carry = lax.fori_loop(0, n_full, lambda jt, c: slc_step(jt, c, False), _softmax_init(rows, HEAD_DIM))
    m_s, l_s, acc_s = slc_step(n_full, carry, True)
    o_slc = acc_s / l_s

    tw = min(tq, WINDOW_ROWS)
    span = min(WINDOW + tw, seq)
    o_win = []
    for w in range(tq // tw):
        r0 = t0 + w * tw
        start = pl.multiple_of(jnp.maximum(r0 - WINDOW, 0), tw)
        kpos = start + lax.broadcasted_iota(jnp.int32, (tw, span), 1)
        row_w = r0 + lax.broadcasted_iota(jnp.int32, (tw, span), 0)
        in_window = jnp.where(kpos <= row_w, jnp.where(kpos > row_w - WINDOW, 0.0, NEG_INF), NEG_INF)
        q_w = jnp.concatenate([qh[r][w * tw:(w + 1) * tw] for r in range(rep)], axis=0)
        s = _dot_nt(q_w, kw_ref[pl.ds(start, span), :]) + jnp.concatenate([in_window] * rep, axis=0)
        p = jnp.exp2(s - jnp.max(s, axis=-1, keepdims=True))
        o_win.append(_dot(p.astype(BF16), vw_ref[pl.ds(start, span), :]) / jnp.sum(p, axis=-1, keepdims=True))

    gates = _sigmoid(g_ref[...].astype(F32))
    z = z_ref[...].astype(F32)
    for r in range(rep):
        rs = slice(r * tq, (r + 1) * tq)
        cs = slice(r * HEAD_DIM, (r + 1) * HEAD_DIM)
        o_win_r = jnp.concatenate([o[r * tw:(r + 1) * tw] for o in o_win], axis=0)
        o = (gates[:, 3 * r:3 * r + 1] * o_cmp[r] + gates[:, 3 * r + 1:3 * r + 2] * o_slc[rs]
             + gates[:, 3 * r + 2:3 * r + 3] * o_win_r)
        o_ref[:, cs] = (o * _silu(z[:, cs])).astype(BF16)


def _nsa_attn(proj, offs, kcmp, vcmp, bsz, seq, tq, tk):
    nq = seq // tq
    nc = seq // CMP_STRIDE
    w = HEAD_DIM
    gw = NSA_REP * HEAD_DIM
    assert seq // SLC_LEN <= SLC_BLOCKS_PAD and tq & (tq - 1) == 0 and tk % tq == 0
    kv = lambda off: pl.BlockSpec((seq, w), lambda b, g, i: (b, off // w + g))
    cmp_spec = pl.BlockSpec((1, 1, nc, w), lambda b, g, i: (b, g, 0, 0))
    return pl.pallas_call(
        functools.partial(_nsa_kernel, tq=tq, tk=tk, seq=seq),
        grid=(bsz, NSA_GROUPS, nq),
        in_specs=[pl.BlockSpec((tq, gw), lambda b, g, i: (b * nq + i, P_NQ // gw + g)),
                  cmp_spec, cmp_spec,
                  kv(P_NKS), kv(offs["NVS"]), kv(P_NKW), kv(offs["NVW"]),
                  pl.BlockSpec((tq, LANES), lambda b, g, i: (b * nq + i, offs["NG"] // LANES + g)),
                  pl.BlockSpec((tq, gw), lambda b, g, i: (b * nq + i, offs["NZ"] // gw + g))],
        out_specs=pl.BlockSpec((tq, gw), lambda b, g, i: (b * nq + i, g)),
        out_shape=jax.ShapeDtypeStruct((bsz * seq, BR_WIDTH), BF16),
        scratch_shapes=[pltpu.VMEM((-(-(seq // SLC_LEN) // 8) * 8, tq), F32)],
        compiler_params=_params("arbitrary", "arbitrary", "arbitrary"),
        name="nsa_attn",
    )(proj, kcmp, vcmp, proj, proj, proj, proj, proj, proj)


def _merge_kernel(x_ref, ya_ref, yb_ref, yc_ref, mg0_ref, mg1_ref, mg2_ref, wb_ref, wo_ref, gp_ref,
                  gate_ref, *rest, emit_h):
    merged = None
    for n, (y_ref, mg_ref) in enumerate(((ya_ref, mg0_ref), (yb_ref, mg1_ref), (yc_ref, mg2_ref))):
        t = _dot(y_ref[...], wb_ref[n]) * _sigmoid(mg_ref[...].astype(F32))
        merged = t if merged is None else merged + t
    o = _dot(merged.astype(BF16), wo_ref[...])
    o = o * lax.rsqrt(jnp.mean(o * o, axis=-1, keepdims=True) + EPS) * gp_ref[...]
    x_new = x_ref[...] + gate_ref[0] * o
    if emit_h:
        g_next_ref, sc_next_ref, sh_next_ref, o_ref, h_ref = rest
        h_ref[...] = _modulated_norm(x_new, g_next_ref[...], sc_next_ref[0], sh_next_ref[0])
    else:
        (o_ref,) = rest
    o_ref[...] = x_new


def _merge(xf, ya, yb, yc, proj, wb_all, wo_all, layer, g_post, gate, next_norm, seq, tm):
    rows, d = xf.shape
    tpb = seq // tm
    row = lambda w_: pl.BlockSpec((tm, w_), lambda i: (i, 0))
    mg = lambda n: pl.BlockSpec((tm, d), lambda i: (i, P_MG // d + n))
    vec = pl.BlockSpec((1, d), lambda i: (0, 0))
    mod_spec = pl.BlockSpec((1, 1, d), lambda i: (i // tpb, 0, 0))
    const = pl.Buffered(1)
    emit_h = next_norm is not None
    in_specs = [row(d), row(BR_WIDTH), row(BR_WIDTH), row(BR_WIDTH), mg(0), mg(1), mg(2),
                pl.BlockSpec((None, N_BRANCH, BR_WIDTH, d), lambda i: (layer, 0, 0, 0), pipeline_mode=const),
                pl.BlockSpec((None, d, d), lambda i: (layer, 0, 0), pipeline_mode=const), vec, mod_spec]
    args = [xf, ya, yb, yc, proj, proj, proj, wb_all, wo_all, g_post.reshape(1, d),
            gate[:, None, :]]
    out_specs, out_shape = row(d), jax.ShapeDtypeStruct((rows, d), F32)
    if emit_h:
        g_next, sc_next, sh_next = next_norm
        in_specs += [vec, mod_spec, mod_spec]
        args += [g_next.reshape(1, d), sc_next[:, None, :], sh_next[:, None, :]]
        out_specs, out_shape = [out_specs, row(d)], [out_shape, jax.ShapeDtypeStruct((rows, d), BF16)]
    out = pl.pallas_call(
        functools.partial(_merge_kernel, emit_h=emit_h),
        grid=(rows // tm,),
        in_specs=in_specs,
        out_specs=out_specs,
        out_shape=out_shape,
        compiler_params=_params("arbitrary"),
        name="merge_out",
    )(*args)
    return (out[0], out[1]) if emit_h else (out, None)


def _tiles(seq, d_model):
    return dict(tm_in=min(1024, seq), tn_in=2048, tm_merge=min(256, seq),
                diff=(min(1024, seq), min(1024, seq)),
                sb=(min(512, seq), min(512, seq) // 2),
                nsa=(min(512, seq), min(512, seq)))


def kernel(x, c, norm_pre_g, norm_post_g, w_ada, b_ada, w_in, lambda_q1, lambda_k1, lambda_q2,
           lambda_k2, diff_norm_g, cmp_pe_k, cmp_w1_k, cmp_w2_k, cmp_pe_v, cmp_w1_v, cmp_w2_v,
           w_branch, w_out):
    bsz, seq, d = x.shape
    depth = w_in.shape[0]
    t = _tiles(seq, d)
    offs = _rest_offsets(d)
    assert d % 512 == 0 and ROPE_COLS % d == 0 and seq % t["tm_in"] == 0 and seq % 128 == 0
    n_total = -(-offs["END"] // t["tn_in"]) * t["tn_in"]
    tables = _rope_tables(seq)
    mod = _ada(c, w_ada, b_ada)
    xf = x.reshape(bsz * seq, d)
    mods = [jnp.split(mod[l], 3, axis=-1) for l in range(depth)]
    h = _prenorm(xf, norm_pre_g[0], mods[0][1], mods[0][0], seq, t["tm_merge"])
    wb_all, wo_all = w_branch.astype(BF16), w_out.astype(BF16)
    wp_all = _prep_w_in(w_in, d, n_total)
    for l in range(depth):
        proj = _inproj(h, tables, wp_all, l, seq, t["tm_in"], t["tn_in"])
        ya = _diff_attn(proj, offs, diff_norm_g[l], lambda_q1[l], lambda_k1[l], lambda_q2[l],
                        lambda_k2[l], l, bsz, seq, *t["diff"])
        perm = jnp.asarray(ROPE_PERM)
        w1_k = cmp_w1_k[l].reshape(CMP_LEN, HEAD_DIM, HEAD_DIM)[:, perm, :].reshape(CMP_LEN * HEAD_DIM, HEAD_DIM)
        kcmp = _compress(proj, P_NKC, cmp_pe_k[l][:, perm], w1_k, cmp_w2_k[l][:, perm], bsz, seq)
        vcmp = _compress(proj, offs["NVC"], cmp_pe_v[l], cmp_w1_v[l], cmp_w2_v[l], bsz, seq)
        yb = _nsa_attn(proj, offs, kcmp, vcmp, bsz, seq, *t["nsa"])
        yc = _sb_attn(proj, offs, bsz, seq, *t["sb"])
        next_norm = (norm_pre_g[l + 1], mods[l + 1][1], mods[l + 1][0]) if l + 1 < depth else None
        xf, h = _merge(xf, ya, yb, yc, proj, wb_all, wo_all, l, norm_post_g[l], mods[l][2],
                       next_norm, seq, t["tm_merge"])
    return xf.reshape(bsz, seq, d)
```

```python
import functools
import math

import jax
import jax.numpy as jnp
import numpy as np
from jax import lax
from jax.experimental import pallas as pl
from jax.experimental.pallas import tpu as pltpu

F32 = jnp.float32
BF16 = jnp.bfloat16

HEAD_DIM = 128
BR_WIDTH = 1024
N_BRANCH = 3
ROPE_THETA = 500000.0
ROPE_DIM = HEAD_DIM // 4
ROPE_HALF = ROPE_DIM // 2
EPS = 1e-6
NEG_INF = -1e30
FORCE_SCORE = 1e6
DIFF_HEADS = BR_WIDTH // (2 * HEAD_DIM)
DIFF_NORM_EPS = 1e-5
NSA_HEADS = BR_WIDTH // HEAD_DIM
NSA_GROUPS = 2
NSA_REP = NSA_HEADS // NSA_GROUPS
CMP_LEN = 32
CMP_STRIDE = 16
SLC_LEN = 64
SLC_SHIFT = SLC_LEN.bit_length() - 1
SLC_TOPK = 16
WINDOW = 512
SB_HEADS = BR_WIDTH // HEAD_DIM
SB_HEADS_PER_STEP = 2
NSA_KV = NSA_GROUPS * HEAD_DIM
ATTN_SCALE = HEAD_DIM ** -0.5
Q_PRESCALE = ATTN_SCALE * math.log2(math.e)

LANES = 128
SLC_BLOCKS_PAD = 128
WINDOW_ROWS = 128
UNSELECTED_BIAS = 2.0 ** 30
SB_UNDERFLOW = -104.0
VMEM_LIMIT = 56 * 1024 * 1024

_O_AQ, _O_AK, _O_AV, _O_AZ = 0, 1024, 2048, 3072
_O_NQ, _O_NKC, _O_NVC, _O_NKS, _O_NVS, _O_NKW, _O_NVW = 4096, 5120, 5376, 5632, 5888, 6144, 6400
_O_NG, _O_NZ = 6656, 6680
_O_CQ, _O_CK, _O_CV, _O_CZ, _O_MG = 7704, 8728, 9752, 10776, 11800
N_GATES = 3 * NSA_HEADS
P_AQ, P_AK, P_NQ, P_NKC, P_NKS, P_NKW = 0, 1024, 2048, 3072, 3328, 3584
ROPE_COLS = 4096
ROPE_PERM = (list(range(0, ROPE_HALF)) + list(range(ROPE_DIM, HEAD_DIM // 2 + ROPE_HALF))
             + list(range(ROPE_HALF, ROPE_DIM)) + list(range(HEAD_DIM // 2 + ROPE_HALF, HEAD_DIM)))
P_MG = ROPE_COLS


def _rest_offsets(d_model):
    r0 = ROPE_COLS + N_BRANCH * d_model
    offs = dict(AV=r0, AZ=r0 + 1024, NZ=r0 + 2048, CQ=r0 + 3072, CK=r0 + 4096, CV=r0 + 5120,
                CZ=r0 + 6144, NVC=r0 + 7168, NVS=r0 + 7424, NVW=r0 + 7680, NG=r0 + 7936)
    offs["END"] = r0 + 7936 + NSA_GROUPS * LANES
    return offs


PREP_ROWS = 2 * LANES
GATES_PER_GROUP = N_GATES // NSA_GROUPS
_ZERO, _COPY, _GATES, _COPY_ROPE = 0, 1, 2, 3


def _prep_plan(d_model, n_total):
    offs = _rest_offsets(d_model)
    pieces = [(P_AQ, _O_AQ, 1024), (P_AK, _O_AK, 1024), (P_NQ, _O_NQ, 1024), (P_NKC, _O_NKC, 256),
              (P_NKS, _O_NKS, 256), (P_NKW, _O_NKW, 256), (P_MG, _O_MG, N_BRANCH * d_model),
              (offs["AV"], _O_AV, 1024), (offs["AZ"], _O_AZ, 1024), (offs["NZ"], _O_NZ, 1024),
              (offs["CQ"], _O_CQ, 4096), (offs["NVC"], _O_NVC, 256), (offs["NVS"], _O_NVS, 256),
              (offs["NVW"], _O_NVW, 256), (offs["NG"], _O_NG, PREP_ROWS)]
    n_tiles = n_total // PREP_ROWS
    kind = np.full((n_tiles,), _ZERO, np.int32)
    src = np.zeros((n_tiles,), np.int32)
    for dest, first, width in pieces:
        for off in range(0, width, PREP_ROWS):
            assert (dest + off) % PREP_ROWS == 0 and (first + off) % 8 == 0
            kind[(dest + off) // PREP_ROWS] = _COPY_ROPE if dest + off < ROPE_COLS else _COPY
            src[(dest + off) // PREP_ROWS] = first + off
    assert NSA_GROUPS * LANES == PREP_ROWS
    kind[offs["NG"] // PREP_ROWS] = _GATES
    return jnp.asarray(kind), jnp.asarray(src // 8)


def _prep_kernel(kind_ref, src_ref, a_ref, o_ref):
    kind = kind_ref[pl.program_id(1)]

    @pl.when(kind == _ZERO)
    def _():
        o_ref[...] = jnp.zeros(o_ref.shape, BF16)

    @pl.when(kind == _COPY)
    def _():
        o_ref[...] = a_ref[0].astype(BF16)

    @pl.when(kind == _COPY_ROPE)
    def _():
        runs = ((0, 0, ROPE_HALF), (ROPE_HALF, ROPE_DIM, HEAD_DIM // 2 - ROPE_HALF),
                (HEAD_DIM // 2, ROPE_HALF, ROPE_HALF), (HEAD_DIM // 2 + ROPE_HALF, HEAD_DIM // 2 + ROPE_HALF,
                                                       HEAD_DIM // 2 - ROPE_HALF))
        for head in range(PREP_ROWS // HEAD_DIM):
            for dst, src, n in runs:
                o_ref[pl.ds(head * HEAD_DIM + dst, n), :] = a_ref[0, pl.ds(head * HEAD_DIM + src, n), :].astype(BF16)

    @pl.when(kind == _GATES)
    def _():
        o_ref[...] = jnp.zeros(o_ref.shape, BF16)
        for g in range(NSA_GROUPS):
            o_ref[pl.ds(g * LANES, GATES_PER_GROUP), :] = (
                a_ref[0, pl.ds(g * GATES_PER_GROUP, GATES_PER_GROUP), :].astype(BF16))


def _prep_w_in(w_in, d_model, n_total):
    wt = jnp.swapaxes(w_in, 1, 2)
    depth, n_in, d = wt.shape
    kind, src = _prep_plan(d_model, n_total)
    grid_spec = pltpu.PrefetchScalarGridSpec(
        num_scalar_prefetch=2,
        grid=(depth, n_total // PREP_ROWS),
        in_specs=[pl.BlockSpec((pl.Element(1), pl.Element(PREP_ROWS), pl.Element(d)),
                               lambda l, j, kind, src: (l, src[j] * 8, 0))],
        out_specs=pl.BlockSpec((None, PREP_ROWS, d), lambda l, j, kind, src: (l, j, 0)),
    )
    return pl.pallas_call(
        _prep_kernel,
        grid_spec=grid_spec,
        out_shape=jax.ShapeDtypeStruct((depth, n_total, d), BF16),
        compiler_params=_params("arbitrary", "arbitrary"),
        name="w_in_prep",
    )(kind, src, wt)


def _rope_tables(seq):
    pos = jnp.arange(seq, dtype=F32)
    inv = ROPE_THETA ** (-jnp.arange(0, ROPE_DIM, 2, dtype=F32) / ROPE_DIM)
    ang = pos[:, None] * inv[None, :]
    cos, sin = jnp.cos(ang), jnp.sin(ang)
    ones = jnp.ones((seq, HEAD_DIM // 2 - ROPE_HALF), F32)
    zeros = jnp.zeros((seq, HEAD_DIM // 2 - ROPE_HALF), F32)
    c = jnp.concatenate([cos, ones, cos, ones], axis=1)
    s = jnp.concatenate([-sin, zeros, sin, zeros], axis=1)
    return c, s


def _sigmoid(x):
    return 1.0 / (1.0 + jnp.exp(-x))


def _silu(x):
    return x * _sigmoid(x)


def _dot_nt(a, b):
    return lax.dot_general(a, b, (((1,), (1,)), ((), ())), preferred_element_type=F32)


def _dot(a, b):
    return jnp.dot(a, b, preferred_element_type=F32)


def _params(*sem):
    return pltpu.CompilerParams(dimension_semantics=sem, vmem_limit_bytes=VMEM_LIMIT)


def _ada_kernel(c_ref, w_ref, b_ref, o_ref):
    c = c_ref[...]
    o_ref[0] = _dot(_silu(c).astype(BF16), w_ref[0].astype(BF16)) + b_ref[0]


def _ada(c, w_ada, b_ada):
    depth, d, n3 = w_ada.shape
    bsz = c.shape[0]
    rows = max(8, bsz)
    cp = jnp.zeros((rows, d), F32).at[:bsz].set(c)
    tn = math.gcd(1024, n3)
    out = pl.pallas_call(
        _ada_kernel,
        grid=(depth, n3 // tn),
        in_specs=[pl.BlockSpec((rows, d), lambda l, j: (0, 0)),
                  pl.BlockSpec((1, d, tn), lambda l, j: (l, 0, j)),
                  pl.BlockSpec((1, 1, tn), lambda l, j: (l, 0, j))],
        out_specs=pl.BlockSpec((1, rows, tn), lambda l, j: (l, 0, j)),
        out_shape=jax.ShapeDtypeStruct((depth, rows, n3), F32),
        compiler_params=_params("arbitrary", "arbitrary"),
        name="ada_mod",
    )(cp, w_ada, b_ada.reshape(depth, 1, n3))
    return out[:, :bsz]


def _modulated_norm(x, g, scale, shift):
    y = x * lax.rsqrt(jnp.mean(x * x, axis=-1, keepdims=True) + EPS) * g
    return (y * (1.0 + scale) + shift).astype(BF16)


def _prenorm_kernel(x_ref, g_ref, sc_ref, sh_ref, h_ref):
    h_ref[...] = _modulated_norm(x_ref[...], g_ref[...], sc_ref[0], sh_ref[0])


def _prenorm(xf, g, scale, shift, seq, tm):
    rows, d = xf.shape
    tpb = seq // tm
    mod_spec = pl.BlockSpec((1, 1, d), lambda i: (i // tpb, 0, 0))
    return pl.pallas_call(
        _prenorm_kernel,
        grid=(rows // tm,),
        in_specs=[pl.BlockSpec((tm, d), lambda i: (i, 0)), pl.BlockSpec((1, d), lambda i: (0, 0)),
                  mod_spec, mod_spec],
        out_specs=pl.BlockSpec((tm, d), lambda i: (i, 0)),
        out_shape=jax.ShapeDtypeStruct((rows, d), BF16),
        compiler_params=_params("arbitrary"),
        name="pre_norm",
    )(xf, g.reshape(1, d), scale[:, None, :], shift[:, None, :])


def _inproj_kernel(h_ref, c_ref, s_ref, w_ref, o_ref, *, n_rope_tiles, tn):
    j = pl.program_id(1)
    y = _dot_nt(h_ref[...], w_ref[...])

    @pl.when(j < n_rope_tiles)
    def _():
        c, s = c_ref[...], s_ref[...]
        for hh in range(tn // HEAD_DIM):
            sl = slice(hh * HEAD_DIM, (hh + 1) * HEAD_DIM)
            yh = y[:, sl]
            r = yh * c + pltpu.roll(yh, HEAD_DIM // 2, 1) * s
            col = j * tn + hh * HEAD_DIM
            is_q = ((col >= P_AQ) & (col < P_AQ + BR_WIDTH)) | ((col >= P_NQ) & (col < P_NQ + BR_WIDTH))
            o_ref[:, sl] = (r * jnp.where(is_q, Q_PRESCALE, 1.0)).astype(BF16)

    @pl.when(j >= n_rope_tiles)
    def _():
        o_ref[...] = y.astype(BF16)


def _inproj(h, tables, wp_all, layer, seq, tm, tn):
    rows, d = h.shape
    n_total = wp_all.shape[1]
    tpb = seq // tm
    c, s = tables
    tab_spec = pl.BlockSpec((tm, HEAD_DIM), lambda i, j: (i % tpb, 0))
    return pl.pallas_call(
        functools.partial(_inproj_kernel, n_rope_tiles=ROPE_COLS // tn, tn=tn),
        grid=(rows // tm, n_total // tn),
        in_specs=[pl.BlockSpec((tm, d), lambda i, j: (i, 0)),
                  tab_spec, tab_spec,
                  pl.BlockSpec((None, tn, d), lambda i, j: (layer, j, 0))],
        out_specs=pl.BlockSpec((tm, tn), lambda i, j: (i, j)),
        out_shape=jax.ShapeDtypeStruct((rows, n_total), BF16),
        compiler_params=_params("arbitrary", "arbitrary"),
        name="in_proj",
    )(h, c, s, wp_all)


def _softmax_update(carry, s, v):
    m, l, acc = carry
    m_new = jnp.maximum(m, jnp.max(s, axis=-1, keepdims=True))
    alpha = jnp.exp2(m - m_new)
    p = jnp.exp2(s - m_new)
    l = alpha * l + jnp.sum(p, axis=-1, keepdims=True)
    acc = alpha * acc + _dot(p.astype(BF16), v)
    return m_new, l, acc


def _softmax_init(rows, width):
    return (jnp.full((rows, 1), NEG_INF, F32), jnp.zeros((rows, 1), F32),
            jnp.zeros((rows, width), F32))


def _diff_kernel(q_ref, k_ref, v_ref, z_ref, ng_ref, lq1_ref, lk1_ref, lq2_ref, lk2_ref, o_ref,
                 *, tq, tk, lam_init):
    i = pl.program_id(2)
    t0 = i * tq
    q = q_ref[...]
    qs = (q[:, :HEAD_DIM], q[:, HEAD_DIM:])
    n_full = t0 // tk

    def step(jt, carry, masked):
        k0 = pl.multiple_of(jt * tk, tk)
        k = k_ref[pl.ds(k0, tk), :]
        v = v_ref[pl.ds(k0, tk), :]
        s = [_dot_nt(qs[cc], k[:, cc * HEAD_DIM:(cc + 1) * HEAD_DIM]) for cc in range(2)]
        if masked:
            kpos = k0 + lax.broadcasted_iota(jnp.int32, (tq, tk), 1)
            tpos = t0 + lax.broadcasted_iota(jnp.int32, (tq, tk), 0)
            s = [jnp.where(kpos <= tpos, sc, NEG_INF) for sc in s]
        return tuple(_softmax_update(carry[cc], s[cc], v) for cc in range(2))

    init = (_softmax_init(tq, 2 * HEAD_DIM), _softmax_init(tq, 2 * HEAD_DIM))
    carry = lax.fori_loop(0, n_full, lambda jt, c: step(jt, c, False), init)
    carry = step(n_full, carry, True)

    lam = (jnp.exp(jnp.sum(lq1_ref[...] * lk1_ref[...], axis=-1, keepdims=True))
           - jnp.exp(jnp.sum(lq2_ref[...] * lk2_ref[...], axis=-1, keepdims=True)) + lam_init)
    o0 = carry[0][2] / carry[0][1]
    o1 = carry[1][2] / carry[1][1]
    o = o0 - lam * o1
    o = o * lax.rsqrt(jnp.mean(o * o, axis=-1, keepdims=True) + DIFF_NORM_EPS) * ng_ref[...]
    o = o * (1.0 - lam_init)
    o_ref[...] = (o * _silu(z_ref[...].astype(F32))).astype(BF16)


def _diff_attn(proj, offs, norm_g, lq1, lk1, lq2, lk2, layer_idx, bsz, seq, tq, tk):
    nq = seq // tq
    w = 2 * HEAD_DIM
    lam_init = 0.8 - 0.6 * math.exp(-0.3 * layer_idx)
    assert tk % tq == 0
    vec = pl.BlockSpec((1, HEAD_DIM), lambda b, h, i: (0, 0))
    return pl.pallas_call(
        functools.partial(_diff_kernel, tq=tq, tk=tk, lam_init=lam_init),
        grid=(bsz, DIFF_HEADS, nq),
        in_specs=[pl.BlockSpec((tq, w), lambda b, h, i: (b * nq + i, P_AQ // w + h)),
                  pl.BlockSpec((seq, w), lambda b, h, i: (b, P_AK // w + h)),
                  pl.BlockSpec((seq, w), lambda b, h, i: (b, offs["AV"] // w + h)),
                  pl.BlockSpec((tq, w), lambda b, h, i: (b * nq + i, offs["AZ"] // w + h)),
                  pl.BlockSpec((1, w), lambda b, h, i: (0, 0)),
                  vec, vec, vec, vec],
        out_specs=pl.BlockSpec((tq, w), lambda b, h, i: (b * nq + i, h)),
        out_shape=jax.ShapeDtypeStruct((bsz * seq, BR_WIDTH), BF16),
        compiler_params=_params("arbitrary", "arbitrary", "arbitrary"),
        name="diff_attn",
    )(proj, proj, proj, proj, norm_g.reshape(1, w), lq1.reshape(1, -1), lk1.reshape(1, -1),
      lq2.reshape(1, -1), lk2.reshape(1, -1))


def _sb_kernel(q_ref, k_ref, v_ref, z_ref, o_ref, *, tq, tk):
    i = pl.program_id(2)
    t0 = i * tq
    n_full = t0 // tk
    heads = SB_HEADS_PER_STEP
    hcol = lambda u: slice((u // 2) * HEAD_DIM, (u // 2 + 1) * HEAD_DIM)
    qh = [q_ref[pl.ds((u % 2) * tk, tk), hcol(u)] for u in range(2 * heads)]
    upper2 = jnp.where((lax.broadcasted_iota(jnp.int32, (2 * tk, tk), 0) & (tk - 1))
                       > lax.broadcasted_iota(jnp.int32, (2 * tk, tk), 1), 1.0, 0.0).astype(BF16)
    strict = (lax.broadcasted_iota(jnp.int32, (tk, tk), 1) < lax.broadcasted_iota(jnp.int32, (tk, tk), 0))

    def steps(items):
        kv = []
        for u, jt, _, _ in items:
            k0 = pl.multiple_of(jt * tk, tk)
            kv.append((k_ref[pl.ds(k0, tk), hcol(u)], v_ref[pl.ds(k0, tk), hcol(u)]))
        z = [_dot_nt(qh[it[0]], kv[n][0]) * ATTN_SCALE for n, it in enumerate(items)]
        log_1m = [-(jnp.maximum(zz, 0.0) + jnp.log(1.0 + jnp.exp(-jnp.abs(zz)))) for zz in z]
        log_1m = [jnp.where(strict, lm, 0.0) if it[3] else lm for lm, it in zip(log_1m, items)]
        stacked = []
        for lm in log_1m:
            hi = lm.astype(BF16)
            stacked.append(jnp.concatenate([hi, (lm - hi.astype(F32)).astype(BF16)], axis=1))
        after = [_dot(st, upper2) + it[2][0] for st, it in zip(stacked, items)]
        a = [jnp.exp(zz + lm + af) for zz, lm, af in zip(z, log_1m, after)]
        a = [jnp.where(strict, aa, 0.0) if it[3] else aa for aa, it in zip(a, items)]
        pv = [_dot(aa.astype(BF16), kv[n][1]) for n, aa in enumerate(a)]
        return [(it[2][0] + jnp.sum(lm, axis=-1, keepdims=True), it[2][1] + p)
                for it, lm, p in zip(items, log_1m, pv)]

    zero = (jnp.zeros((tk, 1), F32), jnp.zeros((tk, HEAD_DIM), F32))
    first = steps([it for hd in range(heads)
                   for it in ((2 * hd + 1, n_full + 1, zero, True), (2 * hd, n_full, zero, True))])
    upper = steps([(2 * hd + 1, n_full, first[2 * hd], False) for hd in range(heads)])
    carry = tuple(c for hd in range(heads) for c in (first[2 * hd + 1], upper[hd]))

    def alive(cs):
        return functools.reduce(jnp.maximum, [jnp.max(c[0]) for c in cs])

    def cond(c):
        return (c[0] < n_full) & (c[1] > SB_UNDERFLOW)

    def body(c):
        jt = n_full - 1 - c[0]
        new = tuple(steps([(u, jt, c[2][u], False) for u in range(2 * heads)]))
        return c[0] + 1, alive(new), new

    carry = lax.while_loop(cond, body, (jnp.int32(0), alive(carry), carry))[2]
    zg = z_ref[...].astype(F32)
    for u in range(2 * heads):
        rows = slice((u % 2) * tk, (u % 2 + 1) * tk)
        o_ref[rows, hcol(u)] = (carry[u][1] * _silu(zg[rows, hcol(u)])).astype(BF16)


def _sb_attn(proj, offs, bsz, seq, tq, tk):
    nq = seq // tq
    w = SB_HEADS_PER_STEP * HEAD_DIM
    assert tq == 2 * tk and SB_HEADS % SB_HEADS_PER_STEP == 0
    return pl.pallas_call(
        functools.partial(_sb_kernel, tq=tq, tk=tk),
        grid=(bsz, SB_HEADS // SB_HEADS_PER_STEP, nq),
        in_specs=[pl.BlockSpec((tq, w), lambda b, h, i: (b * nq + i, offs["CQ"] // w + h)),
                  pl.BlockSpec((seq, w), lambda b, h, i: (b, offs["CK"] // w + h)),
                  pl.BlockSpec((seq, w), lambda b, h, i: (b, offs["CV"] // w + h)),
                  pl.BlockSpec((tq, w), lambda b, h, i: (b * nq + i, offs["CZ"] // w + h))],
        out_specs=pl.BlockSpec((tq, w), lambda b, h, i: (b * nq + i, h)),
        out_shape=jax.ShapeDtypeStruct((bsz * seq, BR_WIDTH), BF16),
        compiler_params=_params("arbitrary", "arbitrary", "arbitrary"),
        name="stick_breaking",
    )(proj, proj, proj, proj)


def _compress_kernel(x_ref, pe_ref, w1_ref, w2_ref, o_ref, xs_ref, *, seq):
    nc = seq // CMP_STRIDE
    xs_ref[pl.ds(0, seq), :] = x_ref[...].astype(F32)
    xs_ref[pl.ds(seq, CMP_STRIDE), :] = jnp.zeros((CMP_STRIDE, HEAD_DIM), F32)
    hid = None
    for l in range(CMP_LEN):
        rows = xs_ref[pl.ds(l, nc, stride=CMP_STRIDE), :] + pe_ref[l:l + 1, :]
        t = _dot(rows.astype(BF16), w1_ref[l * HEAD_DIM:(l + 1) * HEAD_DIM, :])
        hid = t if hid is None else hid + t
    o_ref[0, 0] = _dot(_silu(hid).astype(BF16), w2_ref[...]).astype(BF16)


def _compress(proj, col_off, pe, w1, w2, bsz, seq):
    nc = seq // CMP_STRIDE
    w = HEAD_DIM
    return pl.pallas_call(
        functools.partial(_compress_kernel, seq=seq),
        grid=(bsz, NSA_GROUPS),
        in_specs=[pl.BlockSpec((seq, w), lambda b, g: (b, col_off // w + g)),
                  pl.BlockSpec((CMP_LEN, w), lambda b, g: (0, 0)),
                  pl.BlockSpec((CMP_LEN * w, w), lambda b, g: (0, 0)),
                  pl.BlockSpec((w, w), lambda b, g: (0, 0))],
        out_specs=pl.BlockSpec((1, 1, nc, w), lambda b, g: (b, g, 0, 0)),
        out_shape=jax.ShapeDtypeStruct((bsz, NSA_GROUPS, nc, w), BF16),
        scratch_shapes=[pltpu.VMEM((seq + CMP_STRIDE, w), F32)],
        compiler_params=_params("arbitrary", "arbitrary"),
        name="nsa_compress",
    )(proj, pe, w1.astype(BF16), w2.astype(BF16))


def _nsa_kernel(q_ref, kc_ref, vc_ref, ks_ref, vs_ref, kw_ref, vw_ref, g_ref, z_ref, o_ref, imp_ref,
                *, tq, tk, seq):
    i = pl.program_id(2)
    t0 = i * tq
    nc = seq // CMP_STRIDE
    nb = seq // SLC_LEN
    nbp = SLC_BLOCKS_PAD
    rep = NSA_REP
    q = q_ref[...]
    qh = [q[:, r * HEAD_DIM:(r + 1) * HEAD_DIM] for r in range(rep)]

    kc, vc = kc_ref[0, 0], vc_ref[0, 0]
    tpos_c = t0 + lax.broadcasted_iota(jnp.int32, (tq, nc), 0)
    ncol = lax.broadcasted_iota(jnp.int32, (tq, nc), 1)
    cvalid = CMP_STRIDE * ncol + (CMP_LEN - 1) <= tpos_c
    psum = jnp.zeros((tq, nc), F32)
    o_cmp = []
    for r in range(rep):
        s = jnp.where(cvalid, _dot_nt(qh[r], kc), NEG_INF)
        e = jnp.where(cvalid, jnp.exp2(s - jnp.max(s, axis=-1, keepdims=True)), 0.0)
        den = jnp.sum(e, axis=-1, keepdims=True)
        p = e / jnp.where(den > 0.0, den, 1.0)
        psum = psum + p
        o_cmp.append(_dot(p.astype(BF16), vc))

    nb8 = imp_ref.shape[0]
    jrow = lax.broadcasted_iota(jnp.int32, (nb8, nc), 0)
    ncol2 = lax.broadcasted_iota(jnp.int32, (nb8, nc), 1)
    overlap = jnp.where((CMP_STRIDE * ncol2 < SLC_LEN * jrow + SLC_LEN)
                        & (CMP_STRIDE * ncol2 + CMP_LEN > SLC_LEN * jrow)
                        & (ncol2 < nc - 1) & (jrow < nb), 1.0, 0.0).astype(BF16)
    p1 = psum.astype(BF16)
    r1 = psum - p1.astype(F32)
    p2 = r1.astype(BF16)
    p3 = (r1 - p2.astype(F32)).astype(BF16)
    imp = _dot_nt(overlap, p1) + _dot_nt(overlap, p2) + _dot_nt(overlap, p3)
    jt_ = lax.broadcasted_iota(jnp.int32, (nb8, tq), 0)
    tblk = jnp.right_shift(t0 + lax.broadcasted_iota(jnp.int32, (nb8, tq), 1), SLC_SHIFT)
    imp = jnp.where((jt_ == tblk) | (jt_ == 0), FORCE_SCORE, imp)
    imp = jnp.where(jt_ <= tblk, imp, NEG_INF)
    imp_ref[...] = imp

    def rank_body(jp, cnt):
        row = imp_ref[pl.ds(jp, 1), :]
        tie = jnp.where(jp < jt_, 1.0, 0.0)
        return cnt + jnp.where(row > imp, 1.0, jnp.where(row == imp, tie, 0.0))

    n_live = jnp.minimum((t0 + tq - 1) // SLC_LEN + 1, nb)
    cnt = lax.fori_loop(0, n_live, rank_body, jnp.zeros((nb8, tq), F32))
    keep = (cnt < float(min(SLC_TOPK, nb))) & (imp > 0.5 * NEG_INF)
    unsel_t = jnp.where(keep, 0.0, 1.0)
    if nbp > nb8:
        unsel_t = jnp.concatenate([unsel_t, jnp.zeros((nbp - nb8, tq), F32)], axis=0)
    unselected = unsel_t.T.astype(BF16)

    rows = rep * tq
    q_sel = jnp.concatenate([jnp.concatenate([qh[r], unselected], axis=1) for r in range(rep)], axis=0)

    def slc_step(jt, carry, masked):
        k0 = pl.multiple_of(jt * tk, tk)
        k = ks_ref[pl.ds(k0, tk), :]
        v = vs_ref[pl.ds(k0, tk), :]
        kblk = jnp.right_shift(k0 + lax.broadcasted_iota(jnp.int32, (tk, nbp), 0), SLC_SHIFT)
        bias = jnp.where(lax.broadcasted_iota(jnp.int32, (tk, nbp), 1) == kblk,
                         -UNSELECTED_BIAS, 0.0).astype(BF16)
        s = _dot_nt(q_sel, jnp.concatenate([k, bias], axis=1))
        if masked:
            causal = (k0 + lax.broadcasted_iota(jnp.int32, (tq, tk), 1)
                      <= t0 + lax.broadcasted_iota(jnp.int32, (tq, tk), 0))
            s = s + jnp.concatenate([jnp.where(causal, 0.0, NEG_INF)] * rep, axis=0)
        return _softmax_update(carry, s, v)

    n_full = t0 // tk
    carry = lax.fori_loop(0, n_full, lambda jt, c: slc_step(jt, c, False), _softmax_init(rows, HEAD_DIM))
    m_s, l_s, acc_s = slc_step(n_full, carry, True)
    o_slc = acc_s / l_s

    tw = min(tq, WINDOW_ROWS)
    span = min(WINDOW + tw, seq)
    o_win = []
    for w in range(tq // tw):
        r0 = t0 + w * tw
        start = pl.multiple_of(jnp.maximum(r0 - WINDOW, 0), tw)
        kpos = start + lax.broadcasted_iota(jnp.int32, (tw, span), 1)
        row_w = r0 + lax.broadcasted_iota(jnp.int32, (tw, span), 0)
        in_window = jnp.where(kpos <= row_w, jnp.where(kpos > row_w - WINDOW, 0.0, NEG_INF), NEG_INF)
        q_w = jnp.concatenate([qh[r][w * tw:(w + 1) * tw] for r in range(rep)], axis=0)
        s = _dot_nt(q_w, kw_ref[pl.ds(start, span), :]) + jnp.concatenate([in_window] * rep, axis=0)
        p = jnp.exp2(s - jnp.max(s, axis=-1, keepdims=True))
        o_win.append(_dot(p.astype(BF16), vw_ref[pl.ds(start, span), :]) / jnp.sum(p, axis=-1, keepdims=True))

    gates = _sigmoid(g_ref[...].astype(F32))
    z = z_ref[...].astype(F32)
    for r in range(rep):
        rs = slice(r * tq, (r + 1) * tq)
        cs = slice(r * HEAD_DIM, (r + 1) * HEAD_DIM)
        o_win_r = jnp.concatenate([o[r * tw:(r + 1) * tw] for o in o_win], axis=0)
        o = (gates[:, 3 * r:3 * r + 1] * o_cmp[r] + gates[:, 3 * r + 1:3 * r + 2] * o_slc[rs]
             + gates[:, 3 * r + 2:3 * r + 3] * o_win_r)
        o_ref[:, cs] = (o * _silu(z[:, cs])).astype(BF16)


def _nsa_attn(proj, offs, kcmp, vcmp, bsz, seq, tq, tk):
    nq = seq // tq
    nc = seq // CMP_STRIDE
    w = HEAD_DIM
    gw = NSA_REP * HEAD_DIM
    assert seq // SLC_LEN <= SLC_BLOCKS_PAD and tq & (tq - 1) == 0 and tk % tq == 0
    kv = lambda off: pl.BlockSpec((seq, w), lambda b, g, i: (b, off // w + g))
    cmp_spec = pl.BlockSpec((1, 1, nc, w), lambda b, g, i: (b, g, 0, 0))
    return pl.pallas_call(
        functools.partial(_nsa_kernel, tq=tq, tk=tk, seq=seq),
        grid=(bsz, NSA_GROUPS, nq),
        in_specs=[pl.BlockSpec((tq, gw), lambda b, g, i: (b * nq + i, P_NQ // gw + g)),
                  cmp_spec, cmp_spec,
                  kv(P_NKS), kv(offs["NVS"]), kv(P_NKW), kv(offs["NVW"]),
                  pl.BlockSpec((tq, LANES), lambda b, g, i: (b * nq + i, offs["NG"] // LANES + g)),
                  pl.BlockSpec((tq, gw), lambda b, g, i: (b * nq + i, offs["NZ"] // gw + g))],
        out_specs=pl.BlockSpec((tq, gw), lambda b, g, i: (b * nq + i, g)),
        out_shape=jax.ShapeDtypeStruct((bsz * seq, BR_WIDTH), BF16),
        scratch_shapes=[pltpu.VMEM((-(-(seq // SLC_LEN) // 8) * 8, tq), F32)],
        compiler_params=_params("arbitrary", "arbitrary", "arbitrary"),
        name="nsa_attn",
    )(proj, kcmp, vcmp, proj, proj, proj, proj, proj, proj)


def _merge_kernel(x_ref, ya_ref, yb_ref, yc_ref, mg0_ref, mg1_ref, mg2_ref, wb_ref, wo_ref, gp_ref,
                  gate_ref, *rest, emit_h):
    merged = None
    for n, (y_ref, mg_ref) in enumerate(((ya_ref, mg0_ref), (yb_ref, mg1_ref), (yc_ref, mg2_ref))):
        t = _dot(y_ref[...], wb_ref[n]) * _sigmoid(mg_ref[...].astype(F32))
        merged = t if merged is None else merged + t
    o = _dot(merged.astype(BF16), wo_ref[...])
    o = o * lax.rsqrt(jnp.mean(o * o, axis=-1, keepdims=True) + EPS) * gp_ref[...]
    x_new = x_ref[...] + gate_ref[0] * o
    if emit_h:
        g_next_ref, sc_next_ref, sh_next_ref, o_ref, h_ref = rest
        h_ref[...] = _modulated_norm(x_new, g_next_ref[...], sc_next_ref[0], sh_next_ref[0])
    else:
        (o_ref,) = rest
    o_ref[...] = x_new


def _merge(xf, ya, yb, yc, proj, wb_all, wo_all, layer, g_post, gate, next_norm, seq, tm):
    rows, d = xf.shape
    tpb = seq // tm
    row = lambda w_: pl.BlockSpec((tm, w_), lambda i: (i, 0))
    mg = lambda n: pl.BlockSpec((tm, d), lambda i: (i, P_MG // d + n))
    vec = pl.BlockSpec((1, d), lambda i: (0, 0))
    mod_spec = pl.BlockSpec((1, 1, d), lambda i: (i // tpb, 0, 0))
    const = pl.Buffered(1)
    emit_h = next_norm is not None
    in_specs = [row(d), row(BR_WIDTH), row(BR_WIDTH), row(BR_WIDTH), mg(0), mg(1), mg(2),
                pl.BlockSpec((None, N_BRANCH, BR_WIDTH, d), lambda i: (layer, 0, 0, 0), pipeline_mode=const),
                pl.BlockSpec((None, d, d), lambda i: (layer, 0, 0), pipeline_mode=const), vec, mod_spec]
    args = [xf, ya, yb, yc, proj, proj, proj, wb_all, wo_all, g_post.reshape(1, d),
            gate[:, None, :]]
    out_specs, out_shape = row(d), jax.ShapeDtypeStruct((rows, d), F32)
    if emit_h:
        g_next, sc_next, sh_next = next_norm
        in_specs += [vec, mod_spec, mod_spec]
        args += [g_next.reshape(1, d), sc_next[:, None, :], sh_next[:, None, :]]
        out_specs, out_shape = [out_specs, row(d)], [out_shape, jax.ShapeDtypeStruct((rows, d), BF16)]
    out = pl.pallas_call(
        functools.partial(_merge_kernel, emit_h=emit_h),
        grid=(rows // tm,),
        in_specs=in_specs,
        out_specs=out_specs,
        out_shape=out_shape,
        compiler_params=_params("arbitrary"),
        name="merge_out",
    )(*args)
    return (out[0], out[1]) if emit_h else (out, None)


def _tiles(seq, d_model):
    return dict(tm_in=min(1024, seq), tn_in=2048, tm_merge=min(256, seq),
                diff=(min(1024, seq), min(1024, seq)),
                sb=(min(512, seq), min(512, seq) // 2),
                nsa=(min(512, seq), min(512, seq)))


def kernel(x, c, norm_pre_g, norm_post_g, w_ada, b_ada, w_in, lambda_q1, lambda_k1, lambda_q2,
           lambda_k2, diff_norm_g, cmp_pe_k, cmp_w1_k, cmp_w2_k, cmp_pe_v, cmp_w1_v, cmp_w2_v,
           w_branch, w_out):
    bsz, seq, d = x.shape
    depth = w_in.shape[0]
    t = _tiles(seq, d)
    offs = _rest_offsets(d)
    assert d % 512 == 0 and ROPE_COLS % d == 0 and seq % t["tm_in"] == 0 and seq % 128 == 0
    n_total = -(-offs["END"] // t["tn_in"]) * t["tn_in"]
    tables = _rope_tables(seq)
    mod = _ada(c, w_ada, b_ada)
    xf = x.reshape(bsz * seq, d)
    mods = [jnp.split(mod[l], 3, axis=-1) for l in range(depth)]
    h = _prenorm(xf, norm_pre_g[0], mods[0][1], mods[0][0], seq, t["tm_merge"])
    wb_all, wo_all = w_branch.astype(BF16), w_out.astype(BF16)
    wp_all = _prep_w_in(w_in, d, n_total)
    for l in range(depth):
        proj = _inproj(h, tables, wp_all, l, seq, t["tm_in"], t["tn_in"])
        ya = _diff_attn(proj, offs, diff_norm_g[l], lambda_q1[l], lambda_k1[l], lambda_q2[l],
                        lambda_k2[l], l, bsz, seq, *t["diff"])
        perm = jnp.asarray(ROPE_PERM)
        w1_k = cmp_w1_k[l].reshape(CMP_LEN, HEAD_DIM, HEAD_DIM)[:, perm, :].reshape(CMP_LEN * HEAD_DIM, HEAD_DIM)
        kcmp = _compress(proj, P_NKC, cmp_pe_k[l][:, perm], w1_k, cmp_w2_k[l][:, perm], bsz, seq)
        vcmp = _compress(proj, offs["NVC"], cmp_pe_v[l], cmp_w1_v[l], cmp_w2_v[l], bsz, seq)
        yb = _nsa_attn(proj, offs, kcmp, vcmp, bsz, seq, *t["nsa"])
        yc = _sb_attn(proj, offs, bsz, seq, *t["sb"])
        next_norm = (norm_pre_g[l + 1], mods[l + 1][1], mods[l + 1][0]) if l + 1 < depth else None
        xf, h = _merge(xf, ya, yb, yc, proj, wb_all, wo_all, l, norm_post_g[l], mods[l][2],
                       next_norm, seq, t["tm_merge"])
    return xf.reshape(bsz, seq, d)
```
